```python
import math
import jax, jax.numpy as jnp
from jax import lax
import numpy as np

D_MODEL = 2048
BATCH = 8
SEQ = 4096
DEPTH = 4

CTX_LEN = 256
GRID_W = 64
N_MIXERS = 3
MIXER_OF_LAYER = tuple(i % N_MIXERS for i in range(DEPTH))
N_A = MIXER_OF_LAYER.count(0)
N_B = MIXER_OF_LAYER.count(1)
N_C = MIXER_OF_LAYER.count(2)
LAST_CTX_LAYER = max([i for i in range(DEPTH) if MIXER_OF_LAYER[i] == 1], default=-1)
N_MOD = 9
D_FF = 5632
CONV_W = 3
DA_HEADS = 8
DA_DK = D_MODEL // (2 * DA_HEADS)
DA_DV = 2 * DA_DK
Q_BLOCK = 128
ROPE_BASE = 10000.0
CHUNK = 128
GM_GROUPS = 8
GM_WIDTH = D_MODEL
DEEPNORM_ALPHA = (2.0 * DEPTH) ** 0.25
DEEPNORM_BETA = (8.0 * DEPTH) ** -0.25
LN_EPS = 1e-5
ADA_INIT = 0.5

kernel_name = 'hybrid_dit_shortconv_diffattn_gmlp'


def layer_norm(x, g, b):
    xf = x.astype(jnp.float32)
    mu = jnp.mean(xf, -1, keepdims=True)
    var = jnp.mean(jnp.square(xf - mu), -1, keepdims=True)
    return ((xf - mu) * lax.rsqrt(var + LN_EPS)).astype(x.dtype) * g + b


def rms_norm(x, g):
    xf = x.astype(jnp.float32)
    return (xf * lax.rsqrt(jnp.mean(jnp.square(xf), -1, keepdims=True) + LN_EPS)).astype(x.dtype) * g


def post_norm(x, delta, g, b):
    return layer_norm(DEEPNORM_ALPHA * x + delta, g, b)


def modulate(h, shift, scale):
    return h * (1 + scale) + shift


def ada_modulation(cond, w, b):
    mods = jax.nn.silu(cond) @ w + b
    return [m[:, None, :] for m in jnp.split(mods, N_MOD, axis=-1)]


def swiglu(a, wg, wu, wd):
    return (jax.nn.silu(a @ wg) * (a @ wu)) @ wd


def macaron_ffn(h, shift, scale, gate, wg, wu, wd, g, b):
    return post_norm(h, 0.5 * gate * swiglu(modulate(h, shift, scale), wg, wu, wd), g, b)


def conv3_centred(z, w):
    zp = jnp.pad(z, [(0, 0)] * (z.ndim - 2) + [(1, 1), (0, 0)])
    return w[0] * zp[..., :-2, :] + w[1] * zp[..., 1:-1, :] + w[2] * zp[..., 2:, :]


def short_conv(m, w_in, w_conv, w_out):
    bg, cg, v = jnp.split(m @ w_in, 3, axis=-1)
    return (bg * conv3_centred(cg * v, w_conv)) @ w_out


def chunk_gmlp(m, w_in, ln_g, ln_b, w_s, b_s, w_out):
    n, L, _ = m.shape
    u, v = jnp.split(jax.nn.gelu(m @ w_in, approximate=False), 2, axis=-1)
    v = layer_norm(v, ln_g, ln_b)
    vg = v.reshape(n, L // CHUNK, CHUNK, GM_GROUPS, GM_WIDTH // GM_GROUPS)
    s = jnp.einsum('gpq,bnqgc->bnpgc', w_s, vg) + b_s.T[:, :, None]
    return (u * s.reshape(n, L, GM_WIDTH)) @ w_out


def axial_rope_tables(n_tok, dtype):
    t = jnp.arange(n_tok)
    n_freq = DA_DK // 4
    inv = ROPE_BASE ** (-jnp.arange(n_freq, dtype=jnp.float32) / n_freq)

    def tab(pos):
        ang = (pos.astype(jnp.float32)[:, None] * inv)[:, None, None, :]
        return jnp.cos(ang).astype(dtype), jnp.sin(ang).astype(dtype)

    cr, sr = tab(t // GRID_W)
    cc, sc = tab(t % GRID_W)
    return cr, sr, cc, sc


def rope_rotate(x, cos, sin):
    x1, x2 = jnp.split(x, 2, axis=-1)
    return jnp.concatenate([x1 * cos - x2 * sin, x2 * cos + x1 * sin], axis=-1)


def axial_rope(x, tables):
    cr, sr, cc, sc = tables
    half = x.shape[-1] // 2
    return jnp.concatenate([rope_rotate(x[..., :half], cr, sr), rope_rotate(x[..., half:], cc, sc)], axis=-1)


def heads_qk(p):
    return jnp.transpose(p.reshape(p.shape[:-1] + (DA_HEADS, 2, DA_DK)), (0, 2, 3, 1, 4))


def heads_v(p):
    return jnp.transpose(p.reshape(p.shape[:-1] + (DA_HEADS, DA_DV)), (0, 2, 1, 3))


def diff_lambda(lp, lam_init):
    lp = lp.astype(jnp.float32)
    return jnp.exp(jnp.sum(lp[0] * lp[1])) - jnp.exp(jnp.sum(lp[2] * lp[3])) + lam_init


def diff_attend(q, k, v, lam):
    s = jnp.einsum('bhmqd,bhmkd->bhmqk', q, k).astype(jnp.float32) * (DA_DK ** -0.5)
    p = jax.nn.softmax(s, axis=-1)
    a = p[:, :, 0] - lam * p[:, :, 1]
    return jnp.einsum('bhqk,bhkd->bhqd', a.astype(v.dtype), v)


def diff_attend_blocked(q, keys, vals, lam):
    n, H, _, T, dk = q.shape
    nb = T // Q_BLOCK
    qb = jnp.moveaxis(q.reshape(n, H, 2, nb, Q_BLOCK, dk), 3, 0)
    ob = lax.map(lambda blk: diff_attend(blk, keys, vals, lam), qb)
    return jnp.moveaxis(ob, 0, 2).reshape(n, H, T, DA_DV)


def diff_merge(o, subln, lam_init, w_o):
    o = rms_norm(o, subln) * (1.0 - lam_init)
    n, H, T, dv = o.shape
    return jnp.transpose(o, (0, 2, 1, 3)).reshape(n, T, H * dv) @ w_o


def _normal(key, shape, scale):
    return jax.random.normal(key, shape, jnp.float32) * scale


def setup_inputs(seed: int = 0) -> dict:
    key = jax.random.key(seed)
    ks = jax.random.split(key, 32)
    D, F = D_MODEL, D_FF
    sd = D ** -0.5
    return {
        'x': _normal(ks[0], (BATCH, SEQ, D), 1.0),
        'c': _normal(ks[1], (BATCH, D), 1.0),
        'ctx': _normal(ks[2], (BATCH, CTX_LEN, D), 1.0),
        'c_ctx': _normal(ks[3], (D,), 1.0),
        'ada_w': _normal(ks[4], (DEPTH, D, N_MOD * D), ADA_INIT * sd),
        'ada_b': _normal(ks[5], (DEPTH, N_MOD * D), 0.02),
        'ln_g': 1.0 + _normal(ks[6], (DEPTH, 3, D), 0.02),
        'ln_b': _normal(ks[7], (DEPTH, 3, D), 0.02),
        'ffn_wg': _normal(ks[8], (DEPTH, 2, D, F), sd),
        'ffn_wu': _normal(ks[9], (DEPTH, 2, D, F), sd),
        'ffn_wd': _normal(ks[10], (DEPTH, 2, F, D), DEEPNORM_BETA * F ** -0.5),
        'sc_w_in': _normal(ks[11], (N_A, D, 3 * D), sd),
        'sc_conv': _normal(ks[12], (N_A, CONV_W, D), CONV_W ** -0.5),
        'sc_w_out': _normal(ks[13], (N_A, D, D), DEEPNORM_BETA * sd),
        'da_w_qkv': jnp.concatenate([_normal(ks[14], (N_B, D, 2 * D), sd),
                                     _normal(ks[15], (N_B, D, D), DEEPNORM_BETA * sd)], axis=-1),
        'da_lambda': _normal(ks[16], (N_B, 4, DA_DK), 0.1),
        'da_subln': 1.0 + _normal(ks[17], (N_B, DA_DV), 0.02),
        'da_w_o': _normal(ks[18], (N_B, D, D), DEEPNORM_BETA * sd),
        'gm_w_in': _normal(ks[19], (N_C, D, 2 * GM_WIDTH), sd),
        'gm_ln_g': 1.0 + _normal(ks[20], (N_C, GM_WIDTH), 0.02),
        'gm_ln_b': _normal(ks[21], (N_C, GM_WIDTH), 0.02),
        'gm_w_s': _normal(ks[22], (N_C, GM_GROUPS, CHUNK, CHUNK), CHUNK ** -0.5),
        'gm_b_s': 1.0 + _normal(ks[23], (N_C, GM_GROUPS, CHUNK), 0.02),
        'gm_w_out': _normal(ks[24], (N_C, GM_WIDTH, D), DEEPNORM_BETA * GM_WIDTH ** -0.5),
    }


def reference(x, c, ctx, c_ctx, ada_w, ada_b, ln_g, ln_b, ffn_wg, ffn_wu, ffn_wd,
              sc_w_in, sc_conv, sc_w_out, da_w_qkv, da_lambda, da_subln, da_w_o,
              gm_w_in, gm_ln_g, gm_ln_b, gm_w_s, gm_b_s, gm_w_out):
    n, n_tok, D = x.shape
    rows = n_tok // GRID_W
    rope = axial_rope_tables(n_tok, x.dtype)
    h, hc = x, ctx
    for i in range(DEPTH):
        kind = MIXER_OF_LAYER[i]
        j = i // N_MIXERS
        ctx_in = i <= LAST_CTX_LAYER
        ctx_out = i < LAST_CTX_LAYER
        md = ada_modulation(c, ada_w[i], ada_b[i])
        h = macaron_ffn(h, md[0], md[1], md[2], ffn_wg[i, 0], ffn_wu[i, 0], ffn_wd[i, 0], ln_g[i, 0], ln_b[i, 0])
        m = modulate(h, md[3], md[4])
        if ctx_in:
            mdc = ada_modulation(c_ctx[None, :], ada_w[i], ada_b[i])
            hc = macaron_ffn(hc, mdc[0], mdc[1], mdc[2], ffn_wg[i, 0], ffn_wu[i, 0], ffn_wd[i, 0], ln_g[i, 0], ln_b[i, 0])
            mc = modulate(hc, mdc[3], mdc[4])
        if kind == 0:
            y = short_conv(m.reshape(n, rows, GRID_W, D), sc_w_in[j], sc_conv[j], sc_w_out[j]).reshape(n, n_tok, D)
            if ctx_out:
                yc = short_conv(mc, sc_w_in[j], sc_conv[j], sc_w_out[j])
        elif kind == 1:
            lam_init = 0.8 - 0.6 * math.exp(-0.3 * i)
            lam = diff_lambda(da_lambda[j], lam_init)
            q, k, v = jnp.split(m @ da_w_qkv[j], 3, axis=-1)
            q = jnp.transpose(axial_rope(q.reshape(n, n_tok, DA_HEADS, 2, DA_DK), rope), (0, 2, 3, 1, 4))
            k = jnp.transpose(axial_rope(k.reshape(n, n_tok, DA_HEADS, 2, DA_DK), rope), (0, 2, 3, 1, 4))
            v = heads_v(v)
            kc, vc = jnp.split(mc @ da_w_qkv[j][:, D_MODEL:], 2, axis=-1)
            kc, vc = heads_qk(kc), heads_v(vc)
            keys = jnp.concatenate([k, kc], axis=3)
            vals = jnp.concatenate([v, vc], axis=2)
            y = diff_merge(diff_attend_blocked(q, keys, vals, lam), da_subln[j], lam_init, da_w_o[j])
            if ctx_out:
                qc = heads_qk(mc @ da_w_qkv[j][:, :D_MODEL])
                yc = diff_merge(diff_attend(qc, kc, vc, lam), da_subln[j], lam_init, da_w_o[j])
        else:
            y = chunk_gmlp(m, gm_w_in[j], gm_ln_g[j], gm_ln_b[j], gm_w_s[j], gm_b_s[j], gm_w_out[j])
            if ctx_out:
                yc = chunk_gmlp(mc, gm_w_in[j], gm_ln_g[j], gm_ln_b[j], gm_w_s[j], gm_b_s[j], gm_w_out[j])
        h = post_norm(h, md[5] * y, ln_g[i, 1], ln_b[i, 1])
        h = macaron_ffn(h, md[6], md[7], md[8], ffn_wg[i, 1], ffn_wu[i, 1], ffn_wd[i, 1], ln_g[i, 2], ln_b[i, 2])
        if ctx_out:
            hc = post_norm(hc, mdc[5] * yc, ln_g[i, 1], ln_b[i, 1])
            hc = macaron_ffn(hc, mdc[6], mdc[7], mdc[8], ffn_wg[i, 1], ffn_wu[i, 1], ffn_wd[i, 1], ln_g[i, 2], ln_b[i, 2])
    return h
```

```python
import functools
import math

import jax
import jax.numpy as jnp
from jax import lax
from jax.experimental import pallas as pl
from jax.experimental.pallas import tpu as pltpu

GRID_W = 64
CHUNK = 128
N_MOD = 9
N_MIXERS = 3
ROPE_BASE = 10000.0
LN_EPS = 1e-5

V7X_LANES = 128
V7X_VMEM_BYTES = 64 * 1024 * 1024
V7X_VMEM_CAP = V7X_VMEM_BYTES - 6 * 1024 * 1024

BF16 = jnp.bfloat16
F32 = jnp.float32


def _params(semantics, vmem_estimate):
    limit = min(V7X_VMEM_CAP, max(32 * 1024 * 1024, int(vmem_estimate * 1.3)))
    return pltpu.CompilerParams(dimension_semantics=semantics, vmem_limit_bytes=limit)


def _pick(n, candidates):
    for c in candidates:
        if n % c == 0:
            return c
    return n


def _mod_row(mods_ref, k):
    return mods_ref[0, k:k + 1, :]


def _modulate_bf16(h, mods_ref, k_shift):
    shift = _mod_row(mods_ref, k_shift)
    scale = _mod_row(mods_ref, k_shift + 1)
    return (h * (1.0 + scale) + shift).astype(BF16)


def _layer_norm(x, g, b):
    mu = jnp.mean(x, axis=-1, keepdims=True)
    xc = x - mu
    var = jnp.mean(xc * xc, axis=-1, keepdims=True)
    return xc * lax.rsqrt(var + LN_EPS) * g + b


def _silu(x):
    return x / (1.0 + jnp.exp(-x))


def _gelu(x):
    return 0.5 * x * (1.0 + lax.erf(x * math.sqrt(0.5)))


def _dot(a, b):
    return jnp.dot(a, b, preferred_element_type=F32)


def _ada_kernel(cond_ref, w_ref, b_ref, o_ref):
    a = _silu(cond_ref[...]).astype(BF16)
    o_ref[...] = _dot(a, w_ref[...].astype(BF16)) + b_ref[...]


def _ada(cond, w, b):
    m, d = cond.shape
    n = w.shape[1]
    tn = _pick(n, (1024, 512, 256, 128))
    est = 2 * d * tn * 4 + d * tn * 2 + 4 * m * (d + tn) * 4
    return pl.pallas_call(
        _ada_kernel,
        grid=(n // tn,),
        in_specs=[pl.BlockSpec((m, d), lambda j: (0, 0)),
                  pl.BlockSpec((d, tn), lambda j: (0, j)),
                  pl.BlockSpec((1, tn), lambda j: (0, j))],
        out_specs=pl.BlockSpec((m, tn), lambda j: (0, j)),
        out_shape=jax.ShapeDtypeStruct((m, n), F32),
        compiler_params=_params(("parallel",), est),
        name="ada_mod",
    )(cond, w, b.reshape(1, n))


def _fused_kernel(kind, k_shift, k_gate, res_scale, alpha, period, *refs):
    h_ref, mods_ref = refs[0], refs[1]
    lng_ref, lnb_ref, o_ref, a_ref, acc_ref = refs[-5:]
    w = refs[2:-5]
    j = pl.program_id(1)

    @pl.when(j == 0)
    def _():
        a_ref[...] = _modulate_bf16(h_ref[...], mods_ref, k_shift)
        acc_ref[...] = jnp.zeros_like(acc_ref)

    a = a_ref[...]
    if kind == "ffn":
        wg_ref, wu_ref, wd_ref = w
        hid = _silu(_dot(a, wg_ref[...])) * _dot(a, wu_ref[...])
    elif kind == "conv":
        wb_ref, wc_ref, wv_ref, cw_ref, wd_ref = w
        z = _dot(a, wc_ref[...]) * _dot(a, wv_ref[...])
        tm = z.shape[0]
        pos = lax.broadcasted_iota(jnp.int32, z.shape, 0) & (period - 1)
        z_prev = jnp.where(pos == 0, 0.0, pltpu.roll(z, 1, 0))
        z_next = jnp.where(pos == period - 1, 0.0, pltpu.roll(z, tm - 1, 0))
        cw = cw_ref[...]
        conv = cw[0:1, :] * z_prev + cw[1:2, :] * z + cw[2:3, :] * z_next
        hid = _dot(a, wb_ref[...]) * conv
    else:
        wu_ref, vn_ref, ws_ref, bs_ref, wd_ref = w
        u = _gelu(_dot(a, wu_ref[...]))
        ws = ws_ref[0]
        bs = bs_ref[0]
        tm = u.shape[0]
        parts = []
        for c in range(tm // CHUNK):
            rows = slice(c * CHUNK, (c + 1) * CHUNK)
            s = _dot(ws, vn_ref[rows, :]) + bs
            parts.append(u[rows, :] * s)
        hid = jnp.concatenate(parts, axis=0)
    acc_ref[...] += _dot(hid.astype(BF16), wd_ref[...])

    @pl.when(j == pl.num_programs(1) - 1)
    def _():
        gate = _mod_row(mods_ref, k_gate)
        y = alpha * h_ref[...] + (res_scale * gate) * acc_ref[...]
        o_ref[...] = _layer_norm(y, lng_ref[...], lnb_ref[...])


def _fused_call(kind, h, mods, weights, ln_g, ln_b, *, k_shift, k_gate, res_scale, alpha,
                period=GRID_W, vn=None):
    n, d = h.shape
    nb = mods.shape[0]
    tm = _pick(n // nb, (512, 256, 128))
    n_tiles = n // tm
    tiles_per_mod = n_tiles // nb

    row = lambda i, j: (i, 0)
    col_blk = lambda i, j: (0, j)
    row_blk = lambda i, j: (j, 0)
    common_in = [pl.BlockSpec((tm, d), row),
                 pl.BlockSpec((1, N_MOD, d), lambda i, j: (i // tiles_per_mod, 0, 0))]
    if kind == "ffn":
        wg, wu, wd = weights
        f = wg.shape[1]
        tc = _pick(f, (512, 256, 128))
        n_chunks = f // tc
        w_in = [pl.BlockSpec((d, tc), col_blk), pl.BlockSpec((d, tc), col_blk),
                pl.BlockSpec((tc, d), row_blk)]
        w_args = [wg, wu, wd]
        w_bytes = 3 * d * tc * 2
    elif kind == "conv":
        w_in3, cw, wd = weights
        tc = _pick(d, (512, 256, 128))
        n_chunks = d // tc
        w_in = [pl.BlockSpec((d, tc), lambda i, j: (0, j)),
                pl.BlockSpec((d, tc), lambda i, j: (0, n_chunks + j)),
                pl.BlockSpec((d, tc), lambda i, j: (0, 2 * n_chunks + j)),
                pl.BlockSpec((3, tc), col_blk),
                pl.BlockSpec((tc, d), row_blk)]
        w_args = [w_in3, w_in3, w_in3, cw, wd]
        w_bytes = 4 * d * tc * 2
    else:
        wu, ws, bs, wd = weights
        groups = ws.shape[0]
        tc = d // groups
        n_chunks = groups
        w_in = [pl.BlockSpec((d, tc), col_blk),
                pl.BlockSpec((tm, tc), lambda i, j: (i, j)),
                pl.BlockSpec((1, CHUNK, CHUNK), lambda i, j: (j, 0, 0)),
                pl.BlockSpec((1, CHUNK, 1), lambda i, j: (j, 0, 0)),
                pl.BlockSpec((tc, d), row_blk)]
        w_args = [wu, vn, ws, bs, wd]
        w_bytes = 2 * d * tc * 2 + tm * tc * 2
    vec = pl.BlockSpec((1, d), lambda i, j: (0, 0))
    est = (2 * 2 * tm * d * 4
           + tm * d * (4 + 2)
           + 2 * w_bytes
           + 6 * tm * tc * 4
           + 2 * tm * d * 4)
    kern = functools.partial(_fused_kernel, kind, k_shift, k_gate, res_scale, alpha, period)
    return pl.pallas_call(
        kern,
        grid=(n_tiles, n_chunks),
        in_specs=common_in + w_in + [vec, vec],
        out_specs=pl.BlockSpec((tm, d), row),
        out_shape=jax.ShapeDtypeStruct((n, d), F32),
        scratch_shapes=[pltpu.VMEM((tm, d), BF16), pltpu.VMEM((tm, d), F32)],
        compiler_params=_params(("parallel", "arbitrary"), est),
        name="fused_" + kind,
    )(h, mods, *w_args, ln_g.reshape(1, d), ln_b.reshape(1, d))


def _qkv_kernel(k_shift, n_rope, h_ref, mods_ref, w_ref, cos_ref, sin_ref, o_ref, a_ref):
    s = pl.program_id(1)

    @pl.when(s == 0)
    def _():
        a_ref[...] = _modulate_bf16(h_ref[...], mods_ref, k_shift)

    def project(rope):
        width = w_ref.shape[1]
        slab = _pick(width, (512, 256, 128))
        for c in range(width // slab):
            y = _dot(a_ref[...], w_ref[:, c * slab:(c + 1) * slab])
            for r in range(slab // V7X_LANES):
                yr = y[:, r * V7X_LANES:(r + 1) * V7X_LANES]
                if rope:
                    yr = yr * cos_ref[...] + pltpu.roll(yr, V7X_LANES // 2, 1) * sin_ref[...]
                lo = c * slab + r * V7X_LANES
                o_ref[:, lo:lo + V7X_LANES] = yr.astype(o_ref.dtype)

    if n_rope > 0:
        @pl.when(s < n_rope)
        def _():
            project(True)

        @pl.when(s >= n_rope)
        def _():
            project(False)
    else:
        project(False)


def _qkv_call(h, mods, w, cos_t, sin_t, *, k_shift, n_rope, seq):
    n, d = h.shape
    nb = mods.shape[0]
    n_sec = w.shape[1] // d
    tm = _pick(min(n // nb, seq), (512, 256, 128))
    n_tiles = n // tm
    tiles_per_mod = n_tiles // nb
    tiles_per_seq = seq // tm
    est = 2 * tm * d * 4 + tm * d * 2 + 2 * d * d * 2 + 2 * tm * d * 2 + 8 * tm * V7X_LANES * 4
    kern = functools.partial(_qkv_kernel, k_shift, n_rope)
    return pl.pallas_call(
        kern,
        grid=(n_tiles, n_sec),
        in_specs=[pl.BlockSpec((tm, d), lambda i, s: (i, 0)),
                  pl.BlockSpec((1, N_MOD, d), lambda i, s: (i // tiles_per_mod, 0, 0)),
                  pl.BlockSpec((d, d), lambda i, s: (0, s)),
                  pl.BlockSpec((tm, V7X_LANES), lambda i, s: (i % tiles_per_seq, 0)),
                  pl.BlockSpec((tm, V7X_LANES), lambda i, s: (i % tiles_per_seq, 0))],
        out_specs=pl.BlockSpec((tm, d), lambda i, s: (i, s)),
        out_shape=jax.ShapeDtypeStruct((n, n_sec * d), BF16),
        scratch_shapes=[pltpu.VMEM((tm, d), BF16)],
        compiler_params=_params(("parallel", "arbitrary"), est),
        name="qkv_proj",
    )(h, mods, w, cos_t, sin_t)


def _gate_branch_kernel(k_shift, h_ref, mods_ref, w_ref, g_ref, b_ref, o_ref):
    a = _modulate_bf16(h_ref[...], mods_ref, k_shift)
    v = _gelu(_dot(a, w_ref[...]))
    o_ref[...] = _layer_norm(v, g_ref[...], b_ref[...]).astype(o_ref.dtype)


def _gate_branch_call(h, mods, w, g, b, *, k_shift):
    n, d = h.shape
    nb = mods.shape[0]
    width = w.shape[1]
    tm = _pick(n // nb, (512, 256, 128))
    n_tiles = n // tm
    tiles_per_mod = n_tiles // nb
    est = 2 * tm * d * 4 + 2 * d * width * 2 + 2 * tm * width * 2 + 4 * tm * width * 4
    return pl.pallas_call(
        functools.partial(_gate_branch_kernel, k_shift),
        grid=(n_tiles,),
        in_specs=[pl.BlockSpec((tm, d), lambda i: (i, 0)),
                  pl.BlockSpec((1, N_MOD, d), lambda i: (i // tiles_per_mod, 0, 0)),
                  pl.BlockSpec((d, width), lambda i: (0, 0)),
                  pl.BlockSpec((1, width), lambda i: (0, 0)),
                  pl.BlockSpec((1, width), lambda i: (0, 0))],
        out_specs=pl.BlockSpec((tm, width), lambda i: (i, 0)),
        out_shape=jax.ShapeDtypeStruct((n, width), BF16),
        compiler_params=_params(("parallel",), est),
        name="gmlp_gate_branch",
    )(h, mods, w, g.reshape(1, width), b.reshape(1, width))


def _attn_kernel(lam_init, tk, lam_ref, q_ref, k_ref, v_ref, kc_ref, vc_ref, subln_ref, o_ref,
                 acc_ref):
    dk = q_ref.shape[1] // 2
    tq = q_ref.shape[0]
    scale = dk ** -0.5
    q = q_ref[...]
    qm = (q[:, :dk], q[:, dk:])
    nt = (((1,), (1,)), ((), ()))

    def update(carry, k_blk, v_blk):
        new = []
        for mp in range(2):
            m_old, l_old = carry[2 * mp], carry[2 * mp + 1]
            s = lax.dot_general(qm[mp], k_blk[:, mp * dk:(mp + 1) * dk], nt,
                                preferred_element_type=F32) * scale
            m_new = jnp.maximum(m_old, jnp.max(s, axis=-1, keepdims=True))
            p = jnp.exp(s - m_new)
            corr = jnp.exp(m_old - m_new)
            l_new = corr * l_old + jnp.sum(p, axis=-1, keepdims=True)
            acc_ref[mp] = corr * acc_ref[mp] + _dot(p.astype(BF16), v_blk)
            new += [m_new, l_new]
        return tuple(new)

    acc_ref[...] = jnp.zeros_like(acc_ref)
    neg = jnp.full((tq, 1), -jnp.inf, F32)
    zero = jnp.zeros((tq, 1), F32)

    def body(c, carry):
        start = pl.multiple_of(c * tk, tk)
        return update(carry, k_ref[pl.ds(start, tk), :], v_ref[pl.ds(start, tk), :])

    carry = lax.fori_loop(0, k_ref.shape[0] // tk, body, (neg, zero, neg, zero))
    _, l0, _, l1 = update(carry, kc_ref[...], vc_ref[...])

    lp = lam_ref[...]
    lam = (jnp.exp(jnp.sum(lp[0:1, :] * lp[1:2, :], axis=-1, keepdims=True))
           - jnp.exp(jnp.sum(lp[2:3, :] * lp[3:4, :], axis=-1, keepdims=True)) + lam_init)
    o = acc_ref[0] / l0 - lam * (acc_ref[1] / l1)
    o = o * lax.rsqrt(jnp.mean(o * o, axis=-1, keepdims=True) + LN_EPS) * subln_ref[...]
    o_ref[...] = (o * (1.0 - lam_init)).astype(o_ref.dtype)


def _attn_call(qkv, kvc, lam_p, subln, *, n_batch, seq, ctx_len, heads, lam_init):
    d = qkv.shape[1] // 3
    dv = d // heads
    tq = _pick(seq, (512, 256, 128))
    tk = _pick(seq, (512, 256, 128))
    nq = seq // tq
    est = (2 * 2 * tq * dv * 2 + 2 * 2 * seq * dv * 2 + 2 * 2 * ctx_len * dv * 2
           + 2 * tq * dv * 4 + 8 * tq * tk * 4)
    kern = functools.partial(_attn_kernel, lam_init, tk)
    return pl.pallas_call(
        kern,
        grid=(n_batch, heads, nq),
        in_specs=[pl.BlockSpec(lam_p.shape, lambda b, h, i: (0, 0)),
                  pl.BlockSpec((tq, dv), lambda b, h, i: (b * nq + i, h)),
                  pl.BlockSpec((seq, dv), lambda b, h, i: (b, heads + h)),
                  pl.BlockSpec((seq, dv), lambda b, h, i: (b, 2 * heads + h)),
                  pl.BlockSpec((ctx_len, dv), lambda b, h, i: (b, h)),
                  pl.BlockSpec((ctx_len, dv), lambda b, h, i: (b, heads + h)),
                  pl.BlockSpec((1, dv), lambda b, h, i: (0, 0))],
        out_specs=pl.BlockSpec((tq, dv), lambda b, h, i: (b * nq + i, h)),
        out_shape=jax.ShapeDtypeStruct((n_batch * seq, d), BF16),
        scratch_shapes=[pltpu.VMEM((2, tq, dv), F32)],
        compiler_params=_params(("parallel", "parallel", "arbitrary"), est),
        name="diff_attention",
    )(lam_p, qkv, qkv, qkv, kvc, kvc, subln.reshape(1, dv))


def _out_proj_kernel(k_gate, alpha, y_ref, h_ref, mods_ref, w_ref, g_ref, b_ref, o_ref):
    y = _dot(y_ref[...], w_ref[...])
    x = alpha * h_ref[...] + _mod_row(mods_ref, k_gate) * y
    o_ref[...] = _layer_norm(x, g_ref[...], b_ref[...])


def _out_proj_call(y, h, mods, w, g, b, *, k_gate, alpha):
    n, d = h.shape
    nb = mods.shape[0]
    tm = _pick(n // nb, (512, 256, 128))
    n_tiles = n // tm
    tiles_per_mod = n_tiles // nb
    est = 2 * tm * d * 2 + 2 * 2 * tm * d * 4 + 2 * d * d * 2 + 4 * tm * d * 4
    return pl.pallas_call(
        functools.partial(_out_proj_kernel, k_gate, alpha),
        grid=(n_tiles,),
        in_specs=[pl.BlockSpec((tm, d), lambda i: (i, 0)),
                  pl.BlockSpec((tm, d), lambda i: (i, 0)),
                  pl.BlockSpec((1, N_MOD, d), lambda i: (i // tiles_per_mod, 0, 0)),
                  pl.BlockSpec((d, d), lambda i: (0, 0)),
                  pl.BlockSpec((1, d), lambda i: (0, 0)),
                  pl.BlockSpec((1, d), lambda i: (0, 0))],
        out_specs=pl.BlockSpec((tm, d), lambda i: (i, 0)),
        out_shape=jax.ShapeDtypeStruct((n, d), F32),
        compiler_params=_params(("parallel",), est),
        name="out_proj_norm",
    )(y, h, mods, w, g.reshape(1, d), b.reshape(1, d))


def _rope_tables(seq, dk):
    n_freq = dk // 4
    t = jnp.arange(seq)
    inv = ROPE_BASE ** (-jnp.arange(n_freq, dtype=F32) / n_freq)
    ang_r = (t // GRID_W).astype(F32)[:, None] * inv
    ang_c = (t % GRID_W).astype(F32)[:, None] * inv
    cos_t = jnp.concatenate([jnp.cos(ang_r), jnp.cos(ang_c)] * 2, axis=-1)
    sin_t = jnp.concatenate([-jnp.sin(ang_r), -jnp.sin(ang_c), jnp.sin(ang_r), jnp.sin(ang_c)], axis=-1)
    return cos_t, sin_t


def _rope_column_layout(w, dk):
    rows, width = w.shape
    w = w.reshape(rows, width // dk, 2, 2, dk // 4)
    return jnp.swapaxes(w, 2, 3).reshape(rows, width)


def kernel(x, c, ctx, c_ctx, ada_w, ada_b, ln_g, ln_b, ffn_wg, ffn_wu, ffn_wd, sc_w_in, sc_conv,
           sc_w_out, da_w_qkv, da_lambda, da_subln, da_w_o, gm_w_in, gm_ln_g, gm_ln_b, gm_w_s,
           gm_b_s, gm_w_out):
    n_batch, seq, d = x.shape
    ctx_len = ctx.shape[1]
    depth = ada_w.shape[0]
    mixer_of_layer = tuple(i % N_MIXERS for i in range(depth))
    last_ctx_layer = max([i for i in range(depth) if mixer_of_layer[i] == 1], default=-1)
    alpha = (2.0 * depth) ** 0.25
    dv = da_subln.shape[-1]
    heads = d // dv
    dk = dv // 2

    h = x.reshape(n_batch * seq, d)
    hc = ctx.reshape(n_batch * ctx_len, d)

    n_cond = n_batch + 1
    cond = jnp.zeros((16 * ((n_cond + 15) // 16), d), F32)
    cond = cond.at[:n_batch].set(c).at[n_batch].set(c_ctx)

    cos_t, sin_t = _rope_tables(seq, dk)
    zero_tab = jnp.zeros((ctx_len, V7X_LANES), F32)

    for i in range(depth):
        kind = mixer_of_layer[i]
        j = i // N_MIXERS
        ctx_in = i <= last_ctx_layer
        ctx_out = i < last_ctx_layer
        mods_all = _ada(cond, ada_w[i], ada_b[i]).reshape(-1, N_MOD, d)
        md = mods_all[:n_batch]
        mdc = mods_all[n_batch:n_batch + 1]

        def ffn(hh, mm, half, k0):
            wts = (ffn_wg[i, half].astype(BF16), ffn_wu[i, half].astype(BF16),
                   ffn_wd[i, half].astype(BF16))
            return _fused_call("ffn", hh, mm, wts, ln_g[i, 2 * half], ln_b[i, 2 * half],
                               k_shift=k0, k_gate=k0 + 2, res_scale=0.5, alpha=alpha)

        h = ffn(h, md, 0, 0)
        if ctx_in:
            hc = ffn(hc, mdc, 0, 0)

        if kind == 0:
            wts = (sc_w_in[j].astype(BF16), sc_conv[j], sc_w_out[j].astype(BF16))
            conv = functools.partial(_fused_call, "conv", weights=wts, ln_g=ln_g[i, 1],
                                     ln_b=ln_b[i, 1], k_shift=3, k_gate=5, res_scale=1.0,
                                     alpha=alpha)
            h = conv(h, md, period=GRID_W)
            if ctx_out:
                hc = conv(hc, mdc, period=ctx_len)
        elif kind == 1:
            lam_init = 0.8 - 0.6 * math.exp(-0.3 * i)
            w_qkv = da_w_qkv[j].astype(BF16)
            w_qkv = jnp.concatenate([_rope_column_layout(w_qkv[:, :2 * d], dk), w_qkv[:, 2 * d:]],
                                    axis=1)
            qkv = _qkv_call(h, md, w_qkv, cos_t, sin_t, k_shift=3, n_rope=2, seq=seq)
            kvc = _qkv_call(hc, mdc, w_qkv[:, d:], zero_tab, zero_tab, k_shift=3, n_rope=0,
                            seq=ctx_len)
            o = _attn_call(qkv, kvc, da_lambda[j], da_subln[j], n_batch=n_batch, seq=seq,
                           ctx_len=ctx_len, heads=heads, lam_init=lam_init)
            h = _out_proj_call(o, h, md, da_w_o[j].astype(BF16), ln_g[i, 1], ln_b[i, 1],
                               k_gate=5, alpha=alpha)
            assert not ctx_out, "context-side attention output is not implemented"
        else:
            w_in = gm_w_in[j].astype(BF16)
            gw = gm_w_out.shape[1]
            wts = (w_in[:, :gw], gm_w_s[j].astype(BF16), gm_b_s[j][:, :, None],
                   gm_w_out[j].astype(BF16))

            def gmlp(hh, mm):
                vn = _gate_branch_call(hh, mm, w_in[:, gw:], gm_ln_g[j], gm_ln_b[j], k_shift=3)
                return _fused_call("gmlp", hh, mm, wts, ln_g[i, 1], ln_b[i, 1], k_shift=3,
                                   k_gate=5, res_scale=1.0, alpha=alpha, vn=vn)

            h = gmlp(h, md)
            if ctx_out:
                hc = gmlp(hc, mdc)

        h = ffn(h, md, 1, 6)
        if ctx_out:
            hc = ffn(hc, mdc, 1, 6)
    return h.reshape(n_batch, seq, d)
```

```python
import functools
import math

import jax
import jax.numpy as jnp
from jax import lax
from jax.experimental import pallas as pl
from jax.experimental.pallas import tpu as pltpu

GRID_W = 64
CHUNK = 128
N_MOD = 9
N_MIXERS = 3
ROPE_BASE = 10000.0
LN_EPS = 1e-5

V7X_LANES = 128
V7X_VMEM_BYTES = 64 * 1024 * 1024
V7X_VMEM_CAP = V7X_VMEM_BYTES - 6 * 1024 * 1024

BF16 = jnp.bfloat16
F32 = jnp.float32

ATTN_TQ = 512
ATTN_GROUP = 4


def _params(semantics, vmem_estimate):
    limit = min(V7X_VMEM_CAP, max(32 * 1024 * 1024, int(vmem_estimate * 1.3)))
    return pltpu.CompilerParams(dimension_semantics=semantics, vmem_limit_bytes=limit)


def _pick(n, candidates):
    for c in candidates:
        if n % c == 0:
            return c
    return n


def _mod_row(mods_ref, k):
    return mods_ref[0, k:k + 1, :]


def _modulate_bf16(h, mods_ref, k_shift):
    shift = _mod_row(mods_ref, k_shift)
    scale = _mod_row(mods_ref, k_shift + 1)
    return (h * (1.0 + scale) + shift).astype(BF16)


def _layer_norm(x, g, b):
    mu = jnp.mean(x, axis=-1, keepdims=True)
    xc = x - mu
    var = jnp.mean(xc * xc, axis=-1, keepdims=True)
    return xc * lax.rsqrt(var + LN_EPS) * g + b


def _silu(x):
    return x / (1.0 + jnp.exp(-x))


def _gelu(x):
    return 0.5 * x * (1.0 + lax.erf(x * math.sqrt(0.5)))


def _dot(a, b):
    return jnp.dot(a, b, preferred_element_type=F32)


def _ada_kernel(cond_ref, w_ref, b_ref, o_ref):
    a = _silu(cond_ref[...]).astype(BF16)
    o_ref[...] = _dot(a, w_ref[...].astype(BF16)) + b_ref[...]


def _ada(cond, w, b, layer):
    m, d = cond.shape
    n = w.shape[2]
    tn = _pick(n, (1024, 512, 256, 128))
    est = 2 * d * tn * 4 + d * tn * 2 + 4 * m * (d + tn) * 4
    return pl.pallas_call(
        _ada_kernel,
        grid=(n // tn,),
        in_specs=[pl.BlockSpec((m, d), lambda j: (0, 0)),
                  pl.BlockSpec((None, d, tn), lambda j: (layer, 0, j)),
                  pl.BlockSpec((None, 1, tn), lambda j: (layer, 0, j))],
        out_specs=pl.BlockSpec((m, tn), lambda j: (0, j)),
        out_shape=jax.ShapeDtypeStruct((m, n), F32),
        compiler_params=_params(("parallel",), est),
        name="ada_mod",
    )(cond, w, b.reshape(b.shape[0], 1, n))


def _fused_kernel(kind, k_shift, k_gate, res_scale, alpha, period, *refs):
    h_ref, mods_ref = refs[0], refs[1]
    lng_ref, lnb_ref, o_ref, a_ref, acc_ref = refs[-5:]
    w = refs[2:-5]
    j = pl.program_id(1)

    @pl.when(j == 0)
    def _():
        a_ref[...] = _modulate_bf16(h_ref[...], mods_ref, k_shift)
        acc_ref[...] = jnp.zeros_like(acc_ref)

    a = a_ref[...]
    if kind == "ffn":
        wg_ref, wu_ref, wd_ref = w
        hid = _silu(_dot(a, wg_ref[...])) * _dot(a, wu_ref[...])
    elif kind == "conv":
        wb_ref, wc_ref, wv_ref, cw_ref, wd_ref = w
        z = _dot(a, wc_ref[...]) * _dot(a, wv_ref[...])
        tm = z.shape[0]
        pos = lax.broadcasted_iota(jnp.int32, z.shape, 0) & (period - 1)
        z_prev = jnp.where(pos == 0, 0.0, pltpu.roll(z, 1, 0))
        z_next = jnp.where(pos == period - 1, 0.0, pltpu.roll(z, tm - 1, 0))
        cw = cw_ref[...]
        conv = cw[0:1, :] * z_prev + cw[1:2, :] * z + cw[2:3, :] * z_next
        hid = _dot(a, wb_ref[...]) * conv
    else:
        wu_ref, vn_ref, ws_ref, bs_ref, wd_ref = w
        u = _gelu(_dot(a, wu_ref[...]))
        tm = u.shape[0]
        gw = u.shape[1] // ws_ref.shape[0]
        cols = []
        for g in range(ws_ref.shape[0]):
            ws = ws_ref[g]
            bs = bs_ref[g]
            rows = [_dot(ws, vn_ref[c * CHUNK:(c + 1) * CHUNK, g * gw:(g + 1) * gw]) + bs
                    for c in range(tm // CHUNK)]
            cols.append(jnp.concatenate(rows, axis=0))
        hid = u * jnp.concatenate(cols, axis=1)
    acc_ref[...] += _dot(hid.astype(BF16), wd_ref[...])

    @pl.when(j == pl.num_programs(1) - 1)
    def _():
        gate = _mod_row(mods_ref, k_gate)
        y = alpha * h_ref[...] + (res_scale * gate) * acc_ref[...]
        o_ref[...] = _layer_norm(y, lng_ref[...], lnb_ref[...])


def _lead_spec(lead, block, index_fn):
    lead = tuple(lead)
    return pl.BlockSpec((None,) * len(lead) + tuple(block),
                        lambda *g: lead + tuple(index_fn(*g)))


def _fused_call(kind, h, mods, weights, lead, ln_g, ln_b, *, k_shift, k_gate, res_scale, alpha,
                period=GRID_W, vn=None):
    n, d = h.shape
    nb = mods.shape[0]
    tm = _pick(n // nb, (512, 256, 128))
    n_tiles = n // tm
    tiles_per_mod = n_tiles // nb

    row = lambda i, j: (i, 0)
    col_blk = lambda i, j: (0, j)
    row_blk = lambda i, j: (j, 0)
    common_in = [pl.BlockSpec((tm, d), row),
                 pl.BlockSpec((1, N_MOD, d), lambda i, j: (i // tiles_per_mod, 0, 0))]
    if kind == "ffn":
        wg, wu, wd = weights
        f = wg.shape[-1]
        tc = _pick(f, (512, 256, 128))
        n_chunks = f // tc
        w_in = [_lead_spec(lead, (d, tc), col_blk), _lead_spec(lead, (d, tc), col_blk),
                _lead_spec(lead, (tc, d), row_blk)]
        w_args = [wg, wu, wd]
        w_bytes = 3 * d * tc * 2
    elif kind == "conv":
        w_in3, cw, wd = weights
        tc = _pick(d, (512, 256, 128))
        n_chunks = d // tc
        w_in = [_lead_spec(lead, (d, tc), lambda i, j: (0, j)),
                _lead_spec(lead, (d, tc), lambda i, j: (0, n_chunks + j)),
                _lead_spec(lead, (d, tc), lambda i, j: (0, 2 * n_chunks + j)),
                _lead_spec(lead, (3, tc), col_blk),
                _lead_spec(lead, (tc, d), row_blk)]
        w_args = [w_in3, w_in3, w_in3, cw, wd]
        w_bytes = 4 * d * tc * 2
    else:
        w_in2, ws, bs, wd = weights
        groups = ws.shape[-3]
        gw = wd.shape[-2] // groups
        per_step = max(g for g in range(1, groups + 1) if groups % g == 0 and g * gw <= 1024)
        tc = per_step * gw
        n_chunks = groups // per_step
        w_in = [_lead_spec(lead, (d, tc), col_blk),
                pl.BlockSpec((tm, tc), lambda i, j: (i, j)),
                _lead_spec(lead, (per_step, CHUNK, CHUNK), lambda i, j: (j, 0, 0)),
                _lead_spec(lead, (per_step, CHUNK, 1), lambda i, j: (j, 0, 0)),
                _lead_spec(lead, (tc, d), row_blk)]
        w_args = [w_in2, vn, ws, bs, wd]
        w_bytes = 2 * d * tc * 2 + tm * tc * 2
    vec = pl.BlockSpec((1, d), lambda i, j: (0, 0))
    est = (2 * 2 * tm * d * 4
           + tm * d * (4 + 2)
           + 2 * w_bytes
           + 6 * tm * tc * 4
           + 2 * tm * d * 4)
    kern = functools.partial(_fused_kernel, kind, k_shift, k_gate, res_scale, alpha, period)
    return pl.pallas_call(
        kern,
        grid=(n_tiles, n_chunks),
        in_specs=common_in + w_in + [vec, vec],
        out_specs=pl.BlockSpec((tm, d), row),
        out_shape=jax.ShapeDtypeStruct((n, d), F32),
        scratch_shapes=[pltpu.VMEM((tm, d), BF16), pltpu.VMEM((tm, d), F32)],
        compiler_params=_params(("parallel", "arbitrary"), est),
        name="fused_" + kind,
    )(h, mods, *w_args, ln_g.reshape(1, d), ln_b.reshape(1, d))


def _qkv_kernel(k_shift, n_rope, h_ref, mods_ref, w_ref, tab_ref, o_ref, vt_ref, a_ref):
    s = pl.program_id(1)
    n_sec = pl.num_programs(1)
    cos_ref, sin_ref = tab_ref.at[0], tab_ref.at[1]

    @pl.when(s == 0)
    def _():
        a_ref[...] = _modulate_bf16(h_ref[...], mods_ref, k_shift)

    def project(rope, transposed):
        width = w_ref.shape[1]
        slab = _pick(width, (512, 256, 128))
        for c in range(width // slab):
            y = _dot(a_ref[...], w_ref[:, c * slab:(c + 1) * slab])
            if transposed:
                vt_ref[c * slab:(c + 1) * slab, :] = y.T.astype(vt_ref.dtype)
                continue
            for r in range(slab // V7X_LANES):
                yr = y[:, r * V7X_LANES:(r + 1) * V7X_LANES]
                if rope:
                    yr = yr * cos_ref[...] + pltpu.roll(yr, V7X_LANES // 2, 1) * sin_ref[...]
                lo = c * slab + r * V7X_LANES
                o_ref[:, lo:lo + V7X_LANES] = yr.astype(o_ref.dtype)

    if n_rope > 0:
        @pl.when(s < n_rope)
        def _():
            project(True, False)

    @pl.when(jnp.logical_and(s >= n_rope, s < n_sec - 1))
    def _():
        project(False, False)

    @pl.when(s == n_sec - 1)
    def _():
        project(False, True)


def _qkv_call(h, mods, w, lead, tabs, *, k_shift, n_rope, seq, first_sec=0):
    n, d = h.shape
    n_tab = tabs.shape[0]
    nb = mods.shape[0]
    n_sec = w.shape[-1] // d - first_sec
    tm = _pick(min(n // nb, seq), (512, 256, 128))
    n_tiles = n // tm
    tiles_per_mod = n_tiles // nb
    tiles_per_seq = seq // tm
    est = (2 * tm * d * 4 + tm * d * 2 + 2 * d * d * 2 + 2 * 2 * tm * d * 2
           + 8 * tm * V7X_LANES * 4 + 4 * tm * 512 * 4)
    kern = functools.partial(_qkv_kernel, k_shift, n_rope)
    return pl.pallas_call(
        kern,
        grid=(n_tiles, n_sec),
        in_specs=[pl.BlockSpec((tm, d), lambda i, s: (i, 0)),
                  pl.BlockSpec((1, N_MOD, d), lambda i, s: (i // tiles_per_mod, 0, 0)),
                  _lead_spec(lead, (d, d), lambda i, s: (0, first_sec + s)),
                  pl.BlockSpec((None, 2, tm, V7X_LANES),
                               lambda i, s: (jnp.minimum(s, n_tab - 1), 0, i % tiles_per_seq, 0))],
        out_specs=[pl.BlockSpec((tm, d), lambda i, s: (i, jnp.minimum(s, n_sec - 2))),
                   pl.BlockSpec((None, d, tm), lambda i, s: (i, 0, 0))],
        out_shape=[jax.ShapeDtypeStruct((n, (n_sec - 1) * d), BF16),
                   jax.ShapeDtypeStruct((n_tiles, d, tm), BF16)],
        scratch_shapes=[pltpu.VMEM((tm, d), BF16)],
        compiler_params=_params(("parallel", "arbitrary"), est),
        name="qkv_proj",
    )(h, mods, w, tabs)


def _gate_branch_kernel(k_shift, h_ref, mods_ref, w_ref, g_ref, b_ref, o_ref):
    a = _modulate_bf16(h_ref[...], mods_ref, k_shift)
    v = _gelu(_dot(a, w_ref[...]))
    o_ref[...] = _layer_norm(v, g_ref[...], b_ref[...]).astype(o_ref.dtype)


def _gate_branch_call(h, mods, w, lead, g, b, *, k_shift):
    n, d = h.shape
    nb = mods.shape[0]
    width = w.shape[-1] // 2
    tm = _pick(n // nb, (512, 256, 128))
    n_tiles = n // tm
    tiles_per_mod = n_tiles // nb
    est = 2 * tm * d * 4 + 2 * d * width * 2 + 2 * tm * width * 2 + 4 * tm * width * 4
    return pl.pallas_call(
        functools.partial(_gate_branch_kernel, k_shift),
        grid=(n_tiles,),
        in_specs=[pl.BlockSpec((tm, d), lambda i: (i, 0)),
                  pl.BlockSpec((1, N_MOD, d), lambda i: (i // tiles_per_mod, 0, 0)),
                  _lead_spec(lead, (d, width), lambda i: (0, 1)),
                  pl.BlockSpec((1, width), lambda i: (0, 0)),
                  pl.BlockSpec((1, width), lambda i: (0, 0))],
        out_specs=pl.BlockSpec((tm, width), lambda i: (i, 0)),
        out_shape=jax.ShapeDtypeStruct((n, width), BF16),
        compiler_params=_params(("parallel",), est),
        name="gmlp_gate_branch",
    )(h, mods, w, g.reshape(1, width), b.reshape(1, width))


def _attn_kernel(lam_init, lam_ref, q_ref, k_ref, vt_ref, kc_ref, vct_ref, subln_ref, o_ref,
                 acc_ref, s_ref):
    dk = q_ref.shape[1] // 2
    tq = q_ref.shape[0]
    tk = vt_ref.shape[2]
    q = q_ref[...]
    qm = (q[:, :dk], q[:, dk:])
    nt = (((1,), (1,)), ((), ()))

    def update(carry, blocks):
        for g, (k_blk, _) in enumerate(blocks):
            for mp in range(2):
                s_ref[g, mp, 0:k_blk.shape[0], :] = lax.dot_general(
                    k_blk[:, mp * dk:(mp + 1) * dk], qm[mp], nt, preferred_element_type=F32)
        carry = list(carry)
        for g, (k_blk, vt_blk) in enumerate(blocks):
            for mp in range(2):
                m_old, l_old = carry[2 * mp], carry[2 * mp + 1]
                st = s_ref[g, mp, 0:k_blk.shape[0], :]
                m_new = jnp.maximum(m_old, jnp.max(st, axis=0, keepdims=True))
                pt = jnp.exp2(st - m_new)
                corr = jnp.exp2(m_old - m_new)
                carry[2 * mp] = m_new
                carry[2 * mp + 1] = corr * l_old + jnp.sum(pt, axis=0, keepdims=True)
                acc_ref[mp] = corr * acc_ref[mp] + _dot(vt_blk, pt.astype(BF16))
        return tuple(carry)

    acc_ref[...] = jnp.zeros_like(acc_ref)
    neg = jnp.full((1, tq), -jnp.inf, F32)
    zero = jnp.zeros((1, tq), F32)
    group = s_ref.shape[0]

    def body(c, carry):
        blocks = []
        for g in range(group):
            start = pl.multiple_of((c * group + g) * tk, tk)
            blocks.append((k_ref[pl.ds(start, tk), :], vt_ref[c * group + g]))
        return update(carry, blocks)

    carry = lax.fori_loop(0, vt_ref.shape[0] // group, body, (neg, zero, neg, zero))
    tc = vct_ref.shape[2]
    carry = update(carry, [(kc_ref[c * tc:(c + 1) * tc, :], vct_ref[c])
                           for c in range(vct_ref.shape[0])])
    _, l0, _, l1 = carry

    lp = lam_ref[...]
    lam = (jnp.exp(jnp.sum(lp[0:1, :] * lp[1:2, :], axis=-1, keepdims=True))
           - jnp.exp(jnp.sum(lp[2:3, :] * lp[3:4, :], axis=-1, keepdims=True)) + lam_init)
    ot = acc_ref[0] / l0 - lam * (acc_ref[1] / l1)
    ot = ot * lax.rsqrt(jnp.mean(ot * ot, axis=0, keepdims=True) + LN_EPS) * subln_ref[...]
    o_ref[...] = (ot * (1.0 - lam_init)).T.astype(o_ref.dtype)


def _attn_call(qk, vt, kc, vct, lam_p, subln, *, n_batch, seq, ctx_len, heads, lam_init):
    d = qk.shape[1] // 2
    dv = d // heads
    tk = vt.shape[2]
    tc = vct.shape[2]
    nk = seq // tk
    nc = ctx_len // tc
    group = _pick(nk, (ATTN_GROUP, 1))
    assert nc <= group and tc <= tk, "context keys must fit one score-scratch group"
    tq = _pick(seq, (ATTN_TQ, 128))
    nq = seq // tq
    est = (2 * 2 * tq * dv * 2 + 2 * 2 * seq * dv * 2 + 2 * 2 * ctx_len * dv * 2
           + 2 * tq * dv * 4 + 8 * tq * tk * 4)
    kern = functools.partial(_attn_kernel, lam_init)
    return pl.pallas_call(
        kern,
        grid=(n_batch, heads, nq),
        in_specs=[pl.BlockSpec(lam_p.shape, lambda b, h, i: (0, 0)),
                  pl.BlockSpec((tq, dv), lambda b, h, i: (b * nq + i, h)),
                  pl.BlockSpec((seq, dv), lambda b, h, i: (b, heads + h)),
                  pl.BlockSpec((nk, dv, tk), lambda b, h, i: (b, h, 0)),
                  pl.BlockSpec((ctx_len, dv), lambda b, h, i: (b, h)),
                  pl.BlockSpec((nc, dv, tc), lambda b, h, i: (b, h, 0)),
                  pl.BlockSpec((dv, 1), lambda b, h, i: (0, 0))],
        out_specs=pl.BlockSpec((tq, dv), lambda b, h, i: (b * nq + i, h)),
        out_shape=jax.ShapeDtypeStruct((n_batch * seq, d), BF16),
        scratch_shapes=[pltpu.VMEM((2, dv, tq), F32), pltpu.VMEM((group, 2, tk, tq), F32)],
        compiler_params=_params(("parallel", "parallel", "arbitrary"), est),
        name="diff_attention",
    )(lam_p, qk, qk, vt, kc, vct, subln.reshape(dv, 1))


def _out_proj_kernel(k_gate, alpha, y_ref, h_ref, mods_ref, w_ref, g_ref, b_ref, o_ref):
    y = _dot(y_ref[...], w_ref[...])
    x = alpha * h_ref[...] + _mod_row(mods_ref, k_gate) * y
    o_ref[...] = _layer_norm(x, g_ref[...], b_ref[...])


def _out_proj_call(y, h, mods, w, lead, g, b, *, k_gate, alpha):
    n, d = h.shape
    nb = mods.shape[0]
    tm = _pick(n // nb, (512, 256, 128))
    n_tiles = n // tm
    tiles_per_mod = n_tiles // nb
    est = 2 * tm * d * 2 + 2 * 2 * tm * d * 4 + 2 * d * d * 2 + 4 * tm * d * 4
    return pl.pallas_call(
        functools.partial(_out_proj_kernel, k_gate, alpha),
        grid=(n_tiles,),
        in_specs=[pl.BlockSpec((tm, d), lambda i: (i, 0)),
                  pl.BlockSpec((tm, d), lambda i: (i, 0)),
                  pl.BlockSpec((1, N_MOD, d), lambda i: (i // tiles_per_mod, 0, 0)),
                  _lead_spec(lead, (d, d), lambda i: (0, 0)),
                  pl.BlockSpec((1, d), lambda i: (0, 0)),
                  pl.BlockSpec((1, d), lambda i: (0, 0))],
        out_specs=pl.BlockSpec((tm, d), lambda i: (i, 0)),
        out_shape=jax.ShapeDtypeStruct((n, d), F32),
        compiler_params=_params(("parallel",), est),
        name="out_proj_norm",
    )(y, h, mods, w, g.reshape(1, d), b.reshape(1, d))


def _rope_tables(seq, dk):
    n_freq = dk // 4
    t = jnp.arange(seq)
    inv = ROPE_BASE ** (-jnp.arange(n_freq, dtype=F32) / n_freq)
    ang_r = (t // GRID_W).astype(F32)[:, None] * inv
    ang_c = (t % GRID_W).astype(F32)[:, None] * inv
    cos_t = jnp.concatenate([jnp.cos(ang_r), jnp.cos(ang_c)] * 2, axis=-1)
    sin_t = jnp.concatenate([-jnp.sin(ang_r), -jnp.sin(ang_c), jnp.sin(ang_r), jnp.sin(ang_c)], axis=-1)
    k_tab = jnp.stack([cos_t, sin_t])
    return jnp.stack([k_tab * (dk ** -0.5 * math.log2(math.e)), k_tab])


def _rope_column_layout(w, dk):
    rows, width = w.shape
    w = w.reshape(rows, width // dk, 2, 2, dk // 4)
    return jnp.swapaxes(w, 2, 3).reshape(rows, width)


def kernel(x, c, ctx, c_ctx, ada_w, ada_b, ln_g, ln_b, ffn_wg, ffn_wu, ffn_wd, sc_w_in, sc_conv,
           sc_w_out, da_w_qkv, da_lambda, da_subln, da_w_o, gm_w_in, gm_ln_g, gm_ln_b, gm_w_s,
           gm_b_s, gm_w_out):
    n_batch, seq, d = x.shape
    ctx_len = ctx.shape[1]
    depth = ada_w.shape[0]
    mixer_of_layer = tuple(i % N_MIXERS for i in range(depth))
    last_ctx_layer = max([i for i in range(depth) if mixer_of_layer[i] == 1], default=-1)
    alpha = (2.0 * depth) ** 0.25
    dv = da_subln.shape[-1]
    heads = d // dv
    dk = dv // 2

    h = x.reshape(n_batch * seq, d)
    hc = ctx.reshape(n_batch * ctx_len, d)

    n_cond = n_batch + 1
    cond = jnp.zeros((16 * ((n_cond + 15) // 16), d), F32)
    cond = cond.at[:n_batch].set(c).at[n_batch].set(c_ctx)

    rope_tabs = _rope_tables(seq, dk)
    no_rope_tabs = jnp.zeros((1, 2, ctx_len, dk), F32)

    ffn_w = (ffn_wg.astype(BF16), ffn_wu.astype(BF16), ffn_wd.astype(BF16))
    conv_w = (sc_w_in.astype(BF16), sc_conv, sc_w_out.astype(BF16))
    qkv_w = da_w_qkv.astype(BF16)
    qkv_w = jnp.concatenate(
        [_rope_column_layout(qkv_w[..., :2 * d].reshape(-1, 2 * d), dk).reshape(qkv_w.shape[0], d, 2 * d),
         qkv_w[..., 2 * d:]], axis=-1)
    attn_wo = da_w_o.astype(BF16)
    gmlp_w = (gm_w_in.astype(BF16), gm_w_s.astype(BF16), gm_b_s[..., None], gm_w_out.astype(BF16))

    for i in range(depth):
        kind = mixer_of_layer[i]
        j = i // N_MIXERS
        ctx_in = i <= last_ctx_layer
        ctx_out = i < last_ctx_layer
        mods_all = _ada(cond, ada_w, ada_b, i).reshape(-1, N_MOD, d)
        md = mods_all[:n_batch]
        mdc = mods_all[n_batch:n_batch + 1]

        def ffn(hh, mm, half, k0):
            return _fused_call("ffn", hh, mm, ffn_w, (i, half), ln_g[i, 2 * half],
                               ln_b[i, 2 * half], k_shift=k0, k_gate=k0 + 2, res_scale=0.5,
                               alpha=alpha)

        h = ffn(h, md, 0, 0)
        if ctx_in:
            hc = ffn(hc, mdc, 0, 0)

        if kind == 0:
            conv = functools.partial(_fused_call, "conv", weights=conv_w, lead=(j,),
                                     ln_g=ln_g[i, 1], ln_b=ln_b[i, 1], k_shift=3, k_gate=5,
                                     res_scale=1.0, alpha=alpha)
            h = conv(h, md, period=GRID_W)
            if ctx_out:
                hc = conv(hc, mdc, period=ctx_len)
        elif kind == 1:
            lam_init = 0.8 - 0.6 * math.exp(-0.3 * i)
            qk, vt = _qkv_call(h, md, qkv_w, (j,), rope_tabs, k_shift=3, n_rope=2, seq=seq)
            kc, vct = _qkv_call(hc, mdc, qkv_w, (j,), no_rope_tabs, k_shift=3, n_rope=0,
                                seq=ctx_len, first_sec=1)
            o = _attn_call(qk, vt, kc, vct, da_lambda[j], da_subln[j], n_batch=n_batch, seq=seq,
                           ctx_len=ctx_len, heads=heads, lam_init=lam_init)
            h = _out_proj_call(o, h, md, attn_wo, (j,), ln_g[i, 1], ln_b[i, 1], k_gate=5,
                               alpha=alpha)
            assert not ctx_out, "context-side attention output is not implemented"
        else:
            def gmlp(hh, mm):
                vn = _gate_branch_call(hh, mm, gmlp_w[0], (j,), gm_ln_g[j], gm_ln_b[j], k_shift=3)
                return _fused_call("gmlp", hh, mm, gmlp_w, (j,), ln_g[i, 1], ln_b[i, 1],
                                   k_shift=3, k_gate=5, res_scale=1.0, alpha=alpha, vn=vn)

            h = gmlp(h, md)
            if ctx_out:
                hc = gmlp(hc, mdc)

        h = ffn(h, md, 1, 6)
        if ctx_out:
            hc = ffn(hc, mdc, 1, 6)
    return h.reshape(n_batch, seq, d)
```

```python
import functools
import math

import jax
import jax.numpy as jnp
from jax import lax
from jax.experimental import pallas as pl
from jax.experimental.pallas import tpu as pltpu

GRID_W = 64
CHUNK = 128
N_MOD = 9
N_MIXERS = 3
ROPE_BASE = 10000.0
LN_EPS = 1e-5

V7X_LANES = 128
V7X_VMEM_BYTES = 64 * 1024 * 1024
V7X_VMEM_CAP = V7X_VMEM_BYTES - 6 * 1024 * 1024

BF16 = jnp.bfloat16
F32 = jnp.float32

ATTN_TQ = 512
ATTN_GROUP = 4
FUSED_CHUNK = 512
FUSED_ROW_BLOCK = 256
PROJ_ROW_BLOCK = 256


def _params(semantics, vmem_estimate):
    limit = min(V7X_VMEM_CAP, max(32 * 1024 * 1024, int(vmem_estimate * 1.3)))
    return pltpu.CompilerParams(dimension_semantics=semantics, vmem_limit_bytes=limit)


def _pick(n, candidates):
    for c in candidates:
        if n % c == 0:
            return c
    return n


def _mod_row(mods_ref, k):
    return mods_ref[0, k:k + 1, :]


def _modulate_bf16(h, mods_ref, k_shift):
    shift = _mod_row(mods_ref, k_shift)
    scale = _mod_row(mods_ref, k_shift + 1)
    return (h * (1.0 + scale) + shift).astype(BF16)


def _layer_norm(x, g, b):
    mu = jnp.mean(x, axis=-1, keepdims=True)
    xc = x - mu
    var = jnp.mean(xc * xc, axis=-1, keepdims=True)
    return xc * lax.rsqrt(var + LN_EPS) * g + b


def _silu(x):
    return x / (1.0 + jnp.exp(-x))


def _gelu(x):
    return 0.5 * x * (1.0 + lax.erf(x * math.sqrt(0.5)))


def _dot(a, b):
    return jnp.dot(a, b, preferred_element_type=F32)


def _ada_kernel(cond_ref, w_ref, b_ref, o_ref):
    a = _silu(cond_ref[...]).astype(BF16)
    o_ref[...] = _dot(a, w_ref[...].astype(BF16)) + b_ref[...]


def _ada(cond, w, b, layer):
    m, d = cond.shape
    n = w.shape[2]
    tn = _pick(n, (1024, 512, 256, 128))
    est = 2 * d * tn * 4 + d * tn * 2 + 4 * m * (d + tn) * 4
    return pl.pallas_call(
        _ada_kernel,
        grid=(n // tn,),
        in_specs=[pl.BlockSpec((m, d), lambda j: (0, 0)),
                  pl.BlockSpec((None, d, tn), lambda j: (layer, 0, j)),
                  pl.BlockSpec((None, 1, tn), lambda j: (layer, 0, j))],
        out_specs=pl.BlockSpec((m, tn), lambda j: (0, j)),
        out_shape=jax.ShapeDtypeStruct((m, n), F32),
        compiler_params=_params(("parallel",), est),
        name="ada_mod",
    )(cond, w, b.reshape(b.shape[0], 1, n))


def _fused_kernel(kind, k_shift, k_gate, res_scale, alpha, period, n_steps, *refs):
    h_ref, mods_ref = refs[0], refs[1]
    lng_ref, lnb_ref, o_ref, a_ref, acc_ref = refs[-5:]
    w = refs[2:-5]
    wd_ref = w[-1]
    tm = h_ref.shape[0]
    rb = min(tm, max(FUSED_ROW_BLOCK, period))
    row_blocks = [slice(r, r + rb) for r in range(0, tm, rb)]

    def up(a, rows):
        if kind == "ffn":
            return _dot(a, w[0][...]), _dot(a, w[1][...])
        if kind == "conv":
            return _dot(a, w[0][...]), _dot(a, w[1][...]), _dot(a, w[2][...])
        vn_ref, ws_ref, bs_ref = w[1], w[2], w[3]
        gw = w[0].shape[1] // ws_ref.shape[0]
        cols = []
        for g in range(ws_ref.shape[0]):
            parts = [_dot(ws_ref[g], vn_ref[c:c + CHUNK, g * gw:(g + 1) * gw]) + bs_ref[g]
                     for c in range(rows.start, rows.stop, CHUNK)]
            cols.append(jnp.concatenate(parts, axis=0))
        return _dot(a, w[0][...]), jnp.concatenate(cols, axis=1)

    def hidden(pre):
        if kind == "ffn":
            g, u = pre
            return _silu(g) * u
        if kind == "conv":
            b, c, v = pre
            z = c * v
            pos = lax.broadcasted_iota(jnp.int32, z.shape, 0) & (period - 1)
            z_prev = jnp.where(pos == 0, 0.0, pltpu.roll(z, 1, 0))
            z_next = jnp.where(pos == period - 1, 0.0, pltpu.roll(z, z.shape[0] - 1, 0))
            cw = w[3][...]
            return b * (cw[0:1, :] * z_prev + cw[1:2, :] * z + cw[2:3, :] * z_next)
        u, s = pre
        return _gelu(u) * s

    def step(first, last):
        def start(rows):
            if first:
                a = _modulate_bf16(h_ref[rows, :], mods_ref, k_shift)
                if not last:
                    a_ref[rows, :] = a
            else:
                a = a_ref[rows, :]
            return up(a, rows)

        def finish(rows, pre):
            acc = _dot(hidden(pre).astype(BF16), wd_ref[...])
            if not first:
                acc = acc_ref[rows, :] + acc
            if last:
                y = alpha * h_ref[rows, :] + (res_scale * _mod_row(mods_ref, k_gate)) * acc
                o_ref[rows, :] = _layer_norm(y, lng_ref[...], lnb_ref[...])
            else:
                acc_ref[rows, :] = acc

        pending = None
        for rows in row_blocks:
            pre = start(rows)
            if pending is not None:
                finish(*pending)
            pending = (rows, pre)
        finish(*pending)

    if n_steps == 1:
        step(True, True)
        return
    j = pl.program_id(1)
    pl.when(j == 0)(functools.partial(step, True, False))
    if n_steps > 2:
        pl.when(jnp.logical_and(j > 0, j < n_steps - 1))(functools.partial(step, False, False))
    pl.when(j == n_steps - 1)(functools.partial(step, False, True))


def _lead_spec(lead, block, index_fn):
    lead = tuple(lead)
    return pl.BlockSpec((None,) * len(lead) + tuple(block),
                        lambda *g: lead + tuple(index_fn(*g)))


def _fused_call(kind, h, mods, weights, lead, ln_g, ln_b, *, k_shift, k_gate, res_scale, alpha,
                period=GRID_W, vn=None):
    n, d = h.shape
    nb = mods.shape[0]
    tm = _pick(n // nb, (512, 256, 128))
    n_tiles = n // tm
    tiles_per_mod = n_tiles // nb

    row = lambda i, j: (i, 0)
    col_blk = lambda i, j: (0, j)
    row_blk = lambda i, j: (j, 0)
    common_in = [pl.BlockSpec((tm, d), row),
                 pl.BlockSpec((1, N_MOD, d), lambda i, j: (i // tiles_per_mod, 0, 0))]
    chunk_blk = lambda i, j: (j, 0, 0)
    if kind == "ffn":
        wg, wu, wd = weights
        n_chunks, tc = wg.shape[-3], wg.shape[-1]
        w_in = [_lead_spec(lead, (None, d, tc), chunk_blk), _lead_spec(lead, (None, d, tc), chunk_blk),
                _lead_spec(lead, (tc, d), row_blk)]
        w_args = [wg, wu, wd]
        w_bytes = 3 * d * tc * 2
    elif kind == "conv":
        w_in3, cw, wd = weights
        n_chunks, tc = w_in3.shape[-3] // 3, w_in3.shape[-1]
        w_in = [_lead_spec(lead, (None, d, tc), chunk_blk),
                _lead_spec(lead, (None, d, tc), lambda i, j: (n_chunks + j, 0, 0)),
                _lead_spec(lead, (None, d, tc), lambda i, j: (2 * n_chunks + j, 0, 0)),
                _lead_spec(lead, (3, tc), col_blk),
                _lead_spec(lead, (tc, d), row_blk)]
        w_args = [w_in3, w_in3, w_in3, cw, wd]
        w_bytes = 4 * d * tc * 2
    else:
        w_in2, ws, bs, wd = weights
        groups = ws.shape[-3]
        gw = wd.shape[-2] // groups
        per_step = max(g for g in range(1, groups + 1) if groups % g == 0 and g * gw <= 1024)
        tc = per_step * gw
        n_chunks = groups // per_step
        w_in = [_lead_spec(lead, (d, tc), col_blk),
                pl.BlockSpec((tm, tc), lambda i, j: (i, j)),
                _lead_spec(lead, (per_step, CHUNK, CHUNK), lambda i, j: (j, 0, 0)),
                _lead_spec(lead, (per_step, CHUNK, 1), lambda i, j: (j, 0, 0)),
                _lead_spec(lead, (tc, d), row_blk)]
        w_args = [w_in2, vn, ws, bs, wd]
        w_bytes = 2 * d * tc * 2 + tm * tc * 2
    vec = pl.BlockSpec((1, d), lambda i, j: (0, 0))
    est = (2 * 2 * tm * d * 4
           + tm * d * (4 + 2)
           + 2 * w_bytes
           + 6 * tm * tc * 4
           + 2 * tm * d * 4)
    kern = functools.partial(_fused_kernel, kind, k_shift, k_gate, res_scale, alpha, period,
                             n_chunks)
    return pl.pallas_call(
        kern,
        grid=(n_tiles, n_chunks),
        in_specs=common_in + w_in + [vec, vec],
        out_specs=pl.BlockSpec((tm, d), row),
        out_shape=jax.ShapeDtypeStruct((n, d), F32),
        scratch_shapes=[pltpu.VMEM((tm, d), BF16), pltpu.VMEM((tm, d), F32)],
        compiler_params=_params(("parallel", "arbitrary"), est),
        name="fused_" + kind,
    )(h, mods, *w_args, ln_g.reshape(1, d), ln_b.reshape(1, d))


def _qkv_kernel(k_shift, n_rope, h_ref, mods_ref, w_ref, tab_ref, o_ref, vt_ref, a_ref):
    s = pl.program_id(1)
    n_sec = pl.num_programs(1)
    cos_ref, sin_ref = tab_ref.at[0], tab_ref.at[1]

    @pl.when(s == 0)
    def _():
        a_ref[...] = _modulate_bf16(h_ref[...], mods_ref, k_shift)

    def project(rope, transposed):
        width = w_ref.shape[1]
        slab = _pick(width, (512, 256, 128))
        for c in range(width // slab):
            y = _dot(a_ref[...], w_ref[:, c * slab:(c + 1) * slab])
            if transposed:
                vt_ref[c * slab:(c + 1) * slab, :] = y.T.astype(vt_ref.dtype)
                continue
            for r in range(slab // V7X_LANES):
                yr = y[:, r * V7X_LANES:(r + 1) * V7X_LANES]
                if rope:
                    yr = yr * cos_ref[...] + pltpu.roll(yr, V7X_LANES // 2, 1) * sin_ref[...]
                lo = c * slab + r * V7X_LANES
                o_ref[:, lo:lo + V7X_LANES] = yr.astype(o_ref.dtype)

    if n_rope > 0:
        @pl.when(s < n_rope)
        def _():
            project(True, False)

    @pl.when(jnp.logical_and(s >= n_rope, s < n_sec - 1))
    def _():
        project(False, False)

    @pl.when(s == n_sec - 1)
    def _():
        project(False, True)


def _qkv_call(h, mods, w, lead, tabs, *, k_shift, n_rope, seq, first_sec=0):
    n, d = h.shape
    n_tab = tabs.shape[0]
    nb = mods.shape[0]
    n_sec = w.shape[-1] // d - first_sec
    tm = _pick(min(n // nb, seq), (512, 256, 128))
    n_tiles = n // tm
    tiles_per_mod = n_tiles // nb
    tiles_per_seq = seq // tm
    est = (2 * tm * d * 4 + tm * d * 2 + 2 * d * d * 2 + 2 * 2 * tm * d * 2
           + 8 * tm * V7X_LANES * 4 + 4 * tm * 512 * 4)
    kern = functools.partial(_qkv_kernel, k_shift, n_rope)
    return pl.pallas_call(
        kern,
        grid=(n_tiles, n_sec),
        in_specs=[pl.BlockSpec((tm, d), lambda i, s: (i, 0)),
                  pl.BlockSpec((1, N_MOD, d), lambda i, s: (i // tiles_per_mod, 0, 0)),
                  _lead_spec(lead, (d, d), lambda i, s: (0, first_sec + s)),
                  pl.BlockSpec((None, 2, tm, V7X_LANES),
                               lambda i, s: (jnp.minimum(s, n_tab - 1), 0, i % tiles_per_seq, 0))],
        out_specs=[pl.BlockSpec((tm, d), lambda i, s: (i, jnp.minimum(s, n_sec - 2))),
                   pl.BlockSpec((None, d, tm), lambda i, s: (i, 0, 0))],
        out_shape=[jax.ShapeDtypeStruct((n, (n_sec - 1) * d), BF16),
                   jax.ShapeDtypeStruct((n_tiles, d, tm), BF16)],
        scratch_shapes=[pltpu.VMEM((tm, d), BF16)],
        compiler_params=_params(("parallel", "arbitrary"), est),
        name="qkv_proj",
    )(h, mods, w, tabs)


def _row_blocks(n_rows, block):
    block = min(n_rows, block)
    return [slice(r, r + block) for r in range(0, n_rows, block)]


def _gate_branch_kernel(k_shift, h_ref, mods_ref, w_ref, g_ref, b_ref, o_ref):
    for rows in _row_blocks(h_ref.shape[0], PROJ_ROW_BLOCK):
        a = _modulate_bf16(h_ref[rows, :], mods_ref, k_shift)
        v = _gelu(_dot(a, w_ref[...]))
        o_ref[rows, :] = _layer_norm(v, g_ref[...], b_ref[...]).astype(o_ref.dtype)


def _gate_branch_call(h, mods, w, lead, g, b, *, k_shift):
    n, d = h.shape
    nb = mods.shape[0]
    width = w.shape[-1] // 2
    tm = _pick(n // nb, (512, 256, 128))
    n_tiles = n // tm
    tiles_per_mod = n_tiles // nb
    est = 2 * tm * d * 4 + 2 * d * width * 2 + 2 * tm * width * 2 + 4 * tm * width * 4
    return pl.pallas_call(
        functools.partial(_gate_branch_kernel, k_shift),
        grid=(n_tiles,),
        in_specs=[pl.BlockSpec((tm, d), lambda i: (i, 0)),
                  pl.BlockSpec((1, N_MOD, d), lambda i: (i // tiles_per_mod, 0, 0)),
                  _lead_spec(lead, (d, width), lambda i: (0, 1)),
                  pl.BlockSpec((1, width), lambda i: (0, 0)),
                  pl.BlockSpec((1, width), lambda i: (0, 0))],
        out_specs=pl.BlockSpec((tm, width), lambda i: (i, 0)),
        out_shape=jax.ShapeDtypeStruct((n, width), BF16),
        compiler_params=_params(("parallel",), est),
        name="gmlp_gate_branch",
    )(h, mods, w, g.reshape(1, width), b.reshape(1, width))


def _attn_kernel(lam_init, lam_ref, q_ref, k_ref, vt_ref, kc_ref, vct_ref, subln_ref, o_ref,
                 acc_ref, s_ref):
    dk = q_ref.shape[1] // 2
    tq = q_ref.shape[0]
    tk = vt_ref.shape[2]
    q = q_ref[...]
    qm = (q[:, :dk], q[:, dk:])
    nt = (((1,), (1,)), ((), ()))

    def update(carry, blocks):
        for g, (k_blk, _) in enumerate(blocks):
            for mp in range(2):
                s_ref[g, mp, 0:k_blk.shape[0], :] = lax.dot_general(
                    k_blk[:, mp * dk:(mp + 1) * dk], qm[mp], nt, preferred_element_type=F32)
        carry = list(carry)
        for g, (k_blk, vt_blk) in enumerate(blocks):
            for mp in range(2):
                m_old, l_old = carry[2 * mp], carry[2 * mp + 1]
                st = s_ref[g, mp, 0:k_blk.shape[0], :]
                m_new = jnp.maximum(m_old, jnp.max(st, axis=0, keepdims=True))
                pt = jnp.exp2(st - m_new)
                corr = jnp.exp2(m_old - m_new)
                carry[2 * mp] = m_new
                carry[2 * mp + 1] = corr * l_old + jnp.sum(pt, axis=0, keepdims=True)
                acc_ref[mp] = corr * acc_ref[mp] + _dot(vt_blk, pt.astype(BF16))
        return tuple(carry)

    acc_ref[...] = jnp.zeros_like(acc_ref)
    neg = jnp.full((1, tq), -jnp.inf, F32)
    zero = jnp.zeros((1, tq), F32)
    group = s_ref.shape[0]

    def body(c, carry):
        blocks = []
        for g in range(group):
            start = pl.multiple_of((c * group + g) * tk, tk)
            blocks.append((k_ref[pl.ds(start, tk), :], vt_ref[c * group + g]))
        return update(carry, blocks)

    carry = lax.fori_loop(0, vt_ref.shape[0] // group, body, (neg, zero, neg, zero))
    tc = vct_ref.shape[2]
    carry = update(carry, [(kc_ref[c * tc:(c + 1) * tc, :], vct_ref[c])
                           for c in range(vct_ref.shape[0])])
    _, l0, _, l1 = carry

    lp = lam_ref[...]
    lam = (jnp.exp(jnp.sum(lp[0:1, :] * lp[1:2, :], axis=-1, keepdims=True))
           - jnp.exp(jnp.sum(lp[2:3, :] * lp[3:4, :], axis=-1, keepdims=True)) + lam_init)
    ot = acc_ref[0] / l0 - lam * (acc_ref[1] / l1)
    ot = ot * lax.rsqrt(jnp.mean(ot * ot, axis=0, keepdims=True) + LN_EPS) * subln_ref[...]
    o_ref[...] = (ot * (1.0 - lam_init)).T.astype(o_ref.dtype)


def _attn_call(qk, vt, kc, vct, lam_p, subln, *, n_batch, seq, ctx_len, heads, lam_init):
    d = qk.shape[1] // 2
    dv = d // heads
    tk = vt.shape[2]
    tc = vct.shape[2]
    nk = seq // tk
    nc = ctx_len // tc
    group = _pick(nk, (ATTN_GROUP, 1))
    assert nc <= group and tc <= tk, "context keys must fit one score-scratch group"
    tq = _pick(seq, (ATTN_TQ, 128))
    nq = seq // tq
    est = (2 * 2 * tq * dv * 2 + 2 * 2 * seq * dv * 2 + 2 * 2 * ctx_len * dv * 2
           + 2 * tq * dv * 4 + 8 * tq * tk * 4)
    kern = functools.partial(_attn_kernel, lam_init)
    return pl.pallas_call(
        kern,
        grid=(n_batch, heads, nq),
        in_specs=[pl.BlockSpec(lam_p.shape, lambda b, h, i: (0, 0)),
                  pl.BlockSpec((tq, dv), lambda b, h, i: (b * nq + i, h)),
                  pl.BlockSpec((seq, dv), lambda b, h, i: (b, heads + h)),
                  pl.BlockSpec((nk, dv, tk), lambda b, h, i: (b, h, 0)),
                  pl.BlockSpec((ctx_len, dv), lambda b, h, i: (b, h)),
                  pl.BlockSpec((nc, dv, tc), lambda b, h, i: (b, h, 0)),
                  pl.BlockSpec((dv, 1), lambda b, h, i: (0, 0))],
        out_specs=pl.BlockSpec((tq, dv), lambda b, h, i: (b * nq + i, h)),
        out_shape=jax.ShapeDtypeStruct((n_batch * seq, d), BF16),
        scratch_shapes=[pltpu.VMEM((2, dv, tq), F32), pltpu.VMEM((group, 2, tk, tq), F32)],
        compiler_params=_params(("parallel", "parallel", "arbitrary"), est),
        name="diff_attention",
    )(lam_p, qk, qk, vt, kc, vct, subln.reshape(dv, 1))


def _out_proj_kernel(k_gate, alpha, y_ref, h_ref, mods_ref, w_ref, g_ref, b_ref, o_ref):
    for rows in _row_blocks(h_ref.shape[0], PROJ_ROW_BLOCK):
        y = _dot(y_ref[rows, :], w_ref[...])
        x = alpha * h_ref[rows, :] + _mod_row(mods_ref, k_gate) * y
        o_ref[rows, :] = _layer_norm(x, g_ref[...], b_ref[...])


def _out_proj_call(y, h, mods, w, lead, g, b, *, k_gate, alpha):
    n, d = h.shape
    nb = mods.shape[0]
    tm = _pick(n // nb, (512, 256, 128))
    n_tiles = n // tm
    tiles_per_mod = n_tiles // nb
    est = 2 * tm * d * 2 + 2 * 2 * tm * d * 4 + 2 * d * d * 2 + 4 * tm * d * 4
    return pl.pallas_call(
        functools.partial(_out_proj_kernel, k_gate, alpha),
        grid=(n_tiles,),
        in_specs=[pl.BlockSpec((tm, d), lambda i: (i, 0)),
                  pl.BlockSpec((tm, d), lambda i: (i, 0)),
                  pl.BlockSpec((1, N_MOD, d), lambda i: (i // tiles_per_mod, 0, 0)),
                  _lead_spec(lead, (d, d), lambda i: (0, 0)),
                  pl.BlockSpec((1, d), lambda i: (0, 0)),
                  pl.BlockSpec((1, d), lambda i: (0, 0))],
        out_specs=pl.BlockSpec((tm, d), lambda i: (i, 0)),
        out_shape=jax.ShapeDtypeStruct((n, d), F32),
        compiler_params=_params(("parallel",), est),
        name="out_proj_norm",
    )(y, h, mods, w, g.reshape(1, d), b.reshape(1, d))


def _rope_tables(seq, dk):
    n_freq = dk // 4
    t = jnp.arange(seq)
    inv = ROPE_BASE ** (-jnp.arange(n_freq, dtype=F32) / n_freq)
    ang_r = (t // GRID_W).astype(F32)[:, None] * inv
    ang_c = (t % GRID_W).astype(F32)[:, None] * inv
    cos_t = jnp.concatenate([jnp.cos(ang_r), jnp.cos(ang_c)] * 2, axis=-1)
    sin_t = jnp.concatenate([-jnp.sin(ang_r), -jnp.sin(ang_c), jnp.sin(ang_r), jnp.sin(ang_c)], axis=-1)
    k_tab = jnp.stack([cos_t, sin_t])
    return jnp.stack([k_tab * (dk ** -0.5 * math.log2(math.e)), k_tab])


def _column_chunks(w):
    n = w.shape[-1]
    tc = _pick(n, (FUSED_CHUNK, 256, 128))
    w = w.reshape(w.shape[:-1] + (n // tc, tc))
    return jnp.swapaxes(w, -3, -2)


def _rope_column_layout(w, dk):
    rows, width = w.shape
    w = w.reshape(rows, width // dk, 2, 2, dk // 4)
    return jnp.swapaxes(w, 2, 3).reshape(rows, width)


def kernel(x, c, ctx, c_ctx, ada_w, ada_b, ln_g, ln_b, ffn_wg, ffn_wu, ffn_wd, sc_w_in, sc_conv,
           sc_w_out, da_w_qkv, da_lambda, da_subln, da_w_o, gm_w_in, gm_ln_g, gm_ln_b, gm_w_s,
           gm_b_s, gm_w_out):
    n_batch, seq, d = x.shape
    ctx_len = ctx.shape[1]
    depth = ada_w.shape[0]
    mixer_of_layer = tuple(i % N_MIXERS for i in range(depth))
    last_ctx_layer = max([i for i in range(depth) if mixer_of_layer[i] == 1], default=-1)
    alpha = (2.0 * depth) ** 0.25
    dv = da_subln.shape[-1]
    heads = d // dv
    dk = dv // 2

    h = x.reshape(n_batch * seq, d)
    hc = ctx.reshape(n_batch * ctx_len, d)

    n_cond = n_batch + 1
    cond = jnp.zeros((16 * ((n_cond + 15) // 16), d), F32)
    cond = cond.at[:n_batch].set(c).at[n_batch].set(c_ctx)

    rope_tabs = _rope_tables(seq, dk)
    no_rope_tabs = jnp.zeros((1, 2, ctx_len, dk), F32)

    ffn_w = (_column_chunks(ffn_wg.astype(BF16)), _column_chunks(ffn_wu.astype(BF16)),
             ffn_wd.astype(BF16))
    conv_w = (_column_chunks(sc_w_in.astype(BF16)), sc_conv, sc_w_out.astype(BF16))
    qkv_w = da_w_qkv.astype(BF16)
    qkv_w = jnp.concatenate(
        [_rope_column_layout(qkv_w[..., :2 * d].reshape(-1, 2 * d), dk).reshape(qkv_w.shape[0], d, 2 * d),
         qkv_w[..., 2 * d:]], axis=-1)
    attn_wo = da_w_o.astype(BF16)
    gmlp_w = (gm_w_in.astype(BF16), gm_w_s.astype(BF16), gm_b_s[..., None], gm_w_out.astype(BF16))

    for i in range(depth):
        kind = mixer_of_layer[i]
        j = i // N_MIXERS
        ctx_in = i <= last_ctx_layer
        ctx_out = i < last_ctx_layer
        mods_all = _ada(cond, ada_w, ada_b, i).reshape(-1, N_MOD, d)
        md = mods_all[:n_batch]
        mdc = mods_all[n_batch:n_batch + 1]

        def ffn(hh, mm, half, k0):
            return _fused_call("ffn", hh, mm, ffn_w, (i, half), ln_g[i, 2 * half],
                               ln_b[i, 2 * half], k_shift=k0, k_gate=k0 + 2, res_scale=0.5,
                               alpha=alpha)

        h = ffn(h, md, 0, 0)
        if ctx_in:
            hc = ffn(hc, mdc, 0, 0)

        if kind == 0:
            conv = functools.partial(_fused_call, "conv", weights=conv_w, lead=(j,),
                                     ln_g=ln_g[i, 1], ln_b=ln_b[i, 1], k_shift=3, k_gate=5,
                                     res_scale=1.0, alpha=alpha)
            h = conv(h, md, period=GRID_W)
            if ctx_out:
                hc = conv(hc, mdc, period=ctx_len)
        elif kind == 1:
            lam_init = 0.8 - 0.6 * math.exp(-0.3 * i)
            qk, vt = _qkv_call(h, md, qkv_w, (j,), rope_tabs, k_shift=3, n_rope=2, seq=seq)
            kc, vct = _qkv_call(hc, mdc, qkv_w, (j,), no_rope_tabs, k_shift=3, n_rope=0,
                                seq=ctx_len, first_sec=1)
            o = _attn_call(qk, vt, kc, vct, da_lambda[j], da_subln[j], n_batch=n_batch, seq=seq,
                           ctx_len=ctx_len, heads=heads, lam_init=lam_init)
            h = _out_proj_call(o, h, md, attn_wo, (j,), ln_g[i, 1], ln_b[i, 1], k_gate=5,
                               alpha=alpha)
            assert not ctx_out, "context-side attention output is not implemented"
        else:
            def gmlp(hh, mm):
                vn = _gate_branch_call(hh, mm, gmlp_w[0], (j,), gm_ln_g[j], gm_ln_b[j], k_shift=3)
                return _fused_call("gmlp", hh, mm, gmlp_w, (j,), ln_g[i, 1], ln_b[i, 1],
                                   k_shift=3, k_gate=5, res_scale=1.0, alpha=alpha, vn=vn)

            h = gmlp(h, md)
            if ctx_out:
                hc = gmlp(hc, mdc)

        h = ffn(h, md, 1, 6)
        if ctx_out:
            hc = ffn(hc, mdc, 1, 6)
    return h.reshape(n_batch, seq, d)
```

```python
import functools
import math

import jax
import jax.numpy as jnp
from jax import lax
from jax.experimental import pallas as pl
from jax.experimental.pallas import tpu as pltpu

GRID_W = 64
CHUNK = 128
N_MOD = 9
N_MIXERS = 3
ROPE_BASE = 10000.0
LN_EPS = 1e-5

V7X_LANES = 128
V7X_VMEM_BYTES = 64 * 1024 * 1024
V7X_VMEM_CAP = V7X_VMEM_BYTES - 6 * 1024 * 1024

BF16 = jnp.bfloat16
F32 = jnp.float32

ATTN_TQ = 512
ATTN_GROUP = 4
FUSED_CHUNK = 512
FUSED_TILE = {"ffn": 1024, "conv": 512, "gmlp": 512}
FUSED_ROW_BLOCK = {"ffn": 512, "conv": 512, "gmlp": 256}
PROJ_ROW_BLOCK = 256


def _params(semantics, vmem_estimate):
    limit = min(V7X_VMEM_CAP, max(32 * 1024 * 1024, int(vmem_estimate * 1.3)))
    return pltpu.CompilerParams(dimension_semantics=semantics, vmem_limit_bytes=limit)


def _pick(n, candidates):
    for c in candidates:
        if n % c == 0:
            return c
    return n


def _mod_row(mods_ref, k):
    return mods_ref[0, k:k + 1, :]


def _modulate_bf16(h, mods_ref, k_shift):
    shift = _mod_row(mods_ref, k_shift)
    scale = _mod_row(mods_ref, k_shift + 1)
    return (h * (1.0 + scale) + shift).astype(BF16)


def _layer_norm(x, g, b):
    mu = jnp.mean(x, axis=-1, keepdims=True)
    xc = x - mu
    var = jnp.mean(xc * xc, axis=-1, keepdims=True)
    return xc * lax.rsqrt(var + LN_EPS) * g + b


def _silu(x):
    return x / (1.0 + jnp.exp(-x))


def _gelu(x):
    return 0.5 * x * (1.0 + lax.erf(x * math.sqrt(0.5)))


def _dot(a, b):
    return jnp.dot(a, b, preferred_element_type=F32)


def _ada_kernel(cond_ref, w_ref, b_ref, o_ref):
    a = _silu(cond_ref[...]).astype(BF16)
    o_ref[...] = _dot(a, w_ref[...].astype(BF16)) + b_ref[...]


def _ada(cond, w, b, layer):
    m, d = cond.shape
    n = w.shape[2]
    tn = _pick(n, (1024, 512, 256, 128))
    est = 2 * d * tn * 4 + d * tn * 2 + 4 * m * (d + tn) * 4
    return pl.pallas_call(
        _ada_kernel,
        grid=(n // tn,),
        in_specs=[pl.BlockSpec((m, d), lambda j: (0, 0)),
                  pl.BlockSpec((None, d, tn), lambda j: (layer, 0, j)),
                  pl.BlockSpec((None, 1, tn), lambda j: (layer, 0, j))],
        out_specs=pl.BlockSpec((m, tn), lambda j: (0, j)),
        out_shape=jax.ShapeDtypeStruct((m, n), F32),
        compiler_params=_params(("parallel",), est),
        name="ada_mod",
    )(cond, w, b.reshape(b.shape[0], 1, n))


def _fused_kernel(kind, k_shift, k_gate, res_scale, alpha, period, n_steps, *refs):
    h_ref, mods_ref = refs[0], refs[1]
    lng_ref, lnb_ref, o_ref, a_ref = refs[-4:]
    acc_ref = o_ref
    w = refs[2:-4]
    wd_ref = w[-1]
    tm = h_ref.shape[0]
    rb = min(tm, max(FUSED_ROW_BLOCK[kind], period))
    row_blocks = [slice(r, r + rb) for r in range(0, tm, rb)]

    def up(a, rows):
        if kind == "ffn":
            return _dot(a, w[0][...]), _dot(a, w[1][...])
        if kind == "conv":
            return _dot(a, w[0][...]), _dot(a, w[1][...]), _dot(a, w[2][...])
        vn_ref, ws_ref, bs_ref = w[1], w[2], w[3]
        gw = w[0].shape[1] // ws_ref.shape[0]
        cols = []
        for g in range(ws_ref.shape[0]):
            parts = [_dot(ws_ref[g], vn_ref[c:c + CHUNK, g * gw:(g + 1) * gw]) + bs_ref[g]
                     for c in range(rows.start, rows.stop, CHUNK)]
            cols.append(jnp.concatenate(parts, axis=0))
        return _dot(a, w[0][...]), jnp.concatenate(cols, axis=1)

    def hidden(pre):
        if kind == "ffn":
            g, u = pre
            return _silu(g) * u
        if kind == "conv":
            b, c, v = pre
            z = c * v
            pos = lax.broadcasted_iota(jnp.int32, z.shape, 0) & (period - 1)
            z_prev = jnp.where(pos == 0, 0.0, pltpu.roll(z, 1, 0))
            z_next = jnp.where(pos == period - 1, 0.0, pltpu.roll(z, z.shape[0] - 1, 0))
            cw = w[3][...]
            return b * (cw[0:1, :] * z_prev + cw[1:2, :] * z + cw[2:3, :] * z_next)
        u, s = pre
        return _gelu(u) * s

    def step(first, last):
        def start(rows):
            if first:
                a = _modulate_bf16(h_ref[rows, :], mods_ref, k_shift)
                if not last:
                    a_ref[rows, :] = a
            else:
                a = a_ref[rows, :]
            return up(a, rows)

        def finish(rows, pre):
            acc = _dot(hidden(pre).astype(BF16), wd_ref[...])
            if not first:
                acc = acc_ref[rows, :] + acc
            if last:
                y = alpha * h_ref[rows, :] + (res_scale * _mod_row(mods_ref, k_gate)) * acc
                o_ref[rows, :] = _layer_norm(y, lng_ref[...], lnb_ref[...])
            else:
                acc_ref[rows, :] = acc

        pending = None
        for rows in row_blocks:
            pre = start(rows)
            if pending is not None:
                finish(*pending)
            pending = (rows, pre)
        finish(*pending)

    if n_steps == 1:
        step(True, True)
        return
    j = pl.program_id(1)
    pl.when(j == 0)(functools.partial(step, True, False))
    if n_steps > 2:
        pl.when(jnp.logical_and(j > 0, j < n_steps - 1))(functools.partial(step, False, False))
    pl.when(j == n_steps - 1)(functools.partial(step, False, True))


def _lead_spec(lead, block, index_fn):
    lead = tuple(lead)
    return pl.BlockSpec((None,) * len(lead) + tuple(block),
                        lambda *g: lead + tuple(index_fn(*g)))


def _fused_call(kind, h, mods, weights, lead, ln_g, ln_b, *, k_shift, k_gate, res_scale, alpha,
                period=GRID_W, vn=None):
    n, d = h.shape
    nb = mods.shape[0]
    tm = _pick(n // nb, (FUSED_TILE[kind], 512, 256, 128))
    n_tiles = n // tm
    tiles_per_mod = n_tiles // nb

    row = lambda i, j: (i, 0)
    col_blk = lambda i, j: (0, j)
    row_blk = lambda i, j: (j, 0)
    common_in = [pl.BlockSpec((tm, d), row),
                 pl.BlockSpec((1, N_MOD, d), lambda i, j: (i // tiles_per_mod, 0, 0))]
    if kind == "ffn":
        wg, wu, wd = weights
        f = wg.shape[-1]
        tc = _pick(f, (FUSED_CHUNK, 256, 128))
        n_chunks = f // tc
        w_in = [_lead_spec(lead, (d, tc), col_blk), _lead_spec(lead, (d, tc), col_blk),
                _lead_spec(lead, (tc, d), row_blk)]
        w_args = [wg, wu, wd]
        w_bytes = 3 * d * tc * 2
    elif kind == "conv":
        w_in3, cw, wd = weights
        tc = _pick(d, (FUSED_CHUNK, 256, 128))
        n_chunks = d // tc
        w_in = [_lead_spec(lead, (d, tc), col_blk),
                _lead_spec(lead, (d, tc), lambda i, j: (0, n_chunks + j)),
                _lead_spec(lead, (d, tc), lambda i, j: (0, 2 * n_chunks + j)),
                _lead_spec(lead, (3, tc), col_blk),
                _lead_spec(lead, (tc, d), row_blk)]
        w_args = [w_in3, w_in3, w_in3, cw, wd]
        w_bytes = 4 * d * tc * 2
    else:
        w_in2, ws, bs, wd = weights
        groups = ws.shape[-3]
        gw = wd.shape[-2] // groups
        per_step = max(g for g in range(1, groups + 1) if groups % g == 0 and g * gw <= 1024)
        tc = per_step * gw
        n_chunks = groups // per_step
        w_in = [_lead_spec(lead, (d, tc), col_blk),
                pl.BlockSpec((tm, tc), lambda i, j: (i, j)),
                _lead_spec(lead, (per_step, CHUNK, CHUNK), lambda i, j: (j, 0, 0)),
                _lead_spec(lead, (per_step, CHUNK, 1), lambda i, j: (j, 0, 0)),
                _lead_spec(lead, (tc, d), row_blk)]
        w_args = [w_in2, vn, ws, bs, wd]
        w_bytes = 2 * d * tc * 2 + tm * tc * 2
    vec = pl.BlockSpec((1, d), lambda i, j: (0, 0))
    rb = min(tm, FUSED_ROW_BLOCK[kind])
    est = (2 * 2 * tm * d * 4
           + tm * d * 2
           + 2 * w_bytes
           + 6 * rb * tc * 4
           + 2 * rb * d * 4)
    kern = functools.partial(_fused_kernel, kind, k_shift, k_gate, res_scale, alpha, period,
                             n_chunks)
    return pl.pallas_call(
        kern,
        grid=(n_tiles, n_chunks),
        in_specs=common_in + w_in + [vec, vec],
        out_specs=pl.BlockSpec((tm, d), row),
        out_shape=jax.ShapeDtypeStruct((n, d), F32),
        scratch_shapes=[pltpu.VMEM((tm, d), BF16)],
        compiler_params=_params(("parallel", "arbitrary"), est),
        name="fused_" + kind,
    )(h, mods, *w_args, ln_g.reshape(1, d), ln_b.reshape(1, d))


def _qkv_kernel(k_shift, n_rope, h_ref, mods_ref, w_ref, tab_ref, o_ref, vt_ref, a_ref):
    s = pl.program_id(1)
    n_sec = pl.num_programs(1)
    cos_ref, sin_ref = tab_ref.at[0], tab_ref.at[1]

    @pl.when(s == 0)
    def _():
        a_ref[...] = _modulate_bf16(h_ref[...], mods_ref, k_shift)

    def project(rope, transposed):
        width = w_ref.shape[1]
        slab = _pick(width, (512, 256, 128))
        for c in range(width // slab):
            y = _dot(a_ref[...], w_ref[:, c * slab:(c + 1) * slab])
            if transposed:
                vt_ref[c * slab:(c + 1) * slab, :] = y.T.astype(vt_ref.dtype)
                continue
            for r in range(slab // V7X_LANES):
                yr = y[:, r * V7X_LANES:(r + 1) * V7X_LANES]
                if rope:
                    yr = yr * cos_ref[...] + pltpu.roll(yr, V7X_LANES // 2, 1) * sin_ref[...]
                lo = c * slab + r * V7X_LANES
                o_ref[:, lo:lo + V7X_LANES] = yr.astype(o_ref.dtype)

    if n_rope > 0:
        @pl.when(s < n_rope)
        def _():
            project(True, False)

    @pl.when(jnp.logical_and(s >= n_rope, s < n_sec - 1))
    def _():
        project(False, False)

    @pl.when(s == n_sec - 1)
    def _():
        project(False, True)


def _qkv_call(h, mods, w, lead, tabs, *, k_shift, n_rope, seq, first_sec=0):
    n, d = h.shape
    n_tab = tabs.shape[0]
    nb = mods.shape[0]
    n_sec = w.shape[-1] // d - first_sec
    tm = _pick(min(n // nb, seq), (512, 256, 128))
    n_tiles = n // tm
    tiles_per_mod = n_tiles // nb
    tiles_per_seq = seq // tm
    est = (2 * tm * d * 4 + tm * d * 2 + 2 * d * d * 2 + 2 * 2 * tm * d * 2
           + 8 * tm * V7X_LANES * 4 + 4 * tm * 512 * 4)
    kern = functools.partial(_qkv_kernel, k_shift, n_rope)
    return pl.pallas_call(
        kern,
        grid=(n_tiles, n_sec),
        in_specs=[pl.BlockSpec((tm, d), lambda i, s: (i, 0)),
                  pl.BlockSpec((1, N_MOD, d), lambda i, s: (i // tiles_per_mod, 0, 0)),
                  _lead_spec(lead, (d, d), lambda i, s: (0, first_sec + s)),
                  pl.BlockSpec((None, 2, tm, V7X_LANES),
                               lambda i, s: (jnp.minimum(s, n_tab - 1), 0, i % tiles_per_seq, 0))],
        out_specs=[pl.BlockSpec((tm, d), lambda i, s: (i, jnp.minimum(s, n_sec - 2))),
                   pl.BlockSpec((None, d, tm), lambda i, s: (i, 0, 0))],
        out_shape=[jax.ShapeDtypeStruct((n, (n_sec - 1) * d), BF16),
                   jax.ShapeDtypeStruct((n_tiles, d, tm), BF16)],
        scratch_shapes=[pltpu.VMEM((tm, d), BF16)],
        compiler_params=_params(("parallel", "arbitrary"), est),
        name="qkv_proj",
    )(h, mods, w, tabs)


def _row_blocks(n_rows, block):
    block = min(n_rows, block)
    return [slice(r, r + block) for r in range(0, n_rows, block)]


def _gate_branch_kernel(k_shift, h_ref, mods_ref, w_ref, g_ref, b_ref, o_ref):
    for rows in _row_blocks(h_ref.shape[0], PROJ_ROW_BLOCK):
        a = _modulate_bf16(h_ref[rows, :], mods_ref, k_shift)
        v = _gelu(_dot(a, w_ref[...]))
        o_ref[rows, :] = _layer_norm(v, g_ref[...], b_ref[...]).astype(o_ref.dtype)


def _gate_branch_call(h, mods, w, lead, g, b, *, k_shift):
    n, d = h.shape
    nb = mods.shape[0]
    width = w.shape[-1] // 2
    tm = _pick(n // nb, (512, 256, 128))
    n_tiles = n // tm
    tiles_per_mod = n_tiles // nb
    est = 2 * tm * d * 4 + 2 * d * width * 2 + 2 * tm * width * 2 + 4 * tm * width * 4
    return pl.pallas_call(
        functools.partial(_gate_branch_kernel, k_shift),
        grid=(n_tiles,),
        in_specs=[pl.BlockSpec((tm, d), lambda i: (i, 0)),
                  pl.BlockSpec((1, N_MOD, d), lambda i: (i // tiles_per_mod, 0, 0)),
                  _lead_spec(lead, (d, width), lambda i: (0, 1)),
                  pl.BlockSpec((1, width), lambda i: (0, 0)),
                  pl.BlockSpec((1, width), lambda i: (0, 0))],
        out_specs=pl.BlockSpec((tm, width), lambda i: (i, 0)),
        out_shape=jax.ShapeDtypeStruct((n, width), BF16),
        compiler_params=_params(("parallel",), est),
        name="gmlp_gate_branch",
    )(h, mods, w, g.reshape(1, width), b.reshape(1, width))


def _attn_kernel(lam_init, lam_ref, q_ref, k_ref, vt_ref, kc_ref, vct_ref, subln_ref, o_ref,
                 acc_ref, s_ref):
    dk = q_ref.shape[1] // 2
    tq = q_ref.shape[0]
    tk = vt_ref.shape[2]
    q = q_ref[...]
    qm = (q[:, :dk], q[:, dk:])
    nt = (((1,), (1,)), ((), ()))

    def update(carry, blocks):
        for g, (k_blk, _) in enumerate(blocks):
            for mp in range(2):
                s_ref[g, mp, 0:k_blk.shape[0], :] = lax.dot_general(
                    k_blk[:, mp * dk:(mp + 1) * dk], qm[mp], nt, preferred_element_type=F32)
        carry = list(carry)
        for g, (k_blk, vt_blk) in enumerate(blocks):
            for mp in range(2):
                m_old, l_old = carry[2 * mp], carry[2 * mp + 1]
                st = s_ref[g, mp, 0:k_blk.shape[0], :]
                m_new = jnp.maximum(m_old, jnp.max(st, axis=0, keepdims=True))
                pt = jnp.exp2(st - m_new)
                corr = jnp.exp2(m_old - m_new)
                carry[2 * mp] = m_new
                carry[2 * mp + 1] = corr * l_old + jnp.sum(pt, axis=0, keepdims=True)
                acc_ref[mp] = corr * acc_ref[mp] + _dot(vt_blk, pt.astype(BF16))
        return tuple(carry)

    acc_ref[...] = jnp.zeros_like(acc_ref)
    neg = jnp.full((1, tq), -jnp.inf, F32)
    zero = jnp.zeros((1, tq), F32)
    group = s_ref.shape[0]

    def body(c, carry):
        blocks = []
        for g in range(group):
            start = pl.multiple_of((c * group + g) * tk, tk)
            blocks.append((k_ref[pl.ds(start, tk), :], vt_ref[c * group + g]))
        return update(carry, blocks)

    carry = lax.fori_loop(0, vt_ref.shape[0] // group, body, (neg, zero, neg, zero))
    tc = vct_ref.shape[2]
    carry = update(carry, [(kc_ref[c * tc:(c + 1) * tc, :], vct_ref[c])
                           for c in range(vct_ref.shape[0])])
    _, l0, _, l1 = carry

    lp = lam_ref[...]
    lam = (jnp.exp(jnp.sum(lp[0:1, :] * lp[1:2, :], axis=-1, keepdims=True))
           - jnp.exp(jnp.sum(lp[2:3, :] * lp[3:4, :], axis=-1, keepdims=True)) + lam_init)
    ot = acc_ref[0] / l0 - lam * (acc_ref[1] / l1)
    ot = ot * lax.rsqrt(jnp.mean(ot * ot, axis=0, keepdims=True) + LN_EPS) * subln_ref[...]
    o_ref[...] = (ot * (1.0 - lam_init)).T.astype(o_ref.dtype)


def _attn_call(qk, vt, kc, vct, lam_p, subln, *, n_batch, seq, ctx_len, heads, lam_init):
    d = qk.shape[1] // 2
    dv = d // heads
    tk = vt.shape[2]
    tc = vct.shape[2]
    nk = seq // tk
    nc = ctx_len // tc
    group = _pick(nk, (ATTN_GROUP, 1))
    assert nc <= group and tc <= tk, "context keys must fit one score-scratch group"
    tq = _pick(seq, (ATTN_TQ, 128))
    nq = seq // tq
    est = (2 * 2 * tq * dv * 2 + 2 * 2 * seq * dv * 2 + 2 * 2 * ctx_len * dv * 2
           + 2 * tq * dv * 4 + 8 * tq * tk * 4)
    kern = functools.partial(_attn_kernel, lam_init)
    return pl.pallas_call(
        kern,
        grid=(n_batch, heads, nq),
        in_specs=[pl.BlockSpec(lam_p.shape, lambda b, h, i: (0, 0)),
                  pl.BlockSpec((tq, dv), lambda b, h, i: (b * nq + i, h)),
                  pl.BlockSpec((seq, dv), lambda b, h, i: (b, heads + h)),
                  pl.BlockSpec((nk, dv, tk), lambda b, h, i: (b, h, 0)),
                  pl.BlockSpec((ctx_len, dv), lambda b, h, i: (b, h)),
                  pl.BlockSpec((nc, dv, tc), lambda b, h, i: (b, h, 0)),
                  pl.BlockSpec((dv, 1), lambda b, h, i: (0, 0))],
        out_specs=pl.BlockSpec((tq, dv), lambda b, h, i: (b * nq + i, h)),
        out_shape=jax.ShapeDtypeStruct((n_batch * seq, d), BF16),
        scratch_shapes=[pltpu.VMEM((2, dv, tq), F32), pltpu.VMEM((group, 2, tk, tq), F32)],
        compiler_params=_params(("parallel", "parallel", "arbitrary"), est),
        name="diff_attention",
    )(lam_p, qk, qk, vt, kc, vct, subln.reshape(dv, 1))


def _out_proj_kernel(k_gate, alpha, y_ref, h_ref, mods_ref, w_ref, g_ref, b_ref, o_ref):
    for rows in _row_blocks(h_ref.shape[0], PROJ_ROW_BLOCK):
        y = _dot(y_ref[rows, :], w_ref[...])
        x = alpha * h_ref[rows, :] + _mod_row(mods_ref, k_gate) * y
        o_ref[rows, :] = _layer_norm(x, g_ref[...], b_ref[...])


def _out_proj_call(y, h, mods, w, lead, g, b, *, k_gate, alpha):
    n, d = h.shape
    nb = mods.shape[0]
    tm = _pick(n // nb, (512, 256, 128))
    n_tiles = n // tm
    tiles_per_mod = n_tiles // nb
    est = 2 * tm * d * 2 + 2 * 2 * tm * d * 4 + 2 * d * d * 2 + 4 * tm * d * 4
    return pl.pallas_call(
        functools.partial(_out_proj_kernel, k_gate, alpha),
        grid=(n_tiles,),
        in_specs=[pl.BlockSpec((tm, d), lambda i: (i, 0)),
                  pl.BlockSpec((tm, d), lambda i: (i, 0)),
                  pl.BlockSpec((1, N_MOD, d), lambda i: (i // tiles_per_mod, 0, 0)),
                  _lead_spec(lead, (d, d), lambda i: (0, 0)),
                  pl.BlockSpec((1, d), lambda i: (0, 0)),
                  pl.BlockSpec((1, d), lambda i: (0, 0))],
        out_specs=pl.BlockSpec((tm, d), lambda i: (i, 0)),
        out_shape=jax.ShapeDtypeStruct((n, d), F32),
        compiler_params=_params(("parallel",), est),
        name="out_proj_norm",
    )(y, h, mods, w, g.reshape(1, d), b.reshape(1, d))


def _rope_tables(seq, dk):
    n_freq = dk // 4
    t = jnp.arange(seq)
    inv = ROPE_BASE ** (-jnp.arange(n_freq, dtype=F32) / n_freq)
    ang_r = (t // GRID_W).astype(F32)[:, None] * inv
    ang_c = (t % GRID_W).astype(F32)[:, None] * inv
    cos_t = jnp.concatenate([jnp.cos(ang_r), jnp.cos(ang_c)] * 2, axis=-1)
    sin_t = jnp.concatenate([-jnp.sin(ang_r), -jnp.sin(ang_c), jnp.sin(ang_r), jnp.sin(ang_c)], axis=-1)
    k_tab = jnp.stack([cos_t, sin_t])
    return jnp.stack([k_tab * (dk ** -0.5 * math.log2(math.e)), k_tab])


def _rope_column_layout(w, dk):
    rows, width = w.shape
    w = w.reshape(rows, width // dk, 2, 2, dk // 4)
    return jnp.swapaxes(w, 2, 3).reshape(rows, width)


def kernel(x, c, ctx, c_ctx, ada_w, ada_b, ln_g, ln_b, ffn_wg, ffn_wu, ffn_wd, sc_w_in, sc_conv,
           sc_w_out, da_w_qkv, da_lambda, da_subln, da_w_o, gm_w_in, gm_ln_g, gm_ln_b, gm_w_s,
           gm_b_s, gm_w_out):
    n_batch, seq, d = x.shape
    ctx_len = ctx.shape[1]
    depth = ada_w.shape[0]
    mixer_of_layer = tuple(i % N_MIXERS for i in range(depth))
    last_ctx_layer = max([i for i in range(depth) if mixer_of_layer[i] == 1], default=-1)
    alpha = (2.0 * depth) ** 0.25
    dv = da_subln.shape[-1]
    heads = d // dv
    dk = dv // 2

    h = x.reshape(n_batch * seq, d)
    hc = ctx.reshape(n_batch * ctx_len, d)

    n_cond = n_batch + 1
    cond = jnp.zeros((16 * ((n_cond + 15) // 16), d), F32)
    cond = cond.at[:n_batch].set(c).at[n_batch].set(c_ctx)

    rope_tabs = _rope_tables(seq, dk)
    no_rope_tabs = jnp.zeros((1, 2, ctx_len, dk), F32)

    ffn_w = (ffn_wg.astype(BF16), ffn_wu.astype(BF16), ffn_wd.astype(BF16))
    conv_w = (sc_w_in.astype(BF16), sc_conv, sc_w_out.astype(BF16))
    qkv_w = da_w_qkv.astype(BF16)
    qkv_w = jnp.concatenate(
        [_rope_column_layout(qkv_w[..., :2 * d].reshape(-1, 2 * d), dk).reshape(qkv_w.shape[0], d, 2 * d),
         qkv_w[..., 2 * d:]], axis=-1)
    attn_wo = da_w_o.astype(BF16)
    gmlp_w = (gm_w_in.astype(BF16), gm_w_s.astype(BF16), gm_b_s[..., None], gm_w_out.astype(BF16))

    for i in range(depth):
        kind = mixer_of_layer[i]
        j = i // N_MIXERS
        ctx_in = i <= last_ctx_layer
        ctx_out = i < last_ctx_layer
        mods_all = _ada(cond, ada_w, ada_b, i).reshape(-1, N_MOD, d)
        md = mods_all[:n_batch]
        mdc = mods_all[n_batch:n_batch + 1]

        def ffn(hh, mm, half, k0):
            return _fused_call("ffn", hh, mm, ffn_w, (i, half), ln_g[i, 2 * half],
                               ln_b[i, 2 * half], k_shift=k0, k_gate=k0 + 2, res_scale=0.5,
                               alpha=alpha)

        h = ffn(h, md, 0, 0)
        if ctx_in:
            hc = ffn(hc, mdc, 0, 0)

        if kind == 0:
            conv = functools.partial(_fused_call, "conv", weights=conv_w, lead=(j,),
                                     ln_g=ln_g[i, 1], ln_b=ln_b[i, 1], k_shift=3, k_gate=5,
                                     res_scale=1.0, alpha=alpha)
            h = conv(h, md, period=GRID_W)
            if ctx_out:
                hc = conv(hc, mdc, period=ctx_len)
        elif kind == 1:
            lam_init = 0.8 - 0.6 * math.exp(-0.3 * i)
            qk, vt = _qkv_call(h, md, qkv_w, (j,), rope_tabs, k_shift=3, n_rope=2, seq=seq)
            kc, vct = _qkv_call(hc, mdc, qkv_w, (j,), no_rope_tabs, k_shift=3, n_rope=0,
                                seq=ctx_len, first_sec=1)
            o = _attn_call(qk, vt, kc, vct, da_lambda[j], da_subln[j], n_batch=n_batch, seq=seq,
                           ctx_len=ctx_len, heads=heads, lam_init=lam_init)
            h = _out_proj_call(o, h, md, attn_wo, (j,), ln_g[i, 1], ln_b[i, 1], k_gate=5,
                               alpha=alpha)
            assert not ctx_out, "context-side attention output is not implemented"
        else:
            def gmlp(hh, mm):
                vn = _gate_branch_call(hh, mm, gmlp_w[0], (j,), gm_ln_g[j], gm_ln_b[j], k_shift=3)
                return _fused_call("gmlp", hh, mm, gmlp_w, (j,), ln_g[i, 1], ln_b[i, 1],
                                   k_shift=3, k_gate=5, res_scale=1.0, alpha=alpha, vn=vn)

            h = gmlp(h, md)
            if ctx_out:
                hc = gmlp(hc, mdc)

        h = ffn(h, md, 1, 6)
        if ctx_out:
            hc = ffn(hc, mdc, 1, 6)
    return h.reshape(n_batch, seq, d)
```

```python
import functools
import math

import jax
import jax.numpy as jnp
from jax import lax
from jax.experimental import pallas as pl
from jax.experimental.pallas import tpu as pltpu

GRID_W = 64
CHUNK = 128
N_MOD = 9
N_MIXERS = 3
ROPE_BASE = 10000.0
LN_EPS = 1e-5

V7X_LANES = 128
V7X_VMEM_BYTES = 64 * 1024 * 1024
V7X_VMEM_CAP = V7X_VMEM_BYTES - 6 * 1024 * 1024

BF16 = jnp.bfloat16
F32 = jnp.float32

ATTN_TQ = 1024
ATTN_GROUP = 4
FUSED_CHUNK = 512
FUSED_TILE = {"ffn": 1024, "conv": 1024, "gmlp": 512}
FUSED_ROW_BLOCK = {"ffn": 512, "conv": 512, "gmlp": 256}
PROJ_ROW_BLOCK = 256


def _params(semantics, vmem_estimate):
    limit = min(V7X_VMEM_CAP, max(32 * 1024 * 1024, int(vmem_estimate * 1.3)))
    return pltpu.CompilerParams(dimension_semantics=semantics, vmem_limit_bytes=limit)


def _pick(n, candidates):
    for c in candidates:
        if n % c == 0:
            return c
    return n


def _mod_row(mods_ref, k):
    return mods_ref[0, k:k + 1, :]


def _modulate_bf16(h, mods_ref, k_shift):
    shift = _mod_row(mods_ref, k_shift)
    scale = _mod_row(mods_ref, k_shift + 1)
    return (h * (1.0 + scale) + shift).astype(BF16)


def _layer_norm(x, g, b):
    mu = jnp.mean(x, axis=-1, keepdims=True)
    xc = x - mu
    var = jnp.mean(xc * xc, axis=-1, keepdims=True)
    return xc * lax.rsqrt(var + LN_EPS) * g + b


def _silu(x):
    return x / (1.0 + jnp.exp(-x))


def _gelu(x):
    return 0.5 * x * (1.0 + lax.erf(x * math.sqrt(0.5)))


def _dot(a, b):
    return jnp.dot(a, b, preferred_element_type=F32)


def _ada_kernel(cond_ref, w_ref, b_ref, o_ref):
    a = _silu(cond_ref[...]).astype(BF16)
    o_ref[...] = _dot(a, w_ref[...].astype(BF16)) + b_ref[...]


def _ada(cond, w, b, layer):
    m, d = cond.shape
    n = w.shape[2]
    tn = _pick(n, (1024, 512, 256, 128))
    est = 2 * d * tn * 4 + d * tn * 2 + 4 * m * (d + tn) * 4
    return pl.pallas_call(
        _ada_kernel,
        grid=(n // tn,),
        in_specs=[pl.BlockSpec((m, d), lambda j: (0, 0)),
                  pl.BlockSpec((None, d, tn), lambda j: (layer, 0, j)),
                  pl.BlockSpec((None, 1, tn), lambda j: (layer, 0, j))],
        out_specs=pl.BlockSpec((m, tn), lambda j: (0, j)),
        out_shape=jax.ShapeDtypeStruct((m, n), F32),
        compiler_params=_params(("parallel",), est),
        name="ada_mod",
    )(cond, w, b.reshape(b.shape[0], 1, n))


def _fused_kernel(kind, k_shift, k_gate, res_scale, alpha, period, n_steps, *refs):
    h_ref, mods_ref = refs[0], refs[1]
    lng_ref, lnb_ref, o_ref, a_ref = refs[-4:]
    acc_ref = o_ref
    w = refs[2:-4]
    wd_ref = w[-1]
    tm = h_ref.shape[0]
    rb = min(tm, max(FUSED_ROW_BLOCK[kind], period))
    row_blocks = [slice(r, r + rb) for r in range(0, tm, rb)]

    def up(a, rows):
        if kind == "ffn":
            return _dot(a, w[0][...]), _dot(a, w[1][...])
        if kind == "conv":
            return _dot(a, w[0][...]), _dot(a, w[1][...]), _dot(a, w[2][...])
        vn_ref, ws_ref, bs_ref = w[1], w[2], w[3]
        gw = w[0].shape[1] // ws_ref.shape[0]
        cols = []
        for g in range(ws_ref.shape[0]):
            parts = [_dot(ws_ref[g], vn_ref[c:c + CHUNK, g * gw:(g + 1) * gw]) + bs_ref[g]
                     for c in range(rows.start, rows.stop, CHUNK)]
            cols.append(jnp.concatenate(parts, axis=0))
        return _dot(a, w[0][...]), jnp.concatenate(cols, axis=1)

    def hidden(pre):
        if kind == "ffn":
            g, u = pre
            return _silu(g) * u
        if kind == "conv":
            b, c, v = pre
            z = c * v
            pos = lax.broadcasted_iota(jnp.int32, z.shape, 0) & (period - 1)
            z_prev = jnp.where(pos == 0, 0.0, pltpu.roll(z, 1, 0))
            z_next = jnp.where(pos == period - 1, 0.0, pltpu.roll(z, z.shape[0] - 1, 0))
            cw = w[3][...]
            return b * (cw[0:1, :] * z_prev + cw[1:2, :] * z + cw[2:3, :] * z_next)
        u, s = pre
        return _gelu(u) * s

    def step(first, last):
        def start(rows):
            if first:
                a = _modulate_bf16(h_ref[rows, :], mods_ref, k_shift)
                if not last:
                    a_ref[rows, :] = a
            else:
                a = a_ref[rows, :]
            return up(a, rows)

        def finish(rows, pre):
            acc = _dot(hidden(pre).astype(BF16), wd_ref[...])
            if not first:
                acc = acc_ref[rows, :] + acc
            if last:
                y = alpha * h_ref[rows, :] + (res_scale * _mod_row(mods_ref, k_gate)) * acc
                o_ref[rows, :] = _layer_norm(y, lng_ref[...], lnb_ref[...])
            else:
                acc_ref[rows, :] = acc

        pending = None
        for rows in row_blocks:
            pre = start(rows)
            if pending is not None:
                finish(*pending)
            pending = (rows, pre)
        finish(*pending)

    if n_steps == 1:
        step(True, True)
        return
    j = pl.program_id(1)
    pl.when(j == 0)(functools.partial(step, True, False))
    if n_steps > 2:
        pl.when(jnp.logical_and(j > 0, j < n_steps - 1))(functools.partial(step, False, False))
    pl.when(j == n_steps - 1)(functools.partial(step, False, True))


def _lead_spec(lead, block, index_fn):
    lead = tuple(lead)
    return pl.BlockSpec((None,) * len(lead) + tuple(block),
                        lambda *g: lead + tuple(index_fn(*g)))


def _fused_call(kind, h, mods, weights, lead, ln_g, ln_b, *, k_shift, k_gate, res_scale, alpha,
                period=GRID_W, vn=None):
    n, d = h.shape
    nb = mods.shape[0]
    tm = _pick(n // nb, (FUSED_TILE[kind], 512, 256, 128))
    n_tiles = n // tm
    tiles_per_mod = n_tiles // nb

    row = lambda i, j: (i, 0)
    col_blk = lambda i, j: (0, j)
    row_blk = lambda i, j: (j, 0)
    common_in = [pl.BlockSpec((tm, d), row),
                 pl.BlockSpec((1, N_MOD, d), lambda i, j: (i // tiles_per_mod, 0, 0))]
    if kind == "ffn":
        wg, wu, wd = weights
        f = wg.shape[-1]
        tc = _pick(f, (FUSED_CHUNK, 256, 128))
        n_chunks = f // tc
        w_in = [_lead_spec(lead, (d, tc), col_blk), _lead_spec(lead, (d, tc), col_blk),
                _lead_spec(lead, (tc, d), row_blk)]
        w_args = [wg, wu, wd]
        w_bytes = 3 * d * tc * 2
    elif kind == "conv":
        w_in3, cw, wd = weights
        tc = _pick(d, (FUSED_CHUNK, 256, 128))
        n_chunks = d // tc
        w_in = [_lead_spec(lead, (d, tc), col_blk),
                _lead_spec(lead, (d, tc), lambda i, j: (0, n_chunks + j)),
                _lead_spec(lead, (d, tc), lambda i, j: (0, 2 * n_chunks + j)),
                _lead_spec(lead, (3, tc), col_blk),
                _lead_spec(lead, (tc, d), row_blk)]
        w_args = [w_in3, w_in3, w_in3, cw, wd]
        w_bytes = 4 * d * tc * 2
    else:
        w_in2, ws, bs, wd = weights
        groups = ws.shape[-3]
        gw = wd.shape[-2] // groups
        per_step = max(g for g in range(1, groups + 1) if groups % g == 0 and g * gw <= 1024)
        tc = per_step * gw
        n_chunks = groups // per_step
        w_in = [_lead_spec(lead, (d, tc), col_blk),
                pl.BlockSpec((tm, tc), lambda i, j: (i, j)),
                _lead_spec(lead, (per_step, CHUNK, CHUNK), lambda i, j: (j, 0, 0)),
                _lead_spec(lead, (per_step, CHUNK, 1), lambda i, j: (j, 0, 0)),
                _lead_spec(lead, (tc, d), row_blk)]
        w_args = [w_in2, vn, ws, bs, wd]
        w_bytes = 2 * d * tc * 2 + tm * tc * 2
    vec = pl.BlockSpec((1, d), lambda i, j: (0, 0))
    rb = min(tm, FUSED_ROW_BLOCK[kind])
    est = (2 * 2 * tm * d * 4
           + tm * d * 2
           + 2 * w_bytes
           + 6 * rb * tc * 4
           + 2 * rb * d * 4)
    kern = functools.partial(_fused_kernel, kind, k_shift, k_gate, res_scale, alpha, period,
                             n_chunks)
    return pl.pallas_call(
        kern,
        grid=(n_tiles, n_chunks),
        in_specs=common_in + w_in + [vec, vec],
        out_specs=pl.BlockSpec((tm, d), row),
        out_shape=jax.ShapeDtypeStruct((n, d), F32),
        scratch_shapes=[pltpu.VMEM((tm, d), BF16)],
        compiler_params=_params(("parallel", "arbitrary"), est),
        name="fused_" + kind,
    )(h, mods, *w_args, ln_g.reshape(1, d), ln_b.reshape(1, d))


def _qkv_kernel(k_shift, n_rope, h_ref, mods_ref, w_ref, tab_ref, o_ref, vt_ref, a_ref):
    s = pl.program_id(1)
    n_sec = pl.num_programs(1)
    cos_ref, sin_ref = tab_ref.at[0], tab_ref.at[1]

    @pl.when(s == 0)
    def _():
        a_ref[...] = _modulate_bf16(h_ref[...], mods_ref, k_shift)

    def project(rope, transposed):
        width = w_ref.shape[1]
        slab = _pick(width, (512, 256, 128))
        for c in range(width // slab):
            y = _dot(a_ref[...], w_ref[:, c * slab:(c + 1) * slab])
            if transposed:
                vt_ref[c * slab:(c + 1) * slab, :] = y.T.astype(vt_ref.dtype)
                continue
            for r in range(slab // V7X_LANES):
                yr = y[:, r * V7X_LANES:(r + 1) * V7X_LANES]
                if rope:
                    yr = yr * cos_ref[...] + pltpu.roll(yr, V7X_LANES // 2, 1) * sin_ref[...]
                lo = c * slab + r * V7X_LANES
                o_ref[:, lo:lo + V7X_LANES] = yr.astype(o_ref.dtype)

    if n_rope > 0:
        @pl.when(s < n_rope)
        def _():
            project(True, False)

    @pl.when(jnp.logical_and(s >= n_rope, s < n_sec - 1))
    def _():
        project(False, False)

    @pl.when(s == n_sec - 1)
    def _():
        project(False, True)


def _qkv_call(h, mods, w, lead, tabs, *, k_shift, n_rope, seq, first_sec=0):
    n, d = h.shape
    n_tab = tabs.shape[0]
    nb = mods.shape[0]
    n_sec = w.shape[-1] // d - first_sec
    tm = _pick(min(n // nb, seq), (512, 256, 128))
    n_tiles = n // tm
    tiles_per_mod = n_tiles // nb
    tiles_per_seq = seq // tm
    est = (2 * tm * d * 4 + tm * d * 2 + 2 * d * d * 2 + 2 * 2 * tm * d * 2
           + 8 * tm * V7X_LANES * 4 + 4 * tm * 512 * 4)
    kern = functools.partial(_qkv_kernel, k_shift, n_rope)
    return pl.pallas_call(
        kern,
        grid=(n_tiles, n_sec),
        in_specs=[pl.BlockSpec((tm, d), lambda i, s: (i, 0)),
                  pl.BlockSpec((1, N_MOD, d), lambda i, s: (i // tiles_per_mod, 0, 0)),
                  _lead_spec(lead, (d, d), lambda i, s: (0, first_sec + s)),
                  pl.BlockSpec((None, 2, tm, V7X_LANES),
                               lambda i, s: (jnp.minimum(s, n_tab - 1), 0, i % tiles_per_seq, 0))],
        out_specs=[pl.BlockSpec((tm, d), lambda i, s: (i, jnp.minimum(s, n_sec - 2))),
                   pl.BlockSpec((None, d, tm), lambda i, s: (i, 0, 0))],
        out_shape=[jax.ShapeDtypeStruct((n, (n_sec - 1) * d), BF16),
                   jax.ShapeDtypeStruct((n_tiles, d, tm), BF16)],
        scratch_shapes=[pltpu.VMEM((tm, d), BF16)],
        compiler_params=_params(("parallel", "arbitrary"), est),
        name="qkv_proj",
    )(h, mods, w, tabs)


def _row_blocks(n_rows, block):
    block = min(n_rows, block)
    return [slice(r, r + block) for r in range(0, n_rows, block)]


def _gate_branch_kernel(k_shift, h_ref, mods_ref, w_ref, g_ref, b_ref, o_ref):
    for rows in _row_blocks(h_ref.shape[0], PROJ_ROW_BLOCK):
        a = _modulate_bf16(h_ref[rows, :], mods_ref, k_shift)
        v = _gelu(_dot(a, w_ref[...]))
        o_ref[rows, :] = _layer_norm(v, g_ref[...], b_ref[...]).astype(o_ref.dtype)


def _gate_branch_call(h, mods, w, lead, g, b, *, k_shift):
    n, d = h.shape
    nb = mods.shape[0]
    width = w.shape[-1] // 2
    tm = _pick(n // nb, (512, 256, 128))
    n_tiles = n // tm
    tiles_per_mod = n_tiles // nb
    est = 2 * tm * d * 4 + 2 * d * width * 2 + 2 * tm * width * 2 + 4 * tm * width * 4
    return pl.pallas_call(
        functools.partial(_gate_branch_kernel, k_shift),
        grid=(n_tiles,),
        in_specs=[pl.BlockSpec((tm, d), lambda i: (i, 0)),
                  pl.BlockSpec((1, N_MOD, d), lambda i: (i // tiles_per_mod, 0, 0)),
                  _lead_spec(lead, (d, width), lambda i: (0, 1)),
                  pl.BlockSpec((1, width), lambda i: (0, 0)),
                  pl.BlockSpec((1, width), lambda i: (0, 0))],
        out_specs=pl.BlockSpec((tm, width), lambda i: (i, 0)),
        out_shape=jax.ShapeDtypeStruct((n, width), BF16),
        compiler_params=_params(("parallel",), est),
        name="gmlp_gate_branch",
    )(h, mods, w, g.reshape(1, width), b.reshape(1, width))


def _attn_kernel(lam_init, lam_ref, q_ref, k_ref, vt_ref, kc_ref, vct_ref, subln_ref, o_ref,
                 acc_ref, s_ref):
    dk = q_ref.shape[1] // 2
    tq = q_ref.shape[0]
    tk = vt_ref.shape[2]
    q = q_ref[...]
    qm = (q[:, :dk], q[:, dk:])
    nt = (((1,), (1,)), ((), ()))

    def update(carry, blocks):
        for g, (k_blk, _) in enumerate(blocks):
            for mp in range(2):
                s_ref[g, mp, 0:k_blk.shape[0], :] = lax.dot_general(
                    k_blk[:, mp * dk:(mp + 1) * dk], qm[mp], nt, preferred_element_type=F32)
        carry = list(carry)
        for g, (k_blk, vt_blk) in enumerate(blocks):
            for mp in range(2):
                m_old, l_old = carry[2 * mp], carry[2 * mp + 1]
                st = s_ref[g, mp, 0:k_blk.shape[0], :]
                m_new = jnp.maximum(m_old, jnp.max(st, axis=0, keepdims=True))
                pt = jnp.exp2(st - m_new)
                corr = jnp.exp2(m_old - m_new)
                carry[2 * mp] = m_new
                carry[2 * mp + 1] = corr * l_old + jnp.sum(pt, axis=0, keepdims=True)
                acc_ref[mp] = corr * acc_ref[mp] + _dot(vt_blk, pt.astype(BF16))
        return tuple(carry)

    acc_ref[...] = jnp.zeros_like(acc_ref)
    neg = jnp.full((1, tq), -jnp.inf, F32)
    zero = jnp.zeros((1, tq), F32)
    group = s_ref.shape[0]

    def body(c, carry):
        blocks = []
        for g in range(group):
            start = pl.multiple_of((c * group + g) * tk, tk)
            blocks.append((k_ref[pl.ds(start, tk), :], vt_ref[c * group + g]))
        return update(carry, blocks)

    carry = lax.fori_loop(0, vt_ref.shape[0] // group, body, (neg, zero, neg, zero))
    tc = vct_ref.shape[2]
    carry = update(carry, [(kc_ref[c * tc:(c + 1) * tc, :], vct_ref[c])
                           for c in range(vct_ref.shape[0])])
    _, l0, _, l1 = carry

    lp = lam_ref[...]
    lam = (jnp.exp(jnp.sum(lp[0:1, :] * lp[1:2, :], axis=-1, keepdims=True))
           - jnp.exp(jnp.sum(lp[2:3, :] * lp[3:4, :], axis=-1, keepdims=True)) + lam_init)
    ot = acc_ref[0] / l0 - lam * (acc_ref[1] / l1)
    ot = ot * lax.rsqrt(jnp.mean(ot * ot, axis=0, keepdims=True) + LN_EPS) * subln_ref[...]
    o_ref[...] = (ot * (1.0 - lam_init)).T.astype(o_ref.dtype)


def _attn_call(qk, vt, kc, vct, lam_p, subln, *, n_batch, seq, ctx_len, heads, lam_init):
    d = qk.shape[1] // 2
    dv = d // heads
    tk = vt.shape[2]
    tc = vct.shape[2]
    nk = seq // tk
    nc = ctx_len // tc
    group = _pick(nk, (ATTN_GROUP, 1))
    assert nc <= group and tc <= tk, "context keys must fit one score-scratch group"
    tq = _pick(seq, (ATTN_TQ, 128))
    nq = seq // tq
    est = (2 * 2 * tq * dv * 2 + 2 * 2 * seq * dv * 2 + 2 * 2 * ctx_len * dv * 2
           + 2 * tq * dv * 4 + (2 * group + 6) * tq * tk * 4)
    kern = functools.partial(_attn_kernel, lam_init)
    return pl.pallas_call(
        kern,
        grid=(n_batch, heads, nq),
        in_specs=[pl.BlockSpec(lam_p.shape, lambda b, h, i: (0, 0)),
                  pl.BlockSpec((tq, dv), lambda b, h, i: (b * nq + i, h)),
                  pl.BlockSpec((seq, dv), lambda b, h, i: (b, heads + h)),
                  pl.BlockSpec((nk, dv, tk), lambda b, h, i: (b, h, 0)),
                  pl.BlockSpec((ctx_len, dv), lambda b, h, i: (b, h)),
                  pl.BlockSpec((nc, dv, tc), lambda b, h, i: (b, h, 0)),
                  pl.BlockSpec((dv, 1), lambda b, h, i: (0, 0))],
        out_specs=pl.BlockSpec((tq, dv), lambda b, h, i: (b * nq + i, h)),
        out_shape=jax.ShapeDtypeStruct((n_batch * seq, d), BF16),
        scratch_shapes=[pltpu.VMEM((2, dv, tq), F32), pltpu.VMEM((group, 2, tk, tq), F32)],
        compiler_params=_params(("parallel", "parallel", "arbitrary"), est),
        name="diff_attention",
    )(lam_p, qk, qk, vt, kc, vct, subln.reshape(dv, 1))


def _out_proj_kernel(k_gate, alpha, y_ref, h_ref, mods_ref, w_ref, g_ref, b_ref, o_ref):
    for rows in _row_blocks(h_ref.shape[0], PROJ_ROW_BLOCK):
        y = _dot(y_ref[rows, :], w_ref[...])
        x = alpha * h_ref[rows, :] + _mod_row(mods_ref, k_gate) * y
        o_ref[rows, :] = _layer_norm(x, g_ref[...], b_ref[...])


def _out_proj_call(y, h, mods, w, lead, g, b, *, k_gate, alpha):
    n, d = h.shape
    nb = mods.shape[0]
    tm = _pick(n // nb, (512, 256, 128))
    n_tiles = n // tm
    tiles_per_mod = n_tiles // nb
    est = 2 * tm * d * 2 + 2 * 2 * tm * d * 4 + 2 * d * d * 2 + 4 * tm * d * 4
    return pl.pallas_call(
        functools.partial(_out_proj_kernel, k_gate, alpha),
        grid=(n_tiles,),
        in_specs=[pl.BlockSpec((tm, d), lambda i: (i, 0)),
                  pl.BlockSpec((tm, d), lambda i: (i, 0)),
                  pl.BlockSpec((1, N_MOD, d), lambda i: (i // tiles_per_mod, 0, 0)),
                  _lead_spec(lead, (d, d), lambda i: (0, 0)),
                  pl.BlockSpec((1, d), lambda i: (0, 0)),
                  pl.BlockSpec((1, d), lambda i: (0, 0))],
        out_specs=pl.BlockSpec((tm, d), lambda i: (i, 0)),
        out_shape=jax.ShapeDtypeStruct((n, d), F32),
        compiler_params=_params(("parallel",), est),
        name="out_proj_norm",
    )(y, h, mods, w, g.reshape(1, d), b.reshape(1, d))


def _rope_tables(seq, dk):
    n_freq = dk // 4
    t = jnp.arange(seq)
    inv = ROPE_BASE ** (-jnp.arange(n_freq, dtype=F32) / n_freq)
    ang_r = (t // GRID_W).astype(F32)[:, None] * inv
    ang_c = (t % GRID_W).astype(F32)[:, None] * inv
    cos_t = jnp.concatenate([jnp.cos(ang_r), jnp.cos(ang_c)] * 2, axis=-1)
    sin_t = jnp.concatenate([-jnp.sin(ang_r), -jnp.sin(ang_c), jnp.sin(ang_r), jnp.sin(ang_c)], axis=-1)
    k_tab = jnp.stack([cos_t, sin_t])
    return jnp.stack([k_tab * (dk ** -0.5 * math.log2(math.e)), k_tab])


def _rope_column_layout(w, dk):
    rows, width = w.shape
    w = w.reshape(rows, width // dk, 2, 2, dk // 4)
    return jnp.swapaxes(w, 2, 3).reshape(rows, width)


def kernel(x, c, ctx, c_ctx, ada_w, ada_b, ln_g, ln_b, ffn_wg, ffn_wu, ffn_wd, sc_w_in, sc_conv,
           sc_w_out, da_w_qkv, da_lambda, da_subln, da_w_o, gm_w_in, gm_ln_g, gm_ln_b, gm_w_s,
           gm_b_s, gm_w_out):
    n_batch, seq, d = x.shape
    ctx_len = ctx.shape[1]
    depth = ada_w.shape[0]
    mixer_of_layer = tuple(i % N_MIXERS for i in range(depth))
    last_ctx_layer = max([i for i in range(depth) if mixer_of_layer[i] == 1], default=-1)
    alpha = (2.0 * depth) ** 0.25
    dv = da_subln.shape[-1]
    heads = d // dv
    dk = dv // 2

    h = x.reshape(n_batch * seq, d)
    hc = ctx.reshape(n_batch * ctx_len, d)

    n_cond = n_batch + 1
    cond = jnp.zeros((16 * ((n_cond + 15) // 16), d), F32)
    cond = cond.at[:n_batch].set(c).at[n_batch].set(c_ctx)

    rope_tabs = _rope_tables(seq, dk)
    no_rope_tabs = jnp.zeros((1, 2, ctx_len, dk), F32)

    ffn_w = (ffn_wg.astype(BF16), ffn_wu.astype(BF16), ffn_wd.astype(BF16))
    conv_w = (sc_w_in.astype(BF16), sc_conv, sc_w_out.astype(BF16))
    qkv_w = da_w_qkv.astype(BF16)
    qkv_w = jnp.concatenate(
        [_rope_column_layout(qkv_w[..., :2 * d].reshape(-1, 2 * d), dk).reshape(qkv_w.shape[0], d, 2 * d),
         qkv_w[..., 2 * d:]], axis=-1)
    attn_wo = da_w_o.astype(BF16)
    gmlp_w = (gm_w_in.astype(BF16), gm_w_s.astype(BF16), gm_b_s[..., None], gm_w_out.astype(BF16))

    for i in range(depth):
        kind = mixer_of_layer[i]
        j = i // N_MIXERS
        ctx_in = i <= last_ctx_layer
        ctx_out = i < last_ctx_layer
        mods_all = _ada(cond, ada_w, ada_b, i).reshape(-1, N_MOD, d)
        md = mods_all[:n_batch]
        mdc = mods_all[n_batch:n_batch + 1]

        def ffn(hh, mm, half, k0):
            return _fused_call("ffn", hh, mm, ffn_w, (i, half), ln_g[i, 2 * half],
                               ln_b[i, 2 * half], k_shift=k0, k_gate=k0 + 2, res_scale=0.5,
                               alpha=alpha)

        h = ffn(h, md, 0, 0)
        if ctx_in:
            hc = ffn(hc, mdc, 0, 0)

        if kind == 0:
            conv = functools.partial(_fused_call, "conv", weights=conv_w, lead=(j,),
                                     ln_g=ln_g[i, 1], ln_b=ln_b[i, 1], k_shift=3, k_gate=5,
                                     res_scale=1.0, alpha=alpha)
            h = conv(h, md, period=GRID_W)
            if ctx_out:
                hc = conv(hc, mdc, period=ctx_len)
        elif kind == 1:
            lam_init = 0.8 - 0.6 * math.exp(-0.3 * i)
            qk, vt = _qkv_call(h, md, qkv_w, (j,), rope_tabs, k_shift=3, n_rope=2, seq=seq)
            kc, vct = _qkv_call(hc, mdc, qkv_w, (j,), no_rope_tabs, k_shift=3, n_rope=0,
                                seq=ctx_len, first_sec=1)
            o = _attn_call(qk, vt, kc, vct, da_lambda[j], da_subln[j], n_batch=n_batch, seq=seq,
                           ctx_len=ctx_len, heads=heads, lam_init=lam_init)
            h = _out_proj_call(o, h, md, attn_wo, (j,), ln_g[i, 1], ln_b[i, 1], k_gate=5,
                               alpha=alpha)
            assert not ctx_out, "context-side attention output is not implemented"
        else:
            def gmlp(hh, mm):
                vn = _gate_branch_call(hh, mm, gmlp_w[0], (j,), gm_ln_g[j], gm_ln_b[j], k_shift=3)
                return _fused_call("gmlp", hh, mm, gmlp_w, (j,), ln_g[i, 1], ln_b[i, 1],
                                   k_shift=3, k_gate=5, res_scale=1.0, alpha=alpha, vn=vn)

            h = gmlp(h, md)
            if ctx_out:
                hc = gmlp(hc, mdc)

        h = ffn(h, md, 1, 6)
        if ctx_out:
            hc = ffn(hc, mdc, 1, 6)
    return h.reshape(n_batch, seq, d)
```

```python
import functools
import math

import jax
import jax.numpy as jnp
from jax import lax
from jax.experimental import pallas as pl
from jax.experimental.pallas import tpu as pltpu

GRID_W = 64
CHUNK = 128
N_MOD = 9
N_MIXERS = 3
ROPE_BASE = 10000.0
LN_EPS = 1e-5

V7X_LANES = 128
V7X_VMEM_BYTES = 64 * 1024 * 1024
V7X_VMEM_CAP = V7X_VMEM_BYTES - 6 * 1024 * 1024

BF16 = jnp.bfloat16
F32 = jnp.float32

ATTN_TQ = 1024
ATTN_GROUP = 4
FUSED_CHUNK = 512
CAST_ROWS = 16
FUSED_TILE = {"ffn": 1024, "conv": 1024, "gmlp": 512}
FUSED_ROW_BLOCK = {"ffn": 512, "conv": 512, "gmlp": 256}
PROJ_ROW_BLOCK = 256


def _params(semantics, vmem_estimate):
    limit = min(V7X_VMEM_CAP, max(32 * 1024 * 1024, int(vmem_estimate * 1.3)))
    return pltpu.CompilerParams(dimension_semantics=semantics, vmem_limit_bytes=limit)


def _pick(n, candidates):
    for c in candidates:
        if n % c == 0:
            return c
    return n


def _mod_row(mods_ref, k):
    return mods_ref[0, k:k + 1, :]


def _modulate_bf16(h, mods_ref, k_shift):
    shift = _mod_row(mods_ref, k_shift)
    scale = _mod_row(mods_ref, k_shift + 1)
    return (h * (1.0 + scale) + shift).astype(BF16)


def _layer_norm(x, g, b):
    mu = jnp.mean(x, axis=-1, keepdims=True)
    xc = x - mu
    var = jnp.mean(xc * xc, axis=-1, keepdims=True)
    return xc * lax.rsqrt(var + LN_EPS) * g + b


def _silu(x):
    return x / (1.0 + jnp.exp(-x))


def _gelu(x):
    return 0.5 * x * (1.0 + lax.erf(x * math.sqrt(0.5)))


def _dot(a, b):
    return jnp.dot(a, b, preferred_element_type=F32)


def _ada_kernel(cond_ref, w_ref, b_ref, o_ref):
    a = _silu(cond_ref[...]).astype(BF16)
    o_ref[...] = _dot(a, w_ref[...].astype(BF16)) + b_ref[...]


def _ada(cond, w, b, layer):
    m, d = cond.shape
    n = w.shape[2]
    tn = _pick(n, (1024, 512, 256, 128))
    est = 2 * d * tn * 4 + d * tn * 2 + 4 * m * (d + tn) * 4
    return pl.pallas_call(
        _ada_kernel,
        grid=(n // tn,),
        in_specs=[pl.BlockSpec((m, d), lambda j: (0, 0)),
                  pl.BlockSpec((None, d, tn), lambda j: (layer, 0, j)),
                  pl.BlockSpec((None, 1, tn), lambda j: (layer, 0, j))],
        out_specs=pl.BlockSpec((m, tn), lambda j: (0, j)),
        out_shape=jax.ShapeDtypeStruct((m, n), F32),
        compiler_params=_params(("parallel",), est),
        name="ada_mod",
    )(cond, w, b.reshape(b.shape[0], 1, n))


def _fused_kernel(kind, k_shift, k_gate, res_scale, alpha, period, n_steps, n_cast, *refs):
    h_ref, mods_ref = refs[0], refs[1]
    n_in = len(refs) - 2 - n_cast
    w = refs[2:n_in - 2 - n_cast]
    lng_ref, lnb_ref = refs[n_in - 2 - n_cast:n_in - n_cast]
    cast_in = refs[n_in - n_cast:n_in]
    o_ref = refs[n_in]
    cast_out = refs[n_in + 1:n_in + 1 + n_cast]
    a_ref = refs[-1]
    acc_ref = o_ref
    wd_ref = w[-1]
    tm = h_ref.shape[0]


    def row_blocks(size):
        rb = min(tm, max(size, period))
        return [slice(r, r + rb) for r in range(0, tm, rb)]

    def up(a, rows):
        if kind == "ffn":
            return _dot(a, w[0][...]), _dot(a, w[1][...])
        if kind == "conv":
            return _dot(a, w[0][...]), _dot(a, w[1][...]), _dot(a, w[2][...])
        vn_ref, ws_ref, bs_ref = w[1], w[2], w[3]
        gw = w[0].shape[1] // ws_ref.shape[0]
        cols = []
        for g in range(ws_ref.shape[0]):
            parts = [_dot(ws_ref[g], vn_ref[c:c + CHUNK, g * gw:(g + 1) * gw]) + bs_ref[g]
                     for c in range(rows.start, rows.stop, CHUNK)]
            cols.append(jnp.concatenate(parts, axis=0))
        return _dot(a, w[0][...]), jnp.concatenate(cols, axis=1)

    def hidden(pre):
        if kind == "ffn":
            g, u = pre
            return _silu(g) * u
        if kind == "conv":
            b, c, v = pre
            z = c * v
            pos = lax.broadcasted_iota(jnp.int32, z.shape, 0) & (period - 1)
            z_prev = jnp.where(pos == 0, 0.0, pltpu.roll(z, 1, 0))
            z_next = jnp.where(pos == period - 1, 0.0, pltpu.roll(z, z.shape[0] - 1, 0))
            cw = w[3][...]
            return b * (cw[0:1, :] * z_prev + cw[1:2, :] * z + cw[2:3, :] * z_next)
        u, s = pre
        return _gelu(u) * s

    def step(first, last):
        for src, dst in zip(cast_in, cast_out):
            dst[...] = src[...].astype(dst.dtype)

        def start(rows):
            if first:
                a = _modulate_bf16(h_ref[rows, :], mods_ref, k_shift)
                if not last:
                    a_ref[rows, :] = a
            else:
                a = a_ref[rows, :]
            return up(a, rows)

        def finish(rows, pre):
            acc = _dot(hidden(pre).astype(BF16), wd_ref[...])
            if not first:
                acc = acc_ref[rows, :] + acc
            if last:
                y = alpha * h_ref[rows, :] + (res_scale * _mod_row(mods_ref, k_gate)) * acc
                o_ref[rows, :] = _layer_norm(y, lng_ref[...], lnb_ref[...])
            else:
                acc_ref[rows, :] = acc

        pending = None
        for rows in row_blocks(FUSED_ROW_BLOCK[kind]):
            pre = start(rows)
            if pending is not None:
                finish(*pending)
            pending = (rows, pre)
        finish(*pending)

    if n_steps == 1:
        step(True, True)
        return
    j = pl.program_id(1)
    pl.when(j == 0)(functools.partial(step, True, False))
    if n_steps > 2:
        pl.when(jnp.logical_and(j > 0, j < n_steps - 1))(functools.partial(step, False, False))
    pl.when(j == n_steps - 1)(functools.partial(step, False, True))


def _lead_spec(lead, block, index_fn):
    lead = tuple(lead)
    return pl.BlockSpec((None,) * len(lead) + tuple(block),
                        lambda *g: lead + tuple(index_fn(*g)))


def _fused_call(kind, h, mods, weights, lead, ln_g, ln_b, *, k_shift, k_gate, res_scale, alpha,
                period=GRID_W, vn=None, cast=None):
    n, d = h.shape
    nb = mods.shape[0]
    tm = _pick(n // nb, (FUSED_TILE[kind], 512, 256, 128))
    n_tiles = n // tm
    tiles_per_mod = n_tiles // nb

    row = lambda i, j: (i, 0)
    col_blk = lambda i, j: (0, j)
    row_blk = lambda i, j: (j, 0)
    common_in = [pl.BlockSpec((tm, d), row),
                 pl.BlockSpec((1, N_MOD, d), lambda i, j: (i // tiles_per_mod, 0, 0))]
    if kind == "ffn":
        wg, wu, wd = weights
        f = wg.shape[-1]
        tc = _pick(f, (FUSED_CHUNK, 256, 128))
        n_chunks = f // tc
        w_in = [_lead_spec(lead, (d, tc), col_blk), _lead_spec(lead, (d, tc), col_blk),
                _lead_spec(lead, (tc, d), row_blk)]
        w_args = [wg, wu, wd]
        w_bytes = 3 * d * tc * 2
    elif kind == "conv":
        w_in3, cw, wd = weights
        tc = _pick(d, (FUSED_CHUNK, 256, 128))
        n_chunks = d // tc
        w_in = [_lead_spec(lead, (d, tc), col_blk),
                _lead_spec(lead, (d, tc), lambda i, j: (0, n_chunks + j)),
                _lead_spec(lead, (d, tc), lambda i, j: (0, 2 * n_chunks + j)),
                _lead_spec(lead, (3, tc), col_blk),
                _lead_spec(lead, (tc, d), row_blk)]
        w_args = [w_in3, w_in3, w_in3, cw, wd]
        w_bytes = 4 * d * tc * 2
    else:
        w_in2, ws, bs, wd = weights
        groups = ws.shape[-3]
        gw = wd.shape[-2] // groups
        per_step = max(g for g in range(1, groups + 1) if groups % g == 0 and g * gw <= 1024)
        tc = per_step * gw
        n_chunks = groups // per_step
        w_in = [_lead_spec(lead, (d, tc), col_blk),
                pl.BlockSpec((tm, tc), lambda i, j: (i, j)),
                _lead_spec(lead, (per_step, CHUNK, CHUNK), lambda i, j: (j, 0, 0)),
                _lead_spec(lead, (per_step, CHUNK, 1), lambda i, j: (j, 0, 0)),
                _lead_spec(lead, (tc, d), row_blk)]
        w_args = [w_in2, vn, ws, bs, wd]
        w_bytes = 2 * d * tc * 2 + tm * tc * 2
    vec = pl.BlockSpec((1, d), lambda i, j: (0, 0))
    rb = min(tm, FUSED_ROW_BLOCK[kind])
    est = (2 * 2 * tm * d * 4
           + tm * d * 2
           + 2 * w_bytes
           + 6 * rb * tc * 4
           + 2 * rb * d * 4)
    cast_arrays, cast_lead = cast if cast is not None else ((), ())
    n_steps = n_tiles * n_chunks
    step_blk = lambda i, j: (i * n_chunks + j, 0, 0)
    cast_views, cast_in, cast_out, cast_shapes = [], [], [], []
    for arr in cast_arrays:
        slab = arr.shape[-2] * arr.shape[-1] // (n_steps * CAST_ROWS)
        assert slab * n_steps * CAST_ROWS == arr.shape[-2] * arr.shape[-1] and slab % V7X_LANES == 0
        cast_views.append(arr.reshape(arr.shape[:-2] + (n_steps, CAST_ROWS, slab)))
        cast_in.append(_lead_spec(cast_lead, (None, CAST_ROWS, slab), step_blk))
        cast_out.append(pl.BlockSpec((None, CAST_ROWS, slab), step_blk))
        cast_shapes.append(jax.ShapeDtypeStruct((n_steps, CAST_ROWS, slab), BF16))
        est += 2 * CAST_ROWS * slab * (4 + 2)
    kern = functools.partial(_fused_kernel, kind, k_shift, k_gate, res_scale, alpha, period,
                             n_chunks, len(cast_arrays))
    outs = pl.pallas_call(
        kern,
        grid=(n_tiles, n_chunks),
        in_specs=common_in + w_in + [vec, vec] + cast_in,
        out_specs=[pl.BlockSpec((tm, d), row)] + cast_out,
        out_shape=[jax.ShapeDtypeStruct((n, d), F32)] + cast_shapes,
        scratch_shapes=[pltpu.VMEM((tm, d), BF16)],
        compiler_params=_params(("parallel", "arbitrary"), est),
        name="fused_" + kind,
    )(h, mods, *w_args, ln_g.reshape(1, d), ln_b.reshape(1, d), *cast_views)
    if cast is None:
        return outs[0]
    return outs[0], [o.reshape(a.shape[-2:]) for o, a in zip(outs[1:], cast_arrays)]


def _qkv_kernel(k_shift, n_rope, h_ref, mods_ref, w_ref, tab_ref, o_ref, vt_ref, a_ref):
    s = pl.program_id(1)
    n_sec = pl.num_programs(1)
    cos_ref, sin_ref = tab_ref.at[0], tab_ref.at[1]

    @pl.when(s == 0)
    def _():
        a_ref[...] = _modulate_bf16(h_ref[...], mods_ref, k_shift)

    def project(rope, transposed):
        width = w_ref.shape[1]
        slab = _pick(width, (512, 256, 128))
        for c in range(width // slab):
            y = _dot(a_ref[...], w_ref[:, c * slab:(c + 1) * slab])
            if transposed:
                vt_ref[c * slab:(c + 1) * slab, :] = y.T.astype(vt_ref.dtype)
                continue
            for r in range(slab // V7X_LANES):
                yr = y[:, r * V7X_LANES:(r + 1) * V7X_LANES]
                if rope:
                    yr = yr * cos_ref[...] + pltpu.roll(yr, V7X_LANES // 2, 1) * sin_ref[...]
                lo = c * slab + r * V7X_LANES
                o_ref[:, lo:lo + V7X_LANES] = yr.astype(o_ref.dtype)

    if n_rope > 0:
        @pl.when(s < n_rope)
        def _():
            project(True, False)

    @pl.when(jnp.logical_and(s >= n_rope, s < n_sec - 1))
    def _():
        project(False, False)

    @pl.when(s == n_sec - 1)
    def _():
        project(False, True)


def _qkv_call(h, mods, w, lead, tabs, *, k_shift, n_rope, seq, first_sec=0):
    n, d = h.shape
    n_tab = tabs.shape[0]
    nb = mods.shape[0]
    n_sec = w.shape[-1] // d - first_sec
    tm = _pick(min(n // nb, seq), (512, 256, 128))
    n_tiles = n // tm
    tiles_per_mod = n_tiles // nb
    tiles_per_seq = seq // tm
    est = (2 * tm * d * 4 + tm * d * 2 + 2 * d * d * 2 + 2 * 2 * tm * d * 2
           + 8 * tm * V7X_LANES * 4 + 4 * tm * 512 * 4)
    kern = functools.partial(_qkv_kernel, k_shift, n_rope)
    return pl.pallas_call(
        kern,
        grid=(n_tiles, n_sec),
        in_specs=[pl.BlockSpec((tm, d), lambda i, s: (i, 0)),
                  pl.BlockSpec((1, N_MOD, d), lambda i, s: (i // tiles_per_mod, 0, 0)),
                  _lead_spec(lead, (d, d), lambda i, s: (0, first_sec + s)),
                  pl.BlockSpec((None, 2, tm, V7X_LANES),
                               lambda i, s: (jnp.minimum(s, n_tab - 1), 0, i % tiles_per_seq, 0))],
        out_specs=[pl.BlockSpec((tm, d), lambda i, s: (i, jnp.minimum(s, n_sec - 2))),
                   pl.BlockSpec((None, d, tm), lambda i, s: (i, 0, 0))],
        out_shape=[jax.ShapeDtypeStruct((n, (n_sec - 1) * d), BF16),
                   jax.ShapeDtypeStruct((n_tiles, d, tm), BF16)],
        scratch_shapes=[pltpu.VMEM((tm, d), BF16)],
        compiler_params=_params(("parallel", "arbitrary"), est),
        name="qkv_proj",
    )(h, mods, w, tabs)


def _row_blocks(n_rows, block):
    block = min(n_rows, block)
    return [slice(r, r + block) for r in range(0, n_rows, block)]


def _gate_branch_kernel(k_shift, h_ref, mods_ref, w_ref, g_ref, b_ref, o_ref):
    for rows in _row_blocks(h_ref.shape[0], PROJ_ROW_BLOCK):
        a = _modulate_bf16(h_ref[rows, :], mods_ref, k_shift)
        v = _gelu(_dot(a, w_ref[...]))
        o_ref[rows, :] = _layer_norm(v, g_ref[...], b_ref[...]).astype(o_ref.dtype)


def _gate_branch_call(h, mods, w, lead, g, b, *, k_shift):
    n, d = h.shape
    nb = mods.shape[0]
    width = w.shape[-1] // 2
    tm = _pick(n // nb, (512, 256, 128))
    n_tiles = n // tm
    tiles_per_mod = n_tiles // nb
    est = 2 * tm * d * 4 + 2 * d * width * 2 + 2 * tm * width * 2 + 4 * tm * width * 4
    return pl.pallas_call(
        functools.partial(_gate_branch_kernel, k_shift),
        grid=(n_tiles,),
        in_specs=[pl.BlockSpec((tm, d), lambda i: (i, 0)),
                  pl.BlockSpec((1, N_MOD, d), lambda i: (i // tiles_per_mod, 0, 0)),
                  _lead_spec(lead, (d, width), lambda i: (0, 1)),
                  pl.BlockSpec((1, width), lambda i: (0, 0)),
                  pl.BlockSpec((1, width), lambda i: (0, 0))],
        out_specs=pl.BlockSpec((tm, width), lambda i: (i, 0)),
        out_shape=jax.ShapeDtypeStruct((n, width), BF16),
        compiler_params=_params(("parallel",), est),
        name="gmlp_gate_branch",
    )(h, mods, w, g.reshape(1, width), b.reshape(1, width))


def _attn_kernel(lam_init, lam_ref, q_ref, k_ref, vt_ref, kc_ref, vct_ref, subln_ref, o_ref,
                 acc_ref, s_ref):
    dk = q_ref.shape[1] // 2
    tq = q_ref.shape[0]
    tk = vt_ref.shape[2]
    q = q_ref[...]
    qm = (q[:, :dk], q[:, dk:])
    nt = (((1,), (1,)), ((), ()))

    def update(carry, blocks):
        for g, (k_blk, _) in enumerate(blocks):
            for mp in range(2):
                s_ref[g, mp, 0:k_blk.shape[0], :] = lax.dot_general(
                    k_blk[:, mp * dk:(mp + 1) * dk], qm[mp], nt, preferred_element_type=F32)
        carry = list(carry)
        for g, (k_blk, vt_blk) in enumerate(blocks):
            for mp in range(2):
                m_old, l_old = carry[2 * mp], carry[2 * mp + 1]
                st = s_ref[g, mp, 0:k_blk.shape[0], :]
                m_new = jnp.maximum(m_old, jnp.max(st, axis=0, keepdims=True))
                pt = jnp.exp2(st - m_new)
                corr = jnp.exp2(m_old - m_new)
                carry[2 * mp] = m_new
                carry[2 * mp + 1] = corr * l_old + jnp.sum(pt, axis=0, keepdims=True)
                acc_ref[mp] = corr * acc_ref[mp] + _dot(vt_blk, pt.astype(BF16))
        return tuple(carry)

    acc_ref[...] = jnp.zeros_like(acc_ref)
    neg = jnp.full((1, tq), -jnp.inf, F32)
    zero = jnp.zeros((1, tq), F32)
    group = s_ref.shape[0]

    def body(c, carry):
        blocks = []
        for g in range(group):
            start = pl.multiple_of((c * group + g) * tk, tk)
            blocks.append((k_ref[pl.ds(start, tk), :], vt_ref[c * group + g]))
        return update(carry, blocks)

    carry = lax.fori_loop(0, vt_ref.shape[0] // group, body, (neg, zero, neg, zero))
    tc = vct_ref.shape[2]
    carry = update(carry, [(kc_ref[c * tc:(c + 1) * tc, :], vct_ref[c])
                           for c in range(vct_ref.shape[0])])
    _, l0, _, l1 = carry

    lp = lam_ref[...]
    lam = (jnp.exp(jnp.sum(lp[0:1, :] * lp[1:2, :], axis=-1, keepdims=True))
           - jnp.exp(jnp.sum(lp[2:3, :] * lp[3:4, :], axis=-1, keepdims=True)) + lam_init)
    ot = acc_ref[0] / l0 - lam * (acc_ref[1] / l1)
    ot = ot * lax.rsqrt(jnp.mean(ot * ot, axis=0, keepdims=True) + LN_EPS) * subln_ref[...]
    o_ref[...] = (ot * (1.0 - lam_init)).T.astype(o_ref.dtype)


def _attn_call(qk, vt, kc, vct, lam_p, subln, *, n_batch, seq, ctx_len, heads, lam_init):
    d = qk.shape[1] // 2
    dv = d // heads
    tk = vt.shape[2]
    tc = vct.shape[2]
    nk = seq // tk
    nc = ctx_len // tc
    group = _pick(nk, (ATTN_GROUP, 1))
    assert nc <= group and tc <= tk, "context keys must fit one score-scratch group"
    tq = _pick(seq, (ATTN_TQ, 128))
    nq = seq // tq
    est = (2 * 2 * tq * dv * 2 + 2 * 2 * seq * dv * 2 + 2 * 2 * ctx_len * dv * 2
           + 2 * tq * dv * 4 + (2 * group + 6) * tq * tk * 4)
    kern = functools.partial(_attn_kernel, lam_init)
    return pl.pallas_call(
        kern,
        grid=(n_batch, heads, nq),
        in_specs=[pl.BlockSpec(lam_p.shape, lambda b, h, i: (0, 0)),
                  pl.BlockSpec((tq, dv), lambda b, h, i: (b * nq + i, h)),
                  pl.BlockSpec((seq, dv), lambda b, h, i: (b, heads + h)),
                  pl.BlockSpec((nk, dv, tk), lambda b, h, i: (b, h, 0)),
                  pl.BlockSpec((ctx_len, dv), lambda b, h, i: (b, h)),
                  pl.BlockSpec((nc, dv, tc), lambda b, h, i: (b, h, 0)),
                  pl.BlockSpec((dv, 1), lambda b, h, i: (0, 0))],
        out_specs=pl.BlockSpec((tq, dv), lambda b, h, i: (b * nq + i, h)),
        out_shape=jax.ShapeDtypeStruct((n_batch * seq, d), BF16),
        scratch_shapes=[pltpu.VMEM((2, dv, tq), F32), pltpu.VMEM((group, 2, tk, tq), F32)],
        compiler_params=_params(("parallel", "parallel", "arbitrary"), est),
        name="diff_attention",
    )(lam_p, qk, qk, vt, kc, vct, subln.reshape(dv, 1))


def _out_proj_kernel(k_gate, alpha, y_ref, h_ref, mods_ref, w_ref, g_ref, b_ref, o_ref):
    for rows in _row_blocks(h_ref.shape[0], PROJ_ROW_BLOCK):
        y = _dot(y_ref[rows, :], w_ref[...])
        x = alpha * h_ref[rows, :] + _mod_row(mods_ref, k_gate) * y
        o_ref[rows, :] = _layer_norm(x, g_ref[...], b_ref[...])


def _out_proj_call(y, h, mods, w, lead, g, b, *, k_gate, alpha):
    n, d = h.shape
    nb = mods.shape[0]
    tm = _pick(n // nb, (512, 256, 128))
    n_tiles = n // tm
    tiles_per_mod = n_tiles // nb
    est = 2 * tm * d * 2 + 2 * 2 * tm * d * 4 + 2 * d * d * 2 + 4 * tm * d * 4
    return pl.pallas_call(
        functools.partial(_out_proj_kernel, k_gate, alpha),
        grid=(n_tiles,),
        in_specs=[pl.BlockSpec((tm, d), lambda i: (i, 0)),
                  pl.BlockSpec((tm, d), lambda i: (i, 0)),
                  pl.BlockSpec((1, N_MOD, d), lambda i: (i // tiles_per_mod, 0, 0)),
                  _lead_spec(lead, (d, d), lambda i: (0, 0)),
                  pl.BlockSpec((1, d), lambda i: (0, 0)),
                  pl.BlockSpec((1, d), lambda i: (0, 0))],
        out_specs=pl.BlockSpec((tm, d), lambda i: (i, 0)),
        out_shape=jax.ShapeDtypeStruct((n, d), F32),
        compiler_params=_params(("parallel",), est),
        name="out_proj_norm",
    )(y, h, mods, w, g.reshape(1, d), b.reshape(1, d))


def _rope_tables(seq, dk):
    n_freq = dk // 4
    t = jnp.arange(seq)
    inv = ROPE_BASE ** (-jnp.arange(n_freq, dtype=F32) / n_freq)
    ang_r = (t // GRID_W).astype(F32)[:, None] * inv
    ang_c = (t % GRID_W).astype(F32)[:, None] * inv
    cos_t = jnp.concatenate([jnp.cos(ang_r), jnp.cos(ang_c)] * 2, axis=-1)
    sin_t = jnp.concatenate([-jnp.sin(ang_r), -jnp.sin(ang_c), jnp.sin(ang_r), jnp.sin(ang_c)], axis=-1)
    k_tab = jnp.stack([cos_t, sin_t])
    return jnp.stack([k_tab * (dk ** -0.5 * math.log2(math.e)), k_tab])


def _rope_column_layout(w, dk):
    rows, width = w.shape
    w = w.reshape(rows, width // dk, 2, 2, dk // 4)
    return jnp.swapaxes(w, 2, 3).reshape(rows, width)


def kernel(x, c, ctx, c_ctx, ada_w, ada_b, ln_g, ln_b, ffn_wg, ffn_wu, ffn_wd, sc_w_in, sc_conv,
           sc_w_out, da_w_qkv, da_lambda, da_subln, da_w_o, gm_w_in, gm_ln_g, gm_ln_b, gm_w_s,
           gm_b_s, gm_w_out):
    n_batch, seq, d = x.shape
    ctx_len = ctx.shape[1]
    depth = ada_w.shape[0]
    mixer_of_layer = tuple(i % N_MIXERS for i in range(depth))
    last_ctx_layer = max([i for i in range(depth) if mixer_of_layer[i] == 1], default=-1)
    alpha = (2.0 * depth) ** 0.25
    dv = da_subln.shape[-1]
    heads = d // dv
    dk = dv // 2

    h = x.reshape(n_batch * seq, d)
    hc = ctx.reshape(n_batch * ctx_len, d)

    n_cond = n_batch + 1
    cond = jnp.zeros((16 * ((n_cond + 15) // 16), d), F32)
    cond = cond.at[:n_batch].set(c).at[n_batch].set(c_ctx)

    rope_tabs = _rope_tables(seq, dk)
    no_rope_tabs = jnp.zeros((1, 2, ctx_len, dk), F32)

    ffn_f32 = (ffn_wg, ffn_wu, ffn_wd)
    ffn_w = [tuple(w[0, 0].astype(BF16) for w in ffn_f32)]
    conv_w = (sc_w_in.astype(BF16), sc_conv, sc_w_out.astype(BF16))
    qkv_w = da_w_qkv.astype(BF16)
    qkv_w = jnp.concatenate(
        [_rope_column_layout(qkv_w[..., :2 * d].reshape(-1, 2 * d), dk).reshape(qkv_w.shape[0], d, 2 * d),
         qkv_w[..., 2 * d:]], axis=-1)
    attn_wo = da_w_o.astype(BF16)
    gmlp_w = (gm_w_in.astype(BF16), gm_w_s.astype(BF16), gm_b_s[..., None], gm_w_out.astype(BF16))

    for i in range(depth):
        kind = mixer_of_layer[i]
        j = i // N_MIXERS
        ctx_in = i <= last_ctx_layer
        ctx_out = i < last_ctx_layer
        mods_all = _ada(cond, ada_w, ada_b, i).reshape(-1, N_MOD, d)
        md = mods_all[:n_batch]
        mdc = mods_all[n_batch:n_batch + 1]

        def ffn_pair(h_lat, h_ctx, half, k0):
            nxt = (i, 1) if half == 0 else (i + 1, 0)
            args = dict(ln_g=ln_g[i, 2 * half], ln_b=ln_b[i, 2 * half], k_shift=k0,
                        k_gate=k0 + 2, res_scale=0.5, alpha=alpha)
            w_now = ffn_w[0]
            if nxt[0] < depth:
                h_lat, w_next = _fused_call("ffn", h_lat, md, w_now, (), cast=(ffn_f32, nxt), **args)
                ffn_w[0] = tuple(w_next)
            else:
                h_lat = _fused_call("ffn", h_lat, md, w_now, (), **args)
            if h_ctx is not None:
                h_ctx = _fused_call("ffn", h_ctx, mdc, w_now, (), **args)
            return h_lat, h_ctx

        h, hc_new = ffn_pair(h, hc if ctx_in else None, 0, 0)
        if ctx_in:
            hc = hc_new

        if kind == 0:
            conv = functools.partial(_fused_call, "conv", weights=conv_w, lead=(j,),
                                     ln_g=ln_g[i, 1], ln_b=ln_b[i, 1], k_shift=3, k_gate=5,
                                     res_scale=1.0, alpha=alpha)
            h = conv(h, md, period=GRID_W)
            if ctx_out:
                hc = conv(hc, mdc, period=ctx_len)
        elif kind == 1:
            lam_init = 0.8 - 0.6 * math.exp(-0.3 * i)
            qk, vt = _qkv_call(h, md, qkv_w, (j,), rope_tabs, k_shift=3, n_rope=2, seq=seq)
            kc, vct = _qkv_call(hc, mdc, qkv_w, (j,), no_rope_tabs, k_shift=3, n_rope=0,
                                seq=ctx_len, first_sec=1)
            o = _attn_call(qk, vt, kc, vct, da_lambda[j], da_subln[j], n_batch=n_batch, seq=seq,
                           ctx_len=ctx_len, heads=heads, lam_init=lam_init)
            h = _out_proj_call(o, h, md, attn_wo, (j,), ln_g[i, 1], ln_b[i, 1], k_gate=5,
                               alpha=alpha)
            assert not ctx_out, "context-side attention output is not implemented"
        else:
            def gmlp(hh, mm):
                vn = _gate_branch_call(hh, mm, gmlp_w[0], (j,), gm_ln_g[j], gm_ln_b[j], k_shift=3)
                return _fused_call("gmlp", hh, mm, gmlp_w, (j,), ln_g[i, 1], ln_b[i, 1],
                                   k_shift=3, k_gate=5, res_scale=1.0, alpha=alpha, vn=vn)

            h = gmlp(h, md)
            if ctx_out:
                hc = gmlp(hc, mdc)

        h, hc_new = ffn_pair(h, hc if ctx_out else None, 1, 6)
        if ctx_out:
            hc = hc_new
    return h.reshape(n_batch, seq, d)
```

```python
import functools
import math

import jax
import jax.numpy as jnp
from jax import lax
from jax.experimental import pallas as pl
from jax.experimental.pallas import tpu as pltpu

GRID_W = 64
CHUNK = 128
N_MOD = 9
N_MIXERS = 3
ROPE_BASE = 10000.0
LN_EPS = 1e-5

V7X_LANES = 128
V7X_VMEM_BYTES = 64 * 1024 * 1024
V7X_VMEM_CAP = V7X_VMEM_BYTES - 6 * 1024 * 1024

BF16 = jnp.bfloat16
F32 = jnp.float32

ATTN_TQ = 1024
ATTN_GROUP = 4
FUSED_CHUNK = 512
CAST_ROWS = 16
FUSED_TILE = {"ffn": 1024, "conv": 1024, "gmlp": 512}
FUSED_ROW_BLOCK = {"ffn": 512, "conv": 512, "gmlp": 256}
PROJ_ROW_BLOCK = 256


def _params(semantics, vmem_estimate):
    limit = min(V7X_VMEM_CAP, max(32 * 1024 * 1024, int(vmem_estimate * 1.3)))
    return pltpu.CompilerParams(dimension_semantics=semantics, vmem_limit_bytes=limit)


def _pick(n, candidates):
    for c in candidates:
        if n % c == 0:
            return c
    return n


def _mod_row(mods_ref, k):
    return mods_ref[0, k:k + 1, :]


def _modulate_bf16(h, mods_ref, k_shift):
    shift = _mod_row(mods_ref, k_shift)
    scale = _mod_row(mods_ref, k_shift + 1)
    return (h * (1.0 + scale) + shift).astype(BF16)


def _layer_norm(x, g, b):
    mu = jnp.mean(x, axis=-1, keepdims=True)
    xc = x - mu
    var = jnp.mean(xc * xc, axis=-1, keepdims=True)
    return xc * lax.rsqrt(var + LN_EPS) * g + b


def _silu(x):
    return x / (1.0 + jnp.exp(-x))


def _gelu(x):
    return 0.5 * x * (1.0 + lax.erf(x * math.sqrt(0.5)))


def _dot(a, b):
    return jnp.dot(a, b, preferred_element_type=F32)


def _ada_kernel(cond_ref, w_ref, b_ref, o_ref):
    a = _silu(cond_ref[...]).astype(BF16)
    o_ref[...] = _dot(a, w_ref[...].astype(BF16)) + b_ref[...]


def _ada(cond, w, b, layer):
    m, d = cond.shape
    n = w.shape[2]
    tn = _pick(n, (1024, 512, 256, 128))
    est = 2 * d * tn * 4 + d * tn * 2 + 4 * m * (d + tn) * 4
    return pl.pallas_call(
        _ada_kernel,
        grid=(n // tn,),
        in_specs=[pl.BlockSpec((m, d), lambda j: (0, 0)),
                  pl.BlockSpec((None, d, tn), lambda j: (layer, 0, j)),
                  pl.BlockSpec((None, 1, tn), lambda j: (layer, 0, j))],
        out_specs=pl.BlockSpec((m, tn), lambda j: (0, j)),
        out_shape=jax.ShapeDtypeStruct((m, n), F32),
        compiler_params=_params(("parallel",), est),
        name="ada_mod",
    )(cond, w, b.reshape(b.shape[0], 1, n))


def _fused_kernel(kind, k_shift, k_gate, res_scale, alpha, period, n_steps, n_cast, *refs):
    h_ref, mods_ref = refs[0], refs[1]
    n_in = len(refs) - 2 - n_cast
    w = refs[2:n_in - 2 - n_cast]
    lng_ref, lnb_ref = refs[n_in - 2 - n_cast:n_in - n_cast]
    cast_in = refs[n_in - n_cast:n_in]
    o_ref = refs[n_in]
    cast_out = refs[n_in + 1:n_in + 1 + n_cast]
    a_ref = refs[-1]
    acc_ref = o_ref
    wd_ref = w[-1]
    tm = h_ref.shape[0]


    def row_blocks(size):
        rb = min(tm, max(size, period))
        return [slice(r, r + rb) for r in range(0, tm, rb)]

    def up(a, rows):
        if kind == "ffn":
            return _dot(a, w[0][...]), _dot(a, w[1][...])
        if kind == "conv":
            return _dot(a, w[0][...]), _dot(a, w[1][...]), _dot(a, w[2][...])
        vn_ref, ws_ref, bs_ref = w[1], w[2], w[3]
        gw = w[0].shape[1] // ws_ref.shape[0]
        cols = []
        for g in range(ws_ref.shape[0]):
            parts = [_dot(ws_ref[g], vn_ref[c:c + CHUNK, g * gw:(g + 1) * gw]) + bs_ref[g]
                     for c in range(rows.start, rows.stop, CHUNK)]
            cols.append(jnp.concatenate(parts, axis=0))
        return _dot(a, w[0][...]), jnp.concatenate(cols, axis=1)

    def hidden(pre):
        if kind == "ffn":
            g, u = pre
            return _silu(g) * u
        if kind == "conv":
            b, c, v = pre
            z = c * v
            pos = lax.broadcasted_iota(jnp.int32, z.shape, 0) & (period - 1)
            z_prev = jnp.where(pos == 0, 0.0, pltpu.roll(z, 1, 0))
            z_next = jnp.where(pos == period - 1, 0.0, pltpu.roll(z, z.shape[0] - 1, 0))
            cw = w[3][...]
            return b * (cw[0:1, :] * z_prev + cw[1:2, :] * z + cw[2:3, :] * z_next)
        u, s = pre
        return _gelu(u) * s

    def step(first, last):
        for src, dst in zip(cast_in, cast_out):
            dst[...] = src[...].astype(dst.dtype)

        def start(rows):
            if first:
                a = _modulate_bf16(h_ref[rows, :], mods_ref, k_shift)
                if not last:
                    a_ref[rows, :] = a
            else:
                a = a_ref[rows, :]
            return up(a, rows)

        def finish(rows, pre):
            acc = _dot(hidden(pre).astype(BF16), wd_ref[...])
            if not first:
                acc = acc_ref[rows, :] + acc
            if last:
                y = alpha * h_ref[rows, :] + (res_scale * _mod_row(mods_ref, k_gate)) * acc
                o_ref[rows, :] = _layer_norm(y, lng_ref[...], lnb_ref[...])
            else:
                acc_ref[rows, :] = acc

        pending = None
        for rows in row_blocks(FUSED_ROW_BLOCK[kind]):
            pre = start(rows)
            if pending is not None:
                finish(*pending)
            pending = (rows, pre)
        finish(*pending)

    if n_steps == 1:
        step(True, True)
        return
    j = pl.program_id(1)
    pl.when(j == 0)(functools.partial(step, True, False))
    if n_steps > 2:
        pl.when(jnp.logical_and(j > 0, j < n_steps - 1))(functools.partial(step, False, False))
    pl.when(j == n_steps - 1)(functools.partial(step, False, True))


def _lead_spec(lead, block, index_fn):
    lead = tuple(lead)
    return pl.BlockSpec((None,) * len(lead) + tuple(block),
                        lambda *g: lead + tuple(index_fn(*g)))


def _fused_call(kind, h, mods, weights, lead, ln_g, ln_b, *, k_shift, k_gate, res_scale, alpha,
                period=GRID_W, vn=None, cast=None):
    n, d = h.shape
    nb = mods.shape[0]
    tm = _pick(n // nb, (FUSED_TILE[kind], 512, 256, 128))
    n_tiles = n // tm
    tiles_per_mod = n_tiles // nb

    row = lambda i, j: (i, 0)
    col_blk = lambda i, j: (0, j)
    row_blk = lambda i, j: (j, 0)
    common_in = [pl.BlockSpec((tm, d), row),
                 pl.BlockSpec((1, N_MOD, d), lambda i, j: (i // tiles_per_mod, 0, 0))]
    if kind == "ffn":
        wg, wu, wd = weights
        f = wg.shape[-1]
        tc = _pick(f, (FUSED_CHUNK, 256, 128))
        n_chunks = f // tc
        w_in = [_lead_spec(lead, (d, tc), col_blk), _lead_spec(lead, (d, tc), col_blk),
                _lead_spec(lead, (tc, d), row_blk)]
        w_args = [wg, wu, wd]
        w_bytes = 3 * d * tc * 2
    elif kind == "conv":
        w_in3, cw, wd = weights
        tc = _pick(d, (FUSED_CHUNK, 256, 128))
        n_chunks = d // tc
        w_in = [_lead_spec(lead, (d, tc), col_blk),
                _lead_spec(lead, (d, tc), lambda i, j: (0, n_chunks + j)),
                _lead_spec(lead, (d, tc), lambda i, j: (0, 2 * n_chunks + j)),
                _lead_spec(lead, (3, tc), col_blk),
                _lead_spec(lead, (tc, d), row_blk)]
        w_args = [w_in3, w_in3, w_in3, cw, wd]
        w_bytes = 4 * d * tc * 2
    else:
        w_in2, ws, bs, wd = weights
        groups = ws.shape[-3]
        gw = wd.shape[-2] // groups
        per_step = max(g for g in range(1, groups + 1) if groups % g == 0 and g * gw <= 1024)
        tc = per_step * gw
        n_chunks = groups // per_step
        w_in = [_lead_spec(lead, (d, tc), col_blk),
                pl.BlockSpec((tm, tc), lambda i, j: (i, j)),
                _lead_spec(lead, (per_step, CHUNK, CHUNK), lambda i, j: (j, 0, 0)),
                _lead_spec(lead, (per_step, CHUNK, 1), lambda i, j: (j, 0, 0)),
                _lead_spec(lead, (tc, d), row_blk)]
        w_args = [w_in2, vn, ws, bs, wd]
        w_bytes = 2 * d * tc * 2 + tm * tc * 2
    vec = pl.BlockSpec((1, d), lambda i, j: (0, 0))
    rb = min(tm, FUSED_ROW_BLOCK[kind])
    est = (2 * 2 * tm * d * 4
           + tm * d * 2
           + 2 * w_bytes
           + 6 * rb * tc * 4
           + 2 * rb * d * 4)
    cast_arrays, cast_lead = cast if cast is not None else ((), ())
    n_steps = n_tiles * n_chunks
    cast_in, cast_out, cast_shapes = [], [], []
    for arr in cast_arrays:
        rows, cols = arr.shape[-2:]
        if (cols % (n_chunks * V7X_LANES) == 0 and rows % (n_tiles * CAST_ROWS) == 0):
            blk, idx = (rows // n_tiles, cols // n_chunks), (lambda i, j: (i, j))
        else:
            assert rows % (n_steps * CAST_ROWS) == 0, "cast array does not split over the grid"
            blk, idx = (rows // n_steps, cols), (lambda i, j: (i * n_chunks + j, 0))
        cast_in.append(_lead_spec(cast_lead, blk, idx))
        cast_out.append(pl.BlockSpec(blk, idx))
        cast_shapes.append(jax.ShapeDtypeStruct((rows, cols), BF16))
        est += 2 * blk[0] * blk[1] * (4 + 2)
    kern = functools.partial(_fused_kernel, kind, k_shift, k_gate, res_scale, alpha, period,
                             n_chunks, len(cast_arrays))
    outs = pl.pallas_call(
        kern,
        grid=(n_tiles, n_chunks),
        in_specs=common_in + w_in + [vec, vec] + cast_in,
        out_specs=[pl.BlockSpec((tm, d), row)] + cast_out,
        out_shape=[jax.ShapeDtypeStruct((n, d), F32)] + cast_shapes,
        scratch_shapes=[pltpu.VMEM((tm, d), BF16)],
        compiler_params=_params(("parallel", "arbitrary"), est),
        name="fused_" + kind,
    )(h, mods, *w_args, ln_g.reshape(1, d), ln_b.reshape(1, d), *cast_arrays)
    return outs[0] if cast is None else (outs[0], outs[1:])


def _qkv_kernel(k_shift, n_rope, h_ref, mods_ref, w_ref, tab_ref, o_ref, vt_ref, a_ref):
    s = pl.program_id(1)
    n_sec = pl.num_programs(1)
    cos_ref, sin_ref = tab_ref.at[0], tab_ref.at[1]

    @pl.when(s == 0)
    def _():
        a_ref[...] = _modulate_bf16(h_ref[...], mods_ref, k_shift)

    def project(rope, transposed):
        width = w_ref.shape[1]
        slab = _pick(width, (512, 256, 128))
        for c in range(width // slab):
            y = _dot(a_ref[...], w_ref[:, c * slab:(c + 1) * slab])
            if transposed:
                vt_ref[c * slab:(c + 1) * slab, :] = y.T.astype(vt_ref.dtype)
                continue
            for r in range(slab // V7X_LANES):
                yr = y[:, r * V7X_LANES:(r + 1) * V7X_LANES]
                if rope:
                    yr = yr * cos_ref[...] + pltpu.roll(yr, V7X_LANES // 2, 1) * sin_ref[...]
                lo = c * slab + r * V7X_LANES
                o_ref[:, lo:lo + V7X_LANES] = yr.astype(o_ref.dtype)

    if n_rope > 0:
        @pl.when(s < n_rope)
        def _():
            project(True, False)

    @pl.when(jnp.logical_and(s >= n_rope, s < n_sec - 1))
    def _():
        project(False, False)

    @pl.when(s == n_sec - 1)
    def _():
        project(False, True)


def _qkv_call(h, mods, w, lead, tabs, *, k_shift, n_rope, seq, first_sec=0):
    n, d = h.shape
    n_tab = tabs.shape[0]
    nb = mods.shape[0]
    n_sec = w.shape[-1] // d - first_sec
    tm = _pick(min(n // nb, seq), (512, 256, 128))
    n_tiles = n // tm
    tiles_per_mod = n_tiles // nb
    tiles_per_seq = seq // tm
    est = (2 * tm * d * 4 + tm * d * 2 + 2 * d * d * 2 + 2 * 2 * tm * d * 2
           + 8 * tm * V7X_LANES * 4 + 4 * tm * 512 * 4)
    kern = functools.partial(_qkv_kernel, k_shift, n_rope)
    return pl.pallas_call(
        kern,
        grid=(n_tiles, n_sec),
        in_specs=[pl.BlockSpec((tm, d), lambda i, s: (i, 0)),
                  pl.BlockSpec((1, N_MOD, d), lambda i, s: (i // tiles_per_mod, 0, 0)),
                  _lead_spec(lead, (d, d), lambda i, s: (0, first_sec + s)),
                  pl.BlockSpec((None, 2, tm, V7X_LANES),
                               lambda i, s: (jnp.minimum(s, n_tab - 1), 0, i % tiles_per_seq, 0))],
        out_specs=[pl.BlockSpec((tm, d), lambda i, s: (i, jnp.minimum(s, n_sec - 2))),
                   pl.BlockSpec((None, d, tm), lambda i, s: (i, 0, 0))],
        out_shape=[jax.ShapeDtypeStruct((n, (n_sec - 1) * d), BF16),
                   jax.ShapeDtypeStruct((n_tiles, d, tm), BF16)],
        scratch_shapes=[pltpu.VMEM((tm, d), BF16)],
        compiler_params=_params(("parallel", "arbitrary"), est),
        name="qkv_proj",
    )(h, mods, w, tabs)


def _row_blocks(n_rows, block):
    block = min(n_rows, block)
    return [slice(r, r + block) for r in range(0, n_rows, block)]


def _gate_branch_kernel(k_shift, h_ref, mods_ref, w_ref, g_ref, b_ref, o_ref):
    for rows in _row_blocks(h_ref.shape[0], PROJ_ROW_BLOCK):
        a = _modulate_bf16(h_ref[rows, :], mods_ref, k_shift)
        v = _gelu(_dot(a, w_ref[...]))
        o_ref[rows, :] = _layer_norm(v, g_ref[...], b_ref[...]).astype(o_ref.dtype)


def _gate_branch_call(h, mods, w, lead, g, b, *, k_shift):
    n, d = h.shape
    nb = mods.shape[0]
    width = w.shape[-1] // 2
    tm = _pick(n // nb, (512, 256, 128))
    n_tiles = n // tm
    tiles_per_mod = n_tiles // nb
    est = 2 * tm * d * 4 + 2 * d * width * 2 + 2 * tm * width * 2 + 4 * tm * width * 4
    return pl.pallas_call(
        functools.partial(_gate_branch_kernel, k_shift),
        grid=(n_tiles,),
        in_specs=[pl.BlockSpec((tm, d), lambda i: (i, 0)),
                  pl.BlockSpec((1, N_MOD, d), lambda i: (i // tiles_per_mod, 0, 0)),
                  _lead_spec(lead, (d, width), lambda i: (0, 1)),
                  pl.BlockSpec((1, width), lambda i: (0, 0)),
                  pl.BlockSpec((1, width), lambda i: (0, 0))],
        out_specs=pl.BlockSpec((tm, width), lambda i: (i, 0)),
        out_shape=jax.ShapeDtypeStruct((n, width), BF16),
        compiler_params=_params(("parallel",), est),
        name="gmlp_gate_branch",
    )(h, mods, w, g.reshape(1, width), b.reshape(1, width))


def _attn_kernel(lam_init, lam_ref, q_ref, k_ref, vt_ref, kc_ref, vct_ref, subln_ref, o_ref,
                 acc_ref, s_ref):
    dk = q_ref.shape[1] // 2
    tq = q_ref.shape[0]
    tk = vt_ref.shape[2]
    q = q_ref[...]
    qm = (q[:, :dk], q[:, dk:])
    nt = (((1,), (1,)), ((), ()))

    def update(carry, blocks):
        for g, (k_blk, _) in enumerate(blocks):
            for mp in range(2):
                s_ref[g, mp, 0:k_blk.shape[0], :] = lax.dot_general(
                    k_blk[:, mp * dk:(mp + 1) * dk], qm[mp], nt, preferred_element_type=F32)
        carry = list(carry)
        for g, (k_blk, vt_blk) in enumerate(blocks):
            for mp in range(2):
                m_old, l_old = carry[2 * mp], carry[2 * mp + 1]
                st = s_ref[g, mp, 0:k_blk.shape[0], :]
                m_new = jnp.maximum(m_old, jnp.max(st, axis=0, keepdims=True))
                pt = jnp.exp2(st - m_new)
                corr = jnp.exp2(m_old - m_new)
                carry[2 * mp] = m_new
                carry[2 * mp + 1] = corr * l_old + jnp.sum(pt, axis=0, keepdims=True)
                acc_ref[mp] = corr * acc_ref[mp] + _dot(vt_blk, pt.astype(BF16))
        return tuple(carry)

    acc_ref[...] = jnp.zeros_like(acc_ref)
    neg = jnp.full((1, tq), -jnp.inf, F32)
    zero = jnp.zeros((1, tq), F32)
    group = s_ref.shape[0]

    def body(c, carry):
        blocks = []
        for g in range(group):
            start = pl.multiple_of((c * group + g) * tk, tk)
            blocks.append((k_ref[pl.ds(start, tk), :], vt_ref[c * group + g]))
        return update(carry, blocks)

    carry = lax.fori_loop(0, vt_ref.shape[0] // group, body, (neg, zero, neg, zero))
    tc = vct_ref.shape[2]
    carry = update(carry, [(kc_ref[c * tc:(c + 1) * tc, :], vct_ref[c])
                           for c in range(vct_ref.shape[0])])
    _, l0, _, l1 = carry

    lp = lam_ref[...]
    lam = (jnp.exp(jnp.sum(lp[0:1, :] * lp[1:2, :], axis=-1, keepdims=True))
           - jnp.exp(jnp.sum(lp[2:3, :] * lp[3:4, :], axis=-1, keepdims=True)) + lam_init)
    ot = acc_ref[0] / l0 - lam * (acc_ref[1] / l1)
    ot = ot * lax.rsqrt(jnp.mean(ot * ot, axis=0, keepdims=True) + LN_EPS) * subln_ref[...]
    o_ref[...] = (ot * (1.0 - lam_init)).T.astype(o_ref.dtype)


def _attn_call(qk, vt, kc, vct, lam_p, subln, *, n_batch, seq, ctx_len, heads, lam_init):
    d = qk.shape[1] // 2
    dv = d // heads
    tk = vt.shape[2]
    tc = vct.shape[2]
    nk = seq // tk
    nc = ctx_len // tc
    group = _pick(nk, (ATTN_GROUP, 1))
    assert nc <= group and tc <= tk, "context keys must fit one score-scratch group"
    tq = _pick(seq, (ATTN_TQ, 128))
    nq = seq // tq
    est = (2 * 2 * tq * dv * 2 + 2 * 2 * seq * dv * 2 + 2 * 2 * ctx_len * dv * 2
           + 2 * tq * dv * 4 + (2 * group + 6) * tq * tk * 4)
    kern = functools.partial(_attn_kernel, lam_init)
    return pl.pallas_call(
        kern,
        grid=(n_batch, heads, nq),
        in_specs=[pl.BlockSpec(lam_p.shape, lambda b, h, i: (0, 0)),
                  pl.BlockSpec((tq, dv), lambda b, h, i: (b * nq + i, h)),
                  pl.BlockSpec((seq, dv), lambda b, h, i: (b, heads + h)),
                  pl.BlockSpec((nk, dv, tk), lambda b, h, i: (b, h, 0)),
                  pl.BlockSpec((ctx_len, dv), lambda b, h, i: (b, h)),
                  pl.BlockSpec((nc, dv, tc), lambda b, h, i: (b, h, 0)),
                  pl.BlockSpec((dv, 1), lambda b, h, i: (0, 0))],
        out_specs=pl.BlockSpec((tq, dv), lambda b, h, i: (b * nq + i, h)),
        out_shape=jax.ShapeDtypeStruct((n_batch * seq, d), BF16),
        scratch_shapes=[pltpu.VMEM((2, dv, tq), F32), pltpu.VMEM((group, 2, tk, tq), F32)],
        compiler_params=_params(("parallel", "parallel", "arbitrary"), est),
        name="diff_attention",
    )(lam_p, qk, qk, vt, kc, vct, subln.reshape(dv, 1))


def _out_proj_kernel(k_gate, alpha, y_ref, h_ref, mods_ref, w_ref, g_ref, b_ref, o_ref):
    for rows in _row_blocks(h_ref.shape[0], PROJ_ROW_BLOCK):
        y = _dot(y_ref[rows, :], w_ref[...])
        x = alpha * h_ref[rows, :] + _mod_row(mods_ref, k_gate) * y
        o_ref[rows, :] = _layer_norm(x, g_ref[...], b_ref[...])


def _out_proj_call(y, h, mods, w, lead, g, b, *, k_gate, alpha):
    n, d = h.shape
    nb = mods.shape[0]
    tm = _pick(n // nb, (512, 256, 128))
    n_tiles = n // tm
    tiles_per_mod = n_tiles // nb
    est = 2 * tm * d * 2 + 2 * 2 * tm * d * 4 + 2 * d * d * 2 + 4 * tm * d * 4
    return pl.pallas_call(
        functools.partial(_out_proj_kernel, k_gate, alpha),
        grid=(n_tiles,),
        in_specs=[pl.BlockSpec((tm, d), lambda i: (i, 0)),
                  pl.BlockSpec((tm, d), lambda i: (i, 0)),
                  pl.BlockSpec((1, N_MOD, d), lambda i: (i // tiles_per_mod, 0, 0)),
                  _lead_spec(lead, (d, d), lambda i: (0, 0)),
                  pl.BlockSpec((1, d), lambda i: (0, 0)),
                  pl.BlockSpec((1, d), lambda i: (0, 0))],
        out_specs=pl.BlockSpec((tm, d), lambda i: (i, 0)),
        out_shape=jax.ShapeDtypeStruct((n, d), F32),
        compiler_params=_params(("parallel",), est),
        name="out_proj_norm",
    )(y, h, mods, w, g.reshape(1, d), b.reshape(1, d))


def _rope_tables(seq, dk):
    n_freq = dk // 4
    t = jnp.arange(seq)
    inv = ROPE_BASE ** (-jnp.arange(n_freq, dtype=F32) / n_freq)
    ang_r = (t // GRID_W).astype(F32)[:, None] * inv
    ang_c = (t % GRID_W).astype(F32)[:, None] * inv
    cos_t = jnp.concatenate([jnp.cos(ang_r), jnp.cos(ang_c)] * 2, axis=-1)
    sin_t = jnp.concatenate([-jnp.sin(ang_r), -jnp.sin(ang_c), jnp.sin(ang_r), jnp.sin(ang_c)], axis=-1)
    k_tab = jnp.stack([cos_t, sin_t])
    return jnp.stack([k_tab * (dk ** -0.5 * math.log2(math.e)), k_tab])


def _rope_column_layout(w, dk):
    rows, width = w.shape
    w = w.reshape(rows, width // dk, 2, 2, dk // 4)
    return jnp.swapaxes(w, 2, 3).reshape(rows, width)


def kernel(x, c, ctx, c_ctx, ada_w, ada_b, ln_g, ln_b, ffn_wg, ffn_wu, ffn_wd, sc_w_in, sc_conv,
           sc_w_out, da_w_qkv, da_lambda, da_subln, da_w_o, gm_w_in, gm_ln_g, gm_ln_b, gm_w_s,
           gm_b_s, gm_w_out):
    n_batch, seq, d = x.shape
    ctx_len = ctx.shape[1]
    depth = ada_w.shape[0]
    mixer_of_layer = tuple(i % N_MIXERS for i in range(depth))
    last_ctx_layer = max([i for i in range(depth) if mixer_of_layer[i] == 1], default=-1)
    alpha = (2.0 * depth) ** 0.25
    dv = da_subln.shape[-1]
    heads = d // dv
    dk = dv // 2

    h = x.reshape(n_batch * seq, d)
    hc = ctx.reshape(n_batch * ctx_len, d)

    n_cond = n_batch + 1
    cond = jnp.zeros((16 * ((n_cond + 15) // 16), d), F32)
    cond = cond.at[:n_batch].set(c).at[n_batch].set(c_ctx)

    rope_tabs = _rope_tables(seq, dk)
    no_rope_tabs = jnp.zeros((1, 2, ctx_len, dk), F32)

    ffn_f32 = (ffn_wg, ffn_wu, ffn_wd)
    ffn_w = [tuple(w[0, 0].astype(BF16) for w in ffn_f32)]
    conv_w = (sc_w_in.astype(BF16), sc_conv, sc_w_out.astype(BF16))
    qkv_w = da_w_qkv.astype(BF16)
    qkv_w = jnp.concatenate(
        [_rope_column_layout(qkv_w[..., :2 * d].reshape(-1, 2 * d), dk).reshape(qkv_w.shape[0], d, 2 * d),
         qkv_w[..., 2 * d:]], axis=-1)
    attn_wo = da_w_o.astype(BF16)
    gmlp_w = (gm_w_in.astype(BF16), gm_w_s.astype(BF16), gm_b_s[..., None], gm_w_out.astype(BF16))

    for i in range(depth):
        kind = mixer_of_layer[i]
        j = i // N_MIXERS
        ctx_in = i <= last_ctx_layer
        ctx_out = i < last_ctx_layer
        mods_all = _ada(cond, ada_w, ada_b, i).reshape(-1, N_MOD, d)
        md = mods_all[:n_batch]
        mdc = mods_all[n_batch:n_batch + 1]

        def ffn_pair(h_lat, h_ctx, half, k0):
            nxt = (i, 1) if half == 0 else (i + 1, 0)
            args = dict(ln_g=ln_g[i, 2 * half], ln_b=ln_b[i, 2 * half], k_shift=k0,
                        k_gate=k0 + 2, res_scale=0.5, alpha=alpha)
            w_now = ffn_w[0]
            if nxt[0] < depth:
                h_lat, w_next = _fused_call("ffn", h_lat, md, w_now, (), cast=(ffn_f32, nxt), **args)
                ffn_w[0] = tuple(w_next)
            else:
                h_lat = _fused_call("ffn", h_lat, md, w_now, (), **args)
            if h_ctx is not None:
                h_ctx = _fused_call("ffn", h_ctx, mdc, w_now, (), **args)
            return h_lat, h_ctx

        h, hc_new = ffn_pair(h, hc if ctx_in else None, 0, 0)
        if ctx_in:
            hc = hc_new

        if kind == 0:
            conv = functools.partial(_fused_call, "conv", weights=conv_w, lead=(j,),
                                     ln_g=ln_g[i, 1], ln_b=ln_b[i, 1], k_shift=3, k_gate=5,
                                     res_scale=1.0, alpha=alpha)
            h = conv(h, md, period=GRID_W)
            if ctx_out:
                hc = conv(hc, mdc, period=ctx_len)
        elif kind == 1:
            lam_init = 0.8 - 0.6 * math.exp(-0.3 * i)
            qk, vt = _qkv_call(h, md, qkv_w, (j,), rope_tabs, k_shift=3, n_rope=2, seq=seq)
            kc, vct = _qkv_call(hc, mdc, qkv_w, (j,), no_rope_tabs, k_shift=3, n_rope=0,
                                seq=ctx_len, first_sec=1)
            o = _attn_call(qk, vt, kc, vct, da_lambda[j], da_subln[j], n_batch=n_batch, seq=seq,
                           ctx_len=ctx_len, heads=heads, lam_init=lam_init)
            h = _out_proj_call(o, h, md, attn_wo, (j,), ln_g[i, 1], ln_b[i, 1], k_gate=5,
                               alpha=alpha)
            assert not ctx_out, "context-side attention output is not implemented"
        else:
            def gmlp(hh, mm):
                vn = _gate_branch_call(hh, mm, gmlp_w[0], (j,), gm_ln_g[j], gm_ln_b[j], k_shift=3)
                return _fused_call("gmlp", hh, mm, gmlp_w, (j,), ln_g[i, 1], ln_b[i, 1],
                                   k_shift=3, k_gate=5, res_scale=1.0, alpha=alpha, vn=vn)

            h = gmlp(h, md)
            if ctx_out:
                hc = gmlp(hc, mdc)

        h, hc_new = ffn_pair(h, hc if ctx_out else None, 1, 6)
        if ctx_out:
            hc = hc_new
    return h.reshape(n_batch, seq, d)
```

```python
import functools
import math

import jax
import jax.numpy as jnp
from jax import lax
from jax.experimental import pallas as pl
from jax.experimental.pallas import tpu as pltpu

GRID_W = 64
CHUNK = 128
N_MOD = 9
N_MIXERS = 3
ROPE_BASE = 10000.0
LN_EPS = 1e-5

V7X_LANES = 128
V7X_VMEM_BYTES = 64 * 1024 * 1024
V7X_VMEM_CAP = V7X_VMEM_BYTES - 6 * 1024 * 1024

BF16 = jnp.bfloat16
F32 = jnp.float32

ATTN_TQ = 1024
ATTN_GROUP = 4
FUSED_CHUNK = 512
CAST_ROWS = 16
FUSED_TILE = {"ffn": 1024, "conv": 1024, "gmlp": 512}
FUSED_ROW_BLOCK = {"ffn": 1024, "conv": 512, "gmlp": 256}
PROJ_ROW_BLOCK = 256


def _params(semantics, vmem_estimate):
    limit = min(V7X_VMEM_CAP, max(32 * 1024 * 1024, int(vmem_estimate * 1.3)))
    return pltpu.CompilerParams(dimension_semantics=semantics, vmem_limit_bytes=limit)


def _pick(n, candidates):
    for c in candidates:
        if n % c == 0:
            return c
    return n


def _mod_row(mods_ref, k):
    return mods_ref[0, k:k + 1, :]


def _modulate_bf16(h, mods_ref, k_shift):
    shift = _mod_row(mods_ref, k_shift)
    scale = _mod_row(mods_ref, k_shift + 1)
    return (h * (1.0 + scale) + shift).astype(BF16)


def _layer_norm(x, g, b):
    mu = jnp.mean(x, axis=-1, keepdims=True)
    xc = x - mu
    var = jnp.mean(xc * xc, axis=-1, keepdims=True)
    return xc * lax.rsqrt(var + LN_EPS) * g + b


def _silu(x):
    return x / (1.0 + jnp.exp(-x))


def _gelu(x):
    return 0.5 * x * (1.0 + lax.erf(x * math.sqrt(0.5)))


def _dot(a, b):
    return jnp.dot(a, b, preferred_element_type=F32)


def _ada_kernel(cond_ref, w_ref, b_ref, o_ref):
    a = _silu(cond_ref[...]).astype(BF16)
    o_ref[...] = _dot(a, w_ref[...].astype(BF16)) + b_ref[...]


def _ada(cond, w, b, layer):
    m, d = cond.shape
    n = w.shape[2]
    tn = _pick(n, (1024, 512, 256, 128))
    est = 2 * d * tn * 4 + d * tn * 2 + 4 * m * (d + tn) * 4
    return pl.pallas_call(
        _ada_kernel,
        grid=(n // tn,),
        in_specs=[pl.BlockSpec((m, d), lambda j: (0, 0)),
                  pl.BlockSpec((None, d, tn), lambda j: (layer, 0, j)),
                  pl.BlockSpec((None, 1, tn), lambda j: (layer, 0, j))],
        out_specs=pl.BlockSpec((m, tn), lambda j: (0, j)),
        out_shape=jax.ShapeDtypeStruct((m, n), F32),
        compiler_params=_params(("parallel",), est),
        name="ada_mod",
    )(cond, w, b.reshape(b.shape[0], 1, n))


def _fused_kernel(kind, k_shift, k_gate, res_scale, alpha, period, n_steps, n_cast, *refs):
    h_ref, mods_ref = refs[0], refs[1]
    n_in = len(refs) - 2 - n_cast
    w = refs[2:n_in - 2 - n_cast]
    lng_ref, lnb_ref = refs[n_in - 2 - n_cast:n_in - n_cast]
    cast_in = refs[n_in - n_cast:n_in]
    o_ref = refs[n_in]
    cast_out = refs[n_in + 1:n_in + 1 + n_cast]
    a_ref = refs[-1]
    acc_ref = o_ref
    wd_ref = w[-1]
    tm = h_ref.shape[0]


    def row_blocks(size):
        rb = min(tm, max(size, period))
        return [slice(r, r + rb) for r in range(0, tm, rb)]

    def up(a, rows):
        if kind == "ffn":
            return _dot(a, w[0][...]), _dot(a, w[1][...])
        if kind == "conv":
            return _dot(a, w[0][...]), _dot(a, w[1][...]), _dot(a, w[2][...])
        vn_ref, ws_ref, bs_ref = w[1], w[2], w[3]
        gw = w[0].shape[1] // ws_ref.shape[0]
        cols = []
        for g in range(ws_ref.shape[0]):
            parts = [_dot(ws_ref[g], vn_ref[c:c + CHUNK, g * gw:(g + 1) * gw]) + bs_ref[g]
                     for c in range(rows.start, rows.stop, CHUNK)]
            cols.append(jnp.concatenate(parts, axis=0))
        return _dot(a, w[0][...]), jnp.concatenate(cols, axis=1)

    def hidden(pre):
        if kind == "ffn":
            g, u = pre
            return _silu(g) * u
        if kind == "conv":
            b, c, v = pre
            z = c * v
            pos = lax.broadcasted_iota(jnp.int32, z.shape, 0) & (period - 1)
            z_prev = jnp.where(pos == 0, 0.0, pltpu.roll(z, 1, 0))
            z_next = jnp.where(pos == period - 1, 0.0, pltpu.roll(z, z.shape[0] - 1, 0))
            cw = w[3][...]
            return b * (cw[0:1, :] * z_prev + cw[1:2, :] * z + cw[2:3, :] * z_next)
        u, s = pre
        return _gelu(u) * s

    def step(first, last):
        for src, dst in zip(cast_in, cast_out):
            dst[...] = src[...].astype(dst.dtype)

        def start(rows):
            if first:
                a = _modulate_bf16(h_ref[rows, :], mods_ref, k_shift)
                if not last:
                    a_ref[rows, :] = a
            else:
                a = a_ref[rows, :]
            return up(a, rows)

        def finish(rows, pre):
            acc = _dot(hidden(pre).astype(BF16), wd_ref[...])
            if not first:
                acc = acc_ref[rows, :] + acc
            if last:
                y = alpha * h_ref[rows, :] + (res_scale * _mod_row(mods_ref, k_gate)) * acc
                o_ref[rows, :] = _layer_norm(y, lng_ref[...], lnb_ref[...])
            else:
                acc_ref[rows, :] = acc

        pending = None
        for rows in row_blocks(FUSED_ROW_BLOCK[kind]):
            pre = start(rows)
            if pending is not None:
                finish(*pending)
            pending = (rows, pre)
        finish(*pending)

    if n_steps == 1:
        step(True, True)
        return
    j = pl.program_id(1)
    pl.when(j == 0)(functools.partial(step, True, False))
    if n_steps > 2:
        pl.when(jnp.logical_and(j > 0, j < n_steps - 1))(functools.partial(step, False, False))
    pl.when(j == n_steps - 1)(functools.partial(step, False, True))


def _lead_spec(lead, block, index_fn):
    lead = tuple(lead)
    return pl.BlockSpec((None,) * len(lead) + tuple(block),
                        lambda *g: lead + tuple(index_fn(*g)))


def _fused_call(kind, h, mods, weights, lead, ln_g, ln_b, *, k_shift, k_gate, res_scale, alpha,
                period=GRID_W, vn=None, cast=None):
    n, d = h.shape
    nb = mods.shape[0]
    tm = _pick(n // nb, (FUSED_TILE[kind], 512, 256, 128))
    n_tiles = n // tm
    tiles_per_mod = n_tiles // nb

    row = lambda i, j: (i, 0)
    col_blk = lambda i, j: (0, j)
    row_blk = lambda i, j: (j, 0)
    common_in = [pl.BlockSpec((tm, d), row),
                 pl.BlockSpec((1, N_MOD, d), lambda i, j: (i // tiles_per_mod, 0, 0))]
    if kind == "ffn":
        wg, wu, wd = weights
        f = wg.shape[-1]
        tc = _pick(f, (FUSED_CHUNK, 256, 128))
        n_chunks = f // tc
        w_in = [_lead_spec(lead, (d, tc), col_blk), _lead_spec(lead, (d, tc), col_blk),
                _lead_spec(lead, (tc, d), row_blk)]
        w_args = [wg, wu, wd]
        w_bytes = 3 * d * tc * 2
    elif kind == "conv":
        w_in3, cw, wd = weights
        tc = _pick(d, (FUSED_CHUNK, 256, 128))
        n_chunks = d // tc
        w_in = [_lead_spec(lead, (d, tc), col_blk),
                _lead_spec(lead, (d, tc), lambda i, j: (0, n_chunks + j)),
                _lead_spec(lead, (d, tc), lambda i, j: (0, 2 * n_chunks + j)),
                _lead_spec(lead, (3, tc), col_blk),
                _lead_spec(lead, (tc, d), row_blk)]
        w_args = [w_in3, w_in3, w_in3, cw, wd]
        w_bytes = 4 * d * tc * 2
    else:
        w_in2, ws, bs, wd = weights
        groups = ws.shape[-3]
        gw = wd.shape[-2] // groups
        per_step = max(g for g in range(1, groups + 1) if groups % g == 0 and g * gw <= 1024)
        tc = per_step * gw
        n_chunks = groups // per_step
        w_in = [_lead_spec(lead, (d, tc), col_blk),
                pl.BlockSpec((tm, tc), lambda i, j: (i, j)),
                _lead_spec(lead, (per_step, CHUNK, CHUNK), lambda i, j: (j, 0, 0)),
                _lead_spec(lead, (per_step, CHUNK, 1), lambda i, j: (j, 0, 0)),
                _lead_spec(lead, (tc, d), row_blk)]
        w_args = [w_in2, vn, ws, bs, wd]
        w_bytes = 2 * d * tc * 2 + tm * tc * 2
    vec = pl.BlockSpec((1, d), lambda i, j: (0, 0))
    rb = min(tm, FUSED_ROW_BLOCK[kind])
    est = (2 * 2 * tm * d * 4
           + tm * d * 2
           + 2 * w_bytes
           + 6 * rb * tc * 4
           + 2 * rb * d * 4)
    cast_arrays, cast_lead = cast if cast is not None else ((), ())
    n_steps = n_tiles * n_chunks
    cast_in, cast_out, cast_shapes = [], [], []
    for arr in cast_arrays:
        rows, cols = arr.shape[-2:]
        if (cols % (n_chunks * V7X_LANES) == 0 and rows % (n_tiles * CAST_ROWS) == 0):
            blk, idx = (rows // n_tiles, cols // n_chunks), (lambda i, j: (i, j))
        else:
            assert rows % (n_steps * CAST_ROWS) == 0, "cast array does not split over the grid"
            blk, idx = (rows // n_steps, cols), (lambda i, j: (i * n_chunks + j, 0))
        cast_in.append(_lead_spec(cast_lead, blk, idx))
        cast_out.append(pl.BlockSpec(blk, idx))
        cast_shapes.append(jax.ShapeDtypeStruct((rows, cols), BF16))
        est += 2 * blk[0] * blk[1] * (4 + 2)
    kern = functools.partial(_fused_kernel, kind, k_shift, k_gate, res_scale, alpha, period,
                             n_chunks, len(cast_arrays))
    outs = pl.pallas_call(
        kern,
        grid=(n_tiles, n_chunks),
        in_specs=common_in + w_in + [vec, vec] + cast_in,
        out_specs=[pl.BlockSpec((tm, d), row)] + cast_out,
        out_shape=[jax.ShapeDtypeStruct((n, d), F32)] + cast_shapes,
        scratch_shapes=[pltpu.VMEM((tm, d), BF16)],
        compiler_params=_params(("parallel", "arbitrary"), est),
        name="fused_" + kind,
    )(h, mods, *w_args, ln_g.reshape(1, d), ln_b.reshape(1, d), *cast_arrays)
    return outs[0] if cast is None else (outs[0], outs[1:])


def _qkv_kernel(k_shift, n_rope, h_ref, mods_ref, w_ref, tab_ref, o_ref, vt_ref, a_ref):
    s = pl.program_id(1)
    n_sec = pl.num_programs(1)
    cos_ref, sin_ref = tab_ref.at[0], tab_ref.at[1]

    @pl.when(s == 0)
    def _():
        a_ref[...] = _modulate_bf16(h_ref[...], mods_ref, k_shift)

    def project(rope, transposed):
        width = w_ref.shape[1]
        slab = _pick(width, (512, 256, 128))
        for c in range(width // slab):
            y = _dot(a_ref[...], w_ref[:, c * slab:(c + 1) * slab])
            if transposed:
                vt_ref[c * slab:(c + 1) * slab, :] = y.T.astype(vt_ref.dtype)
                continue
            for r in range(slab // V7X_LANES):
                yr = y[:, r * V7X_LANES:(r + 1) * V7X_LANES]
                if rope:
                    yr = yr * cos_ref[...] + pltpu.roll(yr, V7X_LANES // 2, 1) * sin_ref[...]
                lo = c * slab + r * V7X_LANES
                o_ref[:, lo:lo + V7X_LANES] = yr.astype(o_ref.dtype)

    if n_rope > 0:
        @pl.when(s < n_rope)
        def _():
            project(True, False)

    @pl.when(jnp.logical_and(s >= n_rope, s < n_sec - 1))
    def _():
        project(False, False)

    @pl.when(s == n_sec - 1)
    def _():
        project(False, True)


def _qkv_call(h, mods, w, lead, tabs, *, k_shift, n_rope, seq, first_sec=0):
    n, d = h.shape
    n_tab = tabs.shape[0]
    nb = mods.shape[0]
    n_sec = w.shape[-1] // d - first_sec
    tm = _pick(min(n // nb, seq), (512, 256, 128))
    n_tiles = n // tm
    tiles_per_mod = n_tiles // nb
    tiles_per_seq = seq // tm
    est = (2 * tm * d * 4 + tm * d * 2 + 2 * d * d * 2 + 2 * 2 * tm * d * 2
           + 8 * tm * V7X_LANES * 4 + 4 * tm * 512 * 4)
    kern = functools.partial(_qkv_kernel, k_shift, n_rope)
    return pl.pallas_call(
        kern,
        grid=(n_tiles, n_sec),
        in_specs=[pl.BlockSpec((tm, d), lambda i, s: (i, 0)),
                  pl.BlockSpec((1, N_MOD, d), lambda i, s: (i // tiles_per_mod, 0, 0)),
                  _lead_spec(lead, (d, d), lambda i, s: (0, first_sec + s)),
                  pl.BlockSpec((None, 2, tm, V7X_LANES),
                               lambda i, s: (jnp.minimum(s, n_tab - 1), 0, i % tiles_per_seq, 0))],
        out_specs=[pl.BlockSpec((tm, d), lambda i, s: (i, jnp.minimum(s, n_sec - 2))),
                   pl.BlockSpec((None, d, tm), lambda i, s: (i, 0, 0))],
        out_shape=[jax.ShapeDtypeStruct((n, (n_sec - 1) * d), BF16),
                   jax.ShapeDtypeStruct((n_tiles, d, tm), BF16)],
        scratch_shapes=[pltpu.VMEM((tm, d), BF16)],
        compiler_params=_params(("parallel", "arbitrary"), est),
        name="qkv_proj",
    )(h, mods, w, tabs)


def _row_blocks(n_rows, block):
    block = min(n_rows, block)
    return [slice(r, r + block) for r in range(0, n_rows, block)]


def _gate_branch_kernel(k_shift, h_ref, mods_ref, w_ref, g_ref, b_ref, o_ref):
    for rows in _row_blocks(h_ref.shape[0], PROJ_ROW_BLOCK):
        a = _modulate_bf16(h_ref[rows, :], mods_ref, k_shift)
        v = _gelu(_dot(a, w_ref[...]))
        o_ref[rows, :] = _layer_norm(v, g_ref[...], b_ref[...]).astype(o_ref.dtype)


def _gate_branch_call(h, mods, w, lead, g, b, *, k_shift):
    n, d = h.shape
    nb = mods.shape[0]
    width = w.shape[-1] // 2
    tm = _pick(n // nb, (512, 256, 128))
    n_tiles = n // tm
    tiles_per_mod = n_tiles // nb
    est = 2 * tm * d * 4 + 2 * d * width * 2 + 2 * tm * width * 2 + 4 * tm * width * 4
    return pl.pallas_call(
        functools.partial(_gate_branch_kernel, k_shift),
        grid=(n_tiles,),
        in_specs=[pl.BlockSpec((tm, d), lambda i: (i, 0)),
                  pl.BlockSpec((1, N_MOD, d), lambda i: (i // tiles_per_mod, 0, 0)),
                  _lead_spec(lead, (d, width), lambda i: (0, 1)),
                  pl.BlockSpec((1, width), lambda i: (0, 0)),
                  pl.BlockSpec((1, width), lambda i: (0, 0))],
        out_specs=pl.BlockSpec((tm, width), lambda i: (i, 0)),
        out_shape=jax.ShapeDtypeStruct((n, width), BF16),
        compiler_params=_params(("parallel",), est),
        name="gmlp_gate_branch",
    )(h, mods, w, g.reshape(1, width), b.reshape(1, width))


def _attn_kernel(lam_init, lam_ref, q_ref, k_ref, vt_ref, kc_ref, vct_ref, subln_ref, o_ref,
                 acc_ref, s_ref):
    dk = q_ref.shape[1] // 2
    tq = q_ref.shape[0]
    tk = vt_ref.shape[2]
    q = q_ref[...]
    qm = (q[:, :dk], q[:, dk:])
    nt = (((1,), (1,)), ((), ()))

    def update(carry, blocks):
        for g, (k_blk, _) in enumerate(blocks):
            for mp in range(2):
                s_ref[g, mp, 0:k_blk.shape[0], :] = lax.dot_general(
                    k_blk[:, mp * dk:(mp + 1) * dk], qm[mp], nt, preferred_element_type=F32)
        carry = list(carry)
        for g, (k_blk, vt_blk) in enumerate(blocks):
            for mp in range(2):
                m_old, l_old = carry[2 * mp], carry[2 * mp + 1]
                st = s_ref[g, mp, 0:k_blk.shape[0], :]
                m_new = jnp.maximum(m_old, jnp.max(st, axis=0, keepdims=True))
                pt = jnp.exp2(st - m_new)
                corr = jnp.exp2(m_old - m_new)
                carry[2 * mp] = m_new
                carry[2 * mp + 1] = corr * l_old + jnp.sum(pt, axis=0, keepdims=True)
                acc_ref[mp] = corr * acc_ref[mp] + _dot(vt_blk, pt.astype(BF16))
        return tuple(carry)

    acc_ref[...] = jnp.zeros_like(acc_ref)
    neg = jnp.full((1, tq), -jnp.inf, F32)
    zero = jnp.zeros((1, tq), F32)
    group = s_ref.shape[0]

    def body(c, carry):
        blocks = []
        for g in range(group):
            start = pl.multiple_of((c * group + g) * tk, tk)
            blocks.append((k_ref[pl.ds(start, tk), :], vt_ref[c * group + g]))
        return update(carry, blocks)

    carry = lax.fori_loop(0, vt_ref.shape[0] // group, body, (neg, zero, neg, zero))
    tc = vct_ref.shape[2]
    carry = update(carry, [(kc_ref[c * tc:(c + 1) * tc, :], vct_ref[c])
                           for c in range(vct_ref.shape[0])])
    _, l0, _, l1 = carry

    lp = lam_ref[...]
    lam = (jnp.exp(jnp.sum(lp[0:1, :] * lp[1:2, :], axis=-1, keepdims=True))
           - jnp.exp(jnp.sum(lp[2:3, :] * lp[3:4, :], axis=-1, keepdims=True)) + lam_init)
    ot = acc_ref[0] / l0 - lam * (acc_ref[1] / l1)
    ot = ot * lax.rsqrt(jnp.mean(ot * ot, axis=0, keepdims=True) + LN_EPS) * subln_ref[...]
    o_ref[...] = (ot * (1.0 - lam_init)).T.astype(o_ref.dtype)


def _attn_call(qk, vt, kc, vct, lam_p, subln, *, n_batch, seq, ctx_len, heads, lam_init):
    d = qk.shape[1] // 2
    dv = d // heads
    tk = vt.shape[2]
    tc = vct.shape[2]
    nk = seq // tk
    nc = ctx_len // tc
    group = _pick(nk, (ATTN_GROUP, 1))
    assert nc <= group and tc <= tk, "context keys must fit one score-scratch group"
    tq = _pick(seq, (ATTN_TQ, 128))
    nq = seq // tq
    est = (2 * 2 * tq * dv * 2 + 2 * 2 * seq * dv * 2 + 2 * 2 * ctx_len * dv * 2
           + 2 * tq * dv * 4 + (2 * group + 6) * tq * tk * 4)
    kern = functools.partial(_attn_kernel, lam_init)
    return pl.pallas_call(
        kern,
        grid=(n_batch, heads, nq),
        in_specs=[pl.BlockSpec(lam_p.shape, lambda b, h, i: (0, 0)),
                  pl.BlockSpec((tq, dv), lambda b, h, i: (b * nq + i, h)),
                  pl.BlockSpec((seq, dv), lambda b, h, i: (b, heads + h)),
                  pl.BlockSpec((nk, dv, tk), lambda b, h, i: (b, h, 0)),
                  pl.BlockSpec((ctx_len, dv), lambda b, h, i: (b, h)),
                  pl.BlockSpec((nc, dv, tc), lambda b, h, i: (b, h, 0)),
                  pl.BlockSpec((dv, 1), lambda b, h, i: (0, 0))],
        out_specs=pl.BlockSpec((tq, dv), lambda b, h, i: (b * nq + i, h)),
        out_shape=jax.ShapeDtypeStruct((n_batch * seq, d), BF16),
        scratch_shapes=[pltpu.VMEM((2, dv, tq), F32), pltpu.VMEM((group, 2, tk, tq), F32)],
        compiler_params=_params(("parallel", "parallel", "arbitrary"), est),
        name="diff_attention",
    )(lam_p, qk, qk, vt, kc, vct, subln.reshape(dv, 1))


def _out_proj_kernel(k_gate, alpha, y_ref, h_ref, mods_ref, w_ref, g_ref, b_ref, o_ref):
    for rows in _row_blocks(h_ref.shape[0], PROJ_ROW_BLOCK):
        y = _dot(y_ref[rows, :], w_ref[...])
        x = alpha * h_ref[rows, :] + _mod_row(mods_ref, k_gate) * y
        o_ref[rows, :] = _layer_norm(x, g_ref[...], b_ref[...])


def _out_proj_call(y, h, mods, w, lead, g, b, *, k_gate, alpha):
    n, d = h.shape
    nb = mods.shape[0]
    tm = _pick(n // nb, (512, 256, 128))
    n_tiles = n // tm
    tiles_per_mod = n_tiles // nb
    est = 2 * tm * d * 2 + 2 * 2 * tm * d * 4 + 2 * d * d * 2 + 4 * tm * d * 4
    return pl.pallas_call(
        functools.partial(_out_proj_kernel, k_gate, alpha),
        grid=(n_tiles,),
        in_specs=[pl.BlockSpec((tm, d), lambda i: (i, 0)),
                  pl.BlockSpec((tm, d), lambda i: (i, 0)),
                  pl.BlockSpec((1, N_MOD, d), lambda i: (i // tiles_per_mod, 0, 0)),
                  _lead_spec(lead, (d, d), lambda i: (0, 0)),
                  pl.BlockSpec((1, d), lambda i: (0, 0)),
                  pl.BlockSpec((1, d), lambda i: (0, 0))],
        out_specs=pl.BlockSpec((tm, d), lambda i: (i, 0)),
        out_shape=jax.ShapeDtypeStruct((n, d), F32),
        compiler_params=_params(("parallel",), est),
        name="out_proj_norm",
    )(y, h, mods, w, g.reshape(1, d), b.reshape(1, d))


def _rope_tables(seq, dk):
    n_freq = dk // 4
    t = jnp.arange(seq)
    inv = ROPE_BASE ** (-jnp.arange(n_freq, dtype=F32) / n_freq)
    ang_r = (t // GRID_W).astype(F32)[:, None] * inv
    ang_c = (t % GRID_W).astype(F32)[:, None] * inv
    cos_t = jnp.concatenate([jnp.cos(ang_r), jnp.cos(ang_c)] * 2, axis=-1)
    sin_t = jnp.concatenate([-jnp.sin(ang_r), -jnp.sin(ang_c), jnp.sin(ang_r), jnp.sin(ang_c)], axis=-1)
    k_tab = jnp.stack([cos_t, sin_t])
    return jnp.stack([k_tab * (dk ** -0.5 * math.log2(math.e)), k_tab])


def _rope_column_layout(w, dk):
    rows, width = w.shape
    w = w.reshape(rows, width // dk, 2, 2, dk // 4)
    return jnp.swapaxes(w, 2, 3).reshape(rows, width)


def kernel(x, c, ctx, c_ctx, ada_w, ada_b, ln_g, ln_b, ffn_wg, ffn_wu, ffn_wd, sc_w_in, sc_conv,
           sc_w_out, da_w_qkv, da_lambda, da_subln, da_w_o, gm_w_in, gm_ln_g, gm_ln_b, gm_w_s,
           gm_b_s, gm_w_out):
    n_batch, seq, d = x.shape
    ctx_len = ctx.shape[1]
    depth = ada_w.shape[0]
    mixer_of_layer = tuple(i % N_MIXERS for i in range(depth))
    last_ctx_layer = max([i for i in range(depth) if mixer_of_layer[i] == 1], default=-1)
    alpha = (2.0 * depth) ** 0.25
    dv = da_subln.shape[-1]
    heads = d // dv
    dk = dv // 2

    h = x.reshape(n_batch * seq, d)
    hc = ctx.reshape(n_batch * ctx_len, d)

    n_cond = n_batch + 1
    cond = jnp.zeros((16 * ((n_cond + 15) // 16), d), F32)
    cond = cond.at[:n_batch].set(c).at[n_batch].set(c_ctx)

    rope_tabs = _rope_tables(seq, dk)
    no_rope_tabs = jnp.zeros((1, 2, ctx_len, dk), F32)

    ffn_f32 = (ffn_wg, ffn_wu, ffn_wd)
    ffn_w = [tuple(w[0, 0].astype(BF16) for w in ffn_f32)]
    conv_w = (sc_w_in.astype(BF16), sc_conv, sc_w_out.astype(BF16))
    qkv_w = da_w_qkv.astype(BF16)
    qkv_w = jnp.concatenate(
        [_rope_column_layout(qkv_w[..., :2 * d].reshape(-1, 2 * d), dk).reshape(qkv_w.shape[0], d, 2 * d),
         qkv_w[..., 2 * d:]], axis=-1)
    attn_wo = da_w_o.astype(BF16)
    gmlp_w = (gm_w_in.astype(BF16), gm_w_s.astype(BF16), gm_b_s[..., None], gm_w_out.astype(BF16))

    for i in range(depth):
        kind = mixer_of_layer[i]
        j = i // N_MIXERS
        ctx_in = i <= last_ctx_layer
        ctx_out = i < last_ctx_layer
        mods_all = _ada(cond, ada_w, ada_b, i).reshape(-1, N_MOD, d)
        md = mods_all[:n_batch]
        mdc = mods_all[n_batch:n_batch + 1]

        def ffn_pair(h_lat, h_ctx, half, k0):
            nxt = (i, 1) if half == 0 else (i + 1, 0)
            args = dict(ln_g=ln_g[i, 2 * half], ln_b=ln_b[i, 2 * half], k_shift=k0,
                        k_gate=k0 + 2, res_scale=0.5, alpha=alpha)
            w_now = ffn_w[0]
            if nxt[0] < depth:
                h_lat, w_next = _fused_call("ffn", h_lat, md, w_now, (), cast=(ffn_f32, nxt), **args)
                ffn_w[0] = tuple(w_next)
            else:
                h_lat = _fused_call("ffn", h_lat, md, w_now, (), **args)
            if h_ctx is not None:
                h_ctx = _fused_call("ffn", h_ctx, mdc, w_now, (), **args)
            return h_lat, h_ctx

        h, hc_new = ffn_pair(h, hc if ctx_in else None, 0, 0)
        if ctx_in:
            hc = hc_new

        if kind == 0:
            conv = functools.partial(_fused_call, "conv", weights=conv_w, lead=(j,),
                                     ln_g=ln_g[i, 1], ln_b=ln_b[i, 1], k_shift=3, k_gate=5,
                                     res_scale=1.0, alpha=alpha)
            h = conv(h, md, period=GRID_W)
            if ctx_out:
                hc = conv(hc, mdc, period=ctx_len)
        elif kind == 1:
            lam_init = 0.8 - 0.6 * math.exp(-0.3 * i)
            qk, vt = _qkv_call(h, md, qkv_w, (j,), rope_tabs, k_shift=3, n_rope=2, seq=seq)
            kc, vct = _qkv_call(hc, mdc, qkv_w, (j,), no_rope_tabs, k_shift=3, n_rope=0,
                                seq=ctx_len, first_sec=1)
            o = _attn_call(qk, vt, kc, vct, da_lambda[j], da_subln[j], n_batch=n_batch, seq=seq,
                           ctx_len=ctx_len, heads=heads, lam_init=lam_init)
            h = _out_proj_call(o, h, md, attn_wo, (j,), ln_g[i, 1], ln_b[i, 1], k_gate=5,
                               alpha=alpha)
            assert not ctx_out, "context-side attention output is not implemented"
        else:
            def gmlp(hh, mm):
                vn = _gate_branch_call(hh, mm, gmlp_w[0], (j,), gm_ln_g[j], gm_ln_b[j], k_shift=3)
                return _fused_call("gmlp", hh, mm, gmlp_w, (j,), ln_g[i, 1], ln_b[i, 1],
                                   k_shift=3, k_gate=5, res_scale=1.0, alpha=alpha, vn=vn)

            h = gmlp(h, md)
            if ctx_out:
                hc = gmlp(hc, mdc)

        h, hc_new = ffn_pair(h, hc if ctx_out else None, 1, 6)
        if ctx_out:
            hc = hc_new
    return h.reshape(n_batch, seq, d)
```

```python
import functools
import math

import jax
import jax.numpy as jnp
from jax import lax
from jax.experimental import pallas as pl
from jax.experimental.pallas import tpu as pltpu

GRID_W = 64
CHUNK = 128
N_MOD = 9
N_MIXERS = 3
ROPE_BASE = 10000.0
LN_EPS = 1e-5

V7X_LANES = 128
V7X_VMEM_BYTES = 64 * 1024 * 1024
V7X_VMEM_CAP = V7X_VMEM_BYTES - 3 * 1024 * 1024

BF16 = jnp.bfloat16
F32 = jnp.float32

ATTN_TQ = 1024
ATTN_GROUP = 4
FUSED_CHUNK = 512
CAST_ROWS = 16
FUSED_TILE = {"ffn": 1024, "conv": 1024, "gmlp": 512}
FUSED_ROW_BLOCK = {"ffn": 1024, "conv": 1024, "gmlp": 512}
PROJ_ROW_BLOCK = 256


def _params(semantics, vmem_estimate):
    limit = min(V7X_VMEM_CAP, max(32 * 1024 * 1024, int(vmem_estimate * 1.3)))
    return pltpu.CompilerParams(dimension_semantics=semantics, vmem_limit_bytes=limit)


def _pick(n, candidates):
    for c in candidates:
        if n % c == 0:
            return c
    return n


def _mod_row(mods_ref, k):
    return mods_ref[0, k:k + 1, :]


def _modulate_bf16(h, mods_ref, k_shift):
    shift = _mod_row(mods_ref, k_shift)
    scale = _mod_row(mods_ref, k_shift + 1)
    return (h * (1.0 + scale) + shift).astype(BF16)


def _layer_norm(x, g, b):
    mu = jnp.mean(x, axis=-1, keepdims=True)
    xc = x - mu
    var = jnp.mean(xc * xc, axis=-1, keepdims=True)
    return xc * lax.rsqrt(var + LN_EPS) * g + b


def _silu(x):
    return x / (1.0 + jnp.exp(-x))


def _gelu(x):
    return 0.5 * x * (1.0 + lax.erf(x * math.sqrt(0.5)))


def _dot(a, b):
    return jnp.dot(a, b, preferred_element_type=F32)


def _ada_kernel(cond_ref, w_ref, b_ref, o_ref):
    a = _silu(cond_ref[...]).astype(BF16)
    o_ref[...] = _dot(a, w_ref[...].astype(BF16)) + b_ref[...]


def _ada(cond, w, b, layer):
    m, d = cond.shape
    n = w.shape[2]
    tn = _pick(n, (1024, 512, 256, 128))
    est = 2 * d * tn * 4 + d * tn * 2 + 4 * m * (d + tn) * 4
    return pl.pallas_call(
        _ada_kernel,
        grid=(n // tn,),
        in_specs=[pl.BlockSpec((m, d), lambda j: (0, 0)),
                  pl.BlockSpec((None, d, tn), lambda j: (layer, 0, j)),
                  pl.BlockSpec((None, 1, tn), lambda j: (layer, 0, j))],
        out_specs=pl.BlockSpec((m, tn), lambda j: (0, j)),
        out_shape=jax.ShapeDtypeStruct((m, n), F32),
        compiler_params=_params(("parallel",), est),
        name="ada_mod",
    )(cond, w, b.reshape(b.shape[0], 1, n))


def _fused_kernel(kind, k_shift, k_gate, res_scale, alpha, period, n_steps, n_cast, *refs):
    h_ref, mods_ref = refs[0], refs[1]
    n_in = len(refs) - 2 - n_cast
    w = refs[2:n_in - 2 - n_cast]
    lng_ref, lnb_ref = refs[n_in - 2 - n_cast:n_in - n_cast]
    cast_in = refs[n_in - n_cast:n_in]
    o_ref = refs[n_in]
    cast_out = refs[n_in + 1:n_in + 1 + n_cast]
    a_ref = refs[-1]
    acc_ref = o_ref
    wd_ref = w[-1]
    tm = h_ref.shape[0]


    def row_blocks(size):
        rb = min(tm, max(size, period))
        return [slice(r, r + rb) for r in range(0, tm, rb)]

    def up(a, rows):
        if kind == "ffn":
            return _dot(a, w[0][...]), _dot(a, w[1][...])
        if kind == "conv":
            return _dot(a, w[0][...]), _dot(a, w[1][...]), _dot(a, w[2][...])
        vn_ref, ws_ref, bs_ref = w[1], w[2], w[3]
        gw = w[0].shape[1] // ws_ref.shape[0]
        cols = []
        for g in range(ws_ref.shape[0]):
            parts = [_dot(ws_ref[g], vn_ref[c:c + CHUNK, g * gw:(g + 1) * gw]) + bs_ref[g]
                     for c in range(rows.start, rows.stop, CHUNK)]
            cols.append(jnp.concatenate(parts, axis=0))
        return _dot(a, w[0][...]), jnp.concatenate(cols, axis=1)

    def hidden(pre):
        if kind == "ffn":
            g, u = pre
            return _silu(g) * u
        if kind == "conv":
            b, c, v = pre
            z = c * v
            pos = lax.broadcasted_iota(jnp.int32, z.shape, 0) & (period - 1)
            z_prev = jnp.where(pos == 0, 0.0, pltpu.roll(z, 1, 0))
            z_next = jnp.where(pos == period - 1, 0.0, pltpu.roll(z, z.shape[0] - 1, 0))
            cw = w[3][...]
            return b * (cw[0:1, :] * z_prev + cw[1:2, :] * z + cw[2:3, :] * z_next)
        u, s = pre
        return _gelu(u) * s

    def step(first, last):
        for src, dst in zip(cast_in, cast_out):
            dst[...] = src[...].astype(dst.dtype)

        def start(rows):
            if first:
                a = _modulate_bf16(h_ref[rows, :], mods_ref, k_shift)
                if not last:
                    a_ref[rows, :] = a
            else:
                a = a_ref[rows, :]
            return up(a, rows)

        def finish(rows, pre):
            acc = _dot(hidden(pre).astype(BF16), wd_ref[...])
            if not first:
                acc = acc_ref[rows, :] + acc
            if last:
                y = alpha * h_ref[rows, :] + (res_scale * _mod_row(mods_ref, k_gate)) * acc
                o_ref[rows, :] = _layer_norm(y, lng_ref[...], lnb_ref[...])
            else:
                acc_ref[rows, :] = acc

        pending = None
        for rows in row_blocks(FUSED_ROW_BLOCK[kind]):
            pre = start(rows)
            if pending is not None:
                finish(*pending)
            pending = (rows, pre)
        finish(*pending)

    if n_steps == 1:
        step(True, True)
        return
    j = pl.program_id(1)
    pl.when(j == 0)(functools.partial(step, True, False))
    if n_steps > 2:
        pl.when(jnp.logical_and(j > 0, j < n_steps - 1))(functools.partial(step, False, False))
    pl.when(j == n_steps - 1)(functools.partial(step, False, True))


def _lead_spec(lead, block, index_fn):
    lead = tuple(lead)
    return pl.BlockSpec((None,) * len(lead) + tuple(block),
                        lambda *g: lead + tuple(index_fn(*g)))


def _fused_call(kind, h, mods, weights, lead, ln_g, ln_b, *, k_shift, k_gate, res_scale, alpha,
                period=GRID_W, vn=None, cast=None):
    n, d = h.shape
    nb = mods.shape[0]
    tm = _pick(n // nb, (FUSED_TILE[kind], 512, 256, 128))
    n_tiles = n // tm
    tiles_per_mod = n_tiles // nb

    row = lambda i, j: (i, 0)
    col_blk = lambda i, j: (0, j)
    row_blk = lambda i, j: (j, 0)
    common_in = [pl.BlockSpec((tm, d), row),
                 pl.BlockSpec((1, N_MOD, d), lambda i, j: (i // tiles_per_mod, 0, 0))]
    if kind == "ffn":
        wg, wu, wd = weights
        f = wg.shape[-1]
        tc = _pick(f, (FUSED_CHUNK, 256, 128))
        n_chunks = f // tc
        w_in = [_lead_spec(lead, (d, tc), col_blk), _lead_spec(lead, (d, tc), col_blk),
                _lead_spec(lead, (tc, d), row_blk)]
        w_args = [wg, wu, wd]
        w_bytes = 3 * d * tc * 2
    elif kind == "conv":
        w_in3, cw, wd = weights
        tc = _pick(d, (FUSED_CHUNK, 256, 128))
        n_chunks = d // tc
        w_in = [_lead_spec(lead, (d, tc), col_blk),
                _lead_spec(lead, (d, tc), lambda i, j: (0, n_chunks + j)),
                _lead_spec(lead, (d, tc), lambda i, j: (0, 2 * n_chunks + j)),
                _lead_spec(lead, (3, tc), col_blk),
                _lead_spec(lead, (tc, d), row_blk)]
        w_args = [w_in3, w_in3, w_in3, cw, wd]
        w_bytes = 4 * d * tc * 2
    else:
        w_in2, ws, bs, wd = weights
        groups = ws.shape[-3]
        gw = wd.shape[-2] // groups
        per_step = max(g for g in range(1, groups + 1) if groups % g == 0 and g * gw <= 1024)
        tc = per_step * gw
        n_chunks = groups // per_step
        w_in = [_lead_spec(lead, (d, tc), col_blk),
                pl.BlockSpec((tm, tc), lambda i, j: (i, j)),
                _lead_spec(lead, (per_step, CHUNK, CHUNK), lambda i, j: (j, 0, 0)),
                _lead_spec(lead, (per_step, CHUNK, 1), lambda i, j: (j, 0, 0)),
                _lead_spec(lead, (tc, d), row_blk)]
        w_args = [w_in2, vn, ws, bs, wd]
        w_bytes = 2 * d * tc * 2 + tm * tc * 2
    vec = pl.BlockSpec((1, d), lambda i, j: (0, 0))
    rb = min(tm, FUSED_ROW_BLOCK[kind])
    est = (2 * 2 * tm * d * 4
           + tm * d * 2
           + 2 * w_bytes
           + 6 * rb * tc * 4
           + 2 * rb * d * 4)
    cast_arrays, cast_lead = cast if cast is not None else ((), ())
    n_steps = n_tiles * n_chunks
    cast_in, cast_out, cast_shapes = [], [], []
    for arr in cast_arrays:
        rows, cols = arr.shape[-2:]
        if (cols % (n_chunks * V7X_LANES) == 0 and rows % (n_tiles * CAST_ROWS) == 0):
            blk, idx = (rows // n_tiles, cols // n_chunks), (lambda i, j: (i, j))
        else:
            assert rows % (n_steps * CAST_ROWS) == 0, "cast array does not split over the grid"
            blk, idx = (rows // n_steps, cols), (lambda i, j: (i * n_chunks + j, 0))
        cast_in.append(_lead_spec(cast_lead, blk, idx))
        cast_out.append(pl.BlockSpec(blk, idx))
        cast_shapes.append(jax.ShapeDtypeStruct((rows, cols), BF16))
        est += 2 * blk[0] * blk[1] * (4 + 2)
    kern = functools.partial(_fused_kernel, kind, k_shift, k_gate, res_scale, alpha, period,
                             n_chunks, len(cast_arrays))
    outs = pl.pallas_call(
        kern,
        grid=(n_tiles, n_chunks),
        in_specs=common_in + w_in + [vec, vec] + cast_in,
        out_specs=[pl.BlockSpec((tm, d), row)] + cast_out,
        out_shape=[jax.ShapeDtypeStruct((n, d), F32)] + cast_shapes,
        scratch_shapes=[pltpu.VMEM((tm, d), BF16)],
        compiler_params=_params(("parallel", "arbitrary"), est),
        name="fused_" + kind,
    )(h, mods, *w_args, ln_g.reshape(1, d), ln_b.reshape(1, d), *cast_arrays)
    return outs[0] if cast is None else (outs[0], outs[1:])


def _qkv_kernel(k_shift, n_rope, h_ref, mods_ref, w_ref, tab_ref, o_ref, vt_ref, a_ref):
    s = pl.program_id(1)
    n_sec = pl.num_programs(1)
    cos_ref, sin_ref = tab_ref.at[0], tab_ref.at[1]

    @pl.when(s == 0)
    def _():
        a_ref[...] = _modulate_bf16(h_ref[...], mods_ref, k_shift)

    def project(rope, transposed):
        width = w_ref.shape[1]
        slab = _pick(width, (512, 256, 128))
        for c in range(width // slab):
            y = _dot(a_ref[...], w_ref[:, c * slab:(c + 1) * slab])
            if transposed:
                vt_ref[c * slab:(c + 1) * slab, :] = y.T.astype(vt_ref.dtype)
                continue
            for r in range(slab // V7X_LANES):
                yr = y[:, r * V7X_LANES:(r + 1) * V7X_LANES]
                if rope:
                    yr = yr * cos_ref[...] + pltpu.roll(yr, V7X_LANES // 2, 1) * sin_ref[...]
                lo = c * slab + r * V7X_LANES
                o_ref[:, lo:lo + V7X_LANES] = yr.astype(o_ref.dtype)

    if n_rope > 0:
        @pl.when(s < n_rope)
        def _():
            project(True, False)

    @pl.when(jnp.logical_and(s >= n_rope, s < n_sec - 1))
    def _():
        project(False, False)

    @pl.when(s == n_sec - 1)
    def _():
        project(False, True)


def _qkv_call(h, mods, w, lead, tabs, *, k_shift, n_rope, seq, first_sec=0):
    n, d = h.shape
    n_tab = tabs.shape[0]
    nb = mods.shape[0]
    n_sec = w.shape[-1] // d - first_sec
    tm = _pick(min(n // nb, seq), (512, 256, 128))
    n_tiles = n // tm
    tiles_per_mod = n_tiles // nb
    tiles_per_seq = seq // tm
    est = (2 * tm * d * 4 + tm * d * 2 + 2 * d * d * 2 + 2 * 2 * tm * d * 2
           + 8 * tm * V7X_LANES * 4 + 4 * tm * 512 * 4)
    kern = functools.partial(_qkv_kernel, k_shift, n_rope)
    return pl.pallas_call(
        kern,
        grid=(n_tiles, n_sec),
        in_specs=[pl.BlockSpec((tm, d), lambda i, s: (i, 0)),
                  pl.BlockSpec((1, N_MOD, d), lambda i, s: (i // tiles_per_mod, 0, 0)),
                  _lead_spec(lead, (d, d), lambda i, s: (0, first_sec + s)),
                  pl.BlockSpec((None, 2, tm, V7X_LANES),
                               lambda i, s: (jnp.minimum(s, n_tab - 1), 0, i % tiles_per_seq, 0))],
        out_specs=[pl.BlockSpec((tm, d), lambda i, s: (i, jnp.minimum(s, n_sec - 2))),
                   pl.BlockSpec((None, d, tm), lambda i, s: (i, 0, 0))],
        out_shape=[jax.ShapeDtypeStruct((n, (n_sec - 1) * d), BF16),
                   jax.ShapeDtypeStruct((n_tiles, d, tm), BF16)],
        scratch_shapes=[pltpu.VMEM((tm, d), BF16)],
        compiler_params=_params(("parallel", "arbitrary"), est),
        name="qkv_proj",
    )(h, mods, w, tabs)


def _row_blocks(n_rows, block):
    block = min(n_rows, block)
    return [slice(r, r + block) for r in range(0, n_rows, block)]


def _gate_branch_kernel(k_shift, h_ref, mods_ref, w_ref, g_ref, b_ref, o_ref):
    for rows in _row_blocks(h_ref.shape[0], PROJ_ROW_BLOCK):
        a = _modulate_bf16(h_ref[rows, :], mods_ref, k_shift)
        v = _gelu(_dot(a, w_ref[...]))
        o_ref[rows, :] = _layer_norm(v, g_ref[...], b_ref[...]).astype(o_ref.dtype)


def _gate_branch_call(h, mods, w, lead, g, b, *, k_shift):
    n, d = h.shape
    nb = mods.shape[0]
    width = w.shape[-1] // 2
    tm = _pick(n // nb, (512, 256, 128))
    n_tiles = n // tm
    tiles_per_mod = n_tiles // nb
    est = 2 * tm * d * 4 + 2 * d * width * 2 + 2 * tm * width * 2 + 4 * tm * width * 4
    return pl.pallas_call(
        functools.partial(_gate_branch_kernel, k_shift),
        grid=(n_tiles,),
        in_specs=[pl.BlockSpec((tm, d), lambda i: (i, 0)),
                  pl.BlockSpec((1, N_MOD, d), lambda i: (i // tiles_per_mod, 0, 0)),
                  _lead_spec(lead, (d, width), lambda i: (0, 1)),
                  pl.BlockSpec((1, width), lambda i: (0, 0)),
                  pl.BlockSpec((1, width), lambda i: (0, 0))],
        out_specs=pl.BlockSpec((tm, width), lambda i: (i, 0)),
        out_shape=jax.ShapeDtypeStruct((n, width), BF16),
        compiler_params=_params(("parallel",), est),
        name="gmlp_gate_branch",
    )(h, mods, w, g.reshape(1, width), b.reshape(1, width))


def _attn_kernel(lam_init, lam_ref, q_ref, k_ref, vt_ref, kc_ref, vct_ref, subln_ref, o_ref,
                 acc_ref, s_ref):
    dk = q_ref.shape[1] // 2
    tq = q_ref.shape[0]
    tk = vt_ref.shape[2]
    q = q_ref[...]
    qm = (q[:, :dk], q[:, dk:])
    nt = (((1,), (1,)), ((), ()))

    def update(carry, blocks):
        for g, (k_blk, _) in enumerate(blocks):
            for mp in range(2):
                s_ref[g, mp, 0:k_blk.shape[0], :] = lax.dot_general(
                    k_blk[:, mp * dk:(mp + 1) * dk], qm[mp], nt, preferred_element_type=F32)
        carry = list(carry)
        for g, (k_blk, vt_blk) in enumerate(blocks):
            for mp in range(2):
                m_old, l_old = carry[2 * mp], carry[2 * mp + 1]
                st = s_ref[g, mp, 0:k_blk.shape[0], :]
                m_new = jnp.maximum(m_old, jnp.max(st, axis=0, keepdims=True))
                pt = jnp.exp2(st - m_new)
                corr = jnp.exp2(m_old - m_new)
                carry[2 * mp] = m_new
                carry[2 * mp + 1] = corr * l_old + jnp.sum(pt, axis=0, keepdims=True)
                acc_ref[mp] = corr * acc_ref[mp] + _dot(vt_blk, pt.astype(BF16))
        return tuple(carry)

    acc_ref[...] = jnp.zeros_like(acc_ref)
    neg = jnp.full((1, tq), -jnp.inf, F32)
    zero = jnp.zeros((1, tq), F32)
    group = s_ref.shape[0]

    def body(c, carry):
        blocks = []
        for g in range(group):
            start = pl.multiple_of((c * group + g) * tk, tk)
            blocks.append((k_ref[pl.ds(start, tk), :], vt_ref[c * group + g]))
        return update(carry, blocks)

    carry = lax.fori_loop(0, vt_ref.shape[0] // group, body, (neg, zero, neg, zero))
    tc = vct_ref.shape[2]
    carry = update(carry, [(kc_ref[c * tc:(c + 1) * tc, :], vct_ref[c])
                           for c in range(vct_ref.shape[0])])
    _, l0, _, l1 = carry

    lp = lam_ref[...]
    lam = (jnp.exp(jnp.sum(lp[0:1, :] * lp[1:2, :], axis=-1, keepdims=True))
           - jnp.exp(jnp.sum(lp[2:3, :] * lp[3:4, :], axis=-1, keepdims=True)) + lam_init)
    ot = acc_ref[0] / l0 - lam * (acc_ref[1] / l1)
    ot = ot * lax.rsqrt(jnp.mean(ot * ot, axis=0, keepdims=True) + LN_EPS) * subln_ref[...]
    o_ref[...] = (ot * (1.0 - lam_init)).T.astype(o_ref.dtype)


def _attn_call(qk, vt, kc, vct, lam_p, subln, *, n_batch, seq, ctx_len, heads, lam_init):
    d = qk.shape[1] // 2
    dv = d // heads
    tk = vt.shape[2]
    tc = vct.shape[2]
    nk = seq // tk
    nc = ctx_len // tc
    group = _pick(nk, (ATTN_GROUP, 1))
    assert nc <= group and tc <= tk, "context keys must fit one score-scratch group"
    tq = _pick(seq, (ATTN_TQ, 128))
    nq = seq // tq
    est = (2 * 2 * tq * dv * 2 + 2 * 2 * seq * dv * 2 + 2 * 2 * ctx_len * dv * 2
           + 2 * tq * dv * 4 + (2 * group + 6) * tq * tk * 4)
    kern = functools.partial(_attn_kernel, lam_init)
    return pl.pallas_call(
        kern,
        grid=(n_batch, heads, nq),
        in_specs=[pl.BlockSpec(lam_p.shape, lambda b, h, i: (0, 0)),
                  pl.BlockSpec((tq, dv), lambda b, h, i: (b * nq + i, h)),
                  pl.BlockSpec((seq, dv), lambda b, h, i: (b, heads + h)),
                  pl.BlockSpec((nk, dv, tk), lambda b, h, i: (b, h, 0)),
                  pl.BlockSpec((ctx_len, dv), lambda b, h, i: (b, h)),
                  pl.BlockSpec((nc, dv, tc), lambda b, h, i: (b, h, 0)),
                  pl.BlockSpec((dv, 1), lambda b, h, i: (0, 0))],
        out_specs=pl.BlockSpec((tq, dv), lambda b, h, i: (b * nq + i, h)),
        out_shape=jax.ShapeDtypeStruct((n_batch * seq, d), BF16),
        scratch_shapes=[pltpu.VMEM((2, dv, tq), F32), pltpu.VMEM((group, 2, tk, tq), F32)],
        compiler_params=_params(("parallel", "parallel", "arbitrary"), est),
        name="diff_attention",
    )(lam_p, qk, qk, vt, kc, vct, subln.reshape(dv, 1))


def _out_proj_kernel(k_gate, alpha, y_ref, h_ref, mods_ref, w_ref, g_ref, b_ref, o_ref):
    for rows in _row_blocks(h_ref.shape[0], PROJ_ROW_BLOCK):
        y = _dot(y_ref[rows, :], w_ref[...])
        x = alpha * h_ref[rows, :] + _mod_row(mods_ref, k_gate) * y
        o_ref[rows, :] = _layer_norm(x, g_ref[...], b_ref[...])


def _out_proj_call(y, h, mods, w, lead, g, b, *, k_gate, alpha):
    n, d = h.shape
    nb = mods.shape[0]
    tm = _pick(n // nb, (512, 256, 128))
    n_tiles = n // tm
    tiles_per_mod = n_tiles // nb
    est = 2 * tm * d * 2 + 2 * 2 * tm * d * 4 + 2 * d * d * 2 + 4 * tm * d * 4
    return pl.pallas_call(
        functools.partial(_out_proj_kernel, k_gate, alpha),
        grid=(n_tiles,),
        in_specs=[pl.BlockSpec((tm, d), lambda i: (i, 0)),
                  pl.BlockSpec((tm, d), lambda i: (i, 0)),
                  pl.BlockSpec((1, N_MOD, d), lambda i: (i // tiles_per_mod, 0, 0)),
                  _lead_spec(lead, (d, d), lambda i: (0, 0)),
                  pl.BlockSpec((1, d), lambda i: (0, 0)),
                  pl.BlockSpec((1, d), lambda i: (0, 0))],
        out_specs=pl.BlockSpec((tm, d), lambda i: (i, 0)),
        out_shape=jax.ShapeDtypeStruct((n, d), F32),
        compiler_params=_params(("parallel",), est),
        name="out_proj_norm",
    )(y, h, mods, w, g.reshape(1, d), b.reshape(1, d))


def _rope_tables(seq, dk):
    n_freq = dk // 4
    t = jnp.arange(seq)
    inv = ROPE_BASE ** (-jnp.arange(n_freq, dtype=F32) / n_freq)
    ang_r = (t // GRID_W).astype(F32)[:, None] * inv
    ang_c = (t % GRID_W).astype(F32)[:, None] * inv
    cos_t = jnp.concatenate([jnp.cos(ang_r), jnp.cos(ang_c)] * 2, axis=-1)
    sin_t = jnp.concatenate([-jnp.sin(ang_r), -jnp.sin(ang_c), jnp.sin(ang_r), jnp.sin(ang_c)], axis=-1)
    k_tab = jnp.stack([cos_t, sin_t])
    return jnp.stack([k_tab * (dk ** -0.5 * math.log2(math.e)), k_tab])


def _rope_column_layout(w, dk):
    rows, width = w.shape
    w = w.reshape(rows, width // dk, 2, 2, dk // 4)
    return jnp.swapaxes(w, 2, 3).reshape(rows, width)


def kernel(x, c, ctx, c_ctx, ada_w, ada_b, ln_g, ln_b, ffn_wg, ffn_wu, ffn_wd, sc_w_in, sc_conv,
           sc_w_out, da_w_qkv, da_lambda, da_subln, da_w_o, gm_w_in, gm_ln_g, gm_ln_b, gm_w_s,
           gm_b_s, gm_w_out):
    n_batch, seq, d = x.shape
    ctx_len = ctx.shape[1]
    depth = ada_w.shape[0]
    mixer_of_layer = tuple(i % N_MIXERS for i in range(depth))
    last_ctx_layer = max([i for i in range(depth) if mixer_of_layer[i] == 1], default=-1)
    alpha = (2.0 * depth) ** 0.25
    dv = da_subln.shape[-1]
    heads = d // dv
    dk = dv // 2

    h = x.reshape(n_batch * seq, d)
    hc = ctx.reshape(n_batch * ctx_len, d)

    n_cond = n_batch + 1
    cond = jnp.zeros((16 * ((n_cond + 15) // 16), d), F32)
    cond = cond.at[:n_batch].set(c).at[n_batch].set(c_ctx)

    rope_tabs = _rope_tables(seq, dk)
    no_rope_tabs = jnp.zeros((1, 2, ctx_len, dk), F32)

    ffn_f32 = (ffn_wg, ffn_wu, ffn_wd)
    ffn_w = [tuple(w[0, 0].astype(BF16) for w in ffn_f32)]
    conv_w = (sc_w_in.astype(BF16), sc_conv, sc_w_out.astype(BF16))
    qkv_w = da_w_qkv.astype(BF16)
    qkv_w = jnp.concatenate(
        [_rope_column_layout(qkv_w[..., :2 * d].reshape(-1, 2 * d), dk).reshape(qkv_w.shape[0], d, 2 * d),
         qkv_w[..., 2 * d:]], axis=-1)
    attn_wo = da_w_o.astype(BF16)
    gmlp_w = (gm_w_in.astype(BF16), gm_w_s.astype(BF16), gm_b_s[..., None], gm_w_out.astype(BF16))

    for i in range(depth):
        kind = mixer_of_layer[i]
        j = i // N_MIXERS
        ctx_in = i <= last_ctx_layer
        ctx_out = i < last_ctx_layer
        mods_all = _ada(cond, ada_w, ada_b, i).reshape(-1, N_MOD, d)
        md = mods_all[:n_batch]
        mdc = mods_all[n_batch:n_batch + 1]

        def ffn_pair(h_lat, h_ctx, half, k0):
            nxt = (i, 1) if half == 0 else (i + 1, 0)
            args = dict(ln_g=ln_g[i, 2 * half], ln_b=ln_b[i, 2 * half], k_shift=k0,
                        k_gate=k0 + 2, res_scale=0.5, alpha=alpha)
            w_now = ffn_w[0]
            if nxt[0] < depth:
                h_lat, w_next = _fused_call("ffn", h_lat, md, w_now, (), cast=(ffn_f32, nxt), **args)
                ffn_w[0] = tuple(w_next)
            else:
                h_lat = _fused_call("ffn", h_lat, md, w_now, (), **args)
            if h_ctx is not None:
                h_ctx = _fused_call("ffn", h_ctx, mdc, w_now, (), **args)
            return h_lat, h_ctx

        h, hc_new = ffn_pair(h, hc if ctx_in else None, 0, 0)
        if ctx_in:
            hc = hc_new

        if kind == 0:
            conv = functools.partial(_fused_call, "conv", weights=conv_w, lead=(j,),
                                     ln_g=ln_g[i, 1], ln_b=ln_b[i, 1], k_shift=3, k_gate=5,
                                     res_scale=1.0, alpha=alpha)
            h = conv(h, md, period=GRID_W)
            if ctx_out:
                hc = conv(hc, mdc, period=ctx_len)
        elif kind == 1:
            lam_init = 0.8 - 0.6 * math.exp(-0.3 * i)
            qk, vt = _qkv_call(h, md, qkv_w, (j,), rope_tabs, k_shift=3, n_rope=2, seq=seq)
            kc, vct = _qkv_call(hc, mdc, qkv_w, (j,), no_rope_tabs, k_shift=3, n_rope=0,
                                seq=ctx_len, first_sec=1)
            o = _attn_call(qk, vt, kc, vct, da_lambda[j], da_subln[j], n_batch=n_batch, seq=seq,
                           ctx_len=ctx_len, heads=heads, lam_init=lam_init)
            h = _out_proj_call(o, h, md, attn_wo, (j,), ln_g[i, 1], ln_b[i, 1], k_gate=5,
                               alpha=alpha)
            assert not ctx_out, "context-side attention output is not implemented"
        else:
            def gmlp(hh, mm):
                vn = _gate_branch_call(hh, mm, gmlp_w[0], (j,), gm_ln_g[j], gm_ln_b[j], k_shift=3)
                return _fused_call("gmlp", hh, mm, gmlp_w, (j,), ln_g[i, 1], ln_b[i, 1],
                                   k_shift=3, k_gate=5, res_scale=1.0, alpha=alpha, vn=vn)

            h = gmlp(h, md)
            if ctx_out:
                hc = gmlp(hc, mdc)

        h, hc_new = ffn_pair(h, hc if ctx_out else None, 1, 6)
        if ctx_out:
            hc = hc_new
    return h.reshape(n_batch, seq, d)
```

```python
import functools
import math

import jax
import jax.numpy as jnp
from jax import lax
from jax.experimental import pallas as pl
from jax.experimental.pallas import tpu as pltpu

GRID_W = 64
CHUNK = 128
N_MOD = 9
N_MIXERS = 3
ROPE_BASE = 10000.0
LN_EPS = 1e-5

V7X_LANES = 128
V7X_VMEM_BYTES = 64 * 1024 * 1024
V7X_VMEM_CAP = V7X_VMEM_BYTES - 6 * 1024 * 1024

BF16 = jnp.bfloat16
F32 = jnp.float32

ATTN_TQ = 1024
ATTN_GROUP = 4
FUSED_CHUNK = 512
CAST_ROWS = 16
FUSED_TILE = {"ffn": 1024, "conv": 1024, "gmlp": 512}
FUSED_ROW_BLOCK = {"ffn": 1024, "conv": 512, "gmlp": 256}
PROJ_ROW_BLOCK = 256


def _params(semantics, vmem_estimate):
    limit = min(V7X_VMEM_CAP, max(32 * 1024 * 1024, int(vmem_estimate * 1.3)))
    return pltpu.CompilerParams(dimension_semantics=semantics, vmem_limit_bytes=limit)


def _pick(n, candidates):
    for c in candidates:
        if n % c == 0:
            return c
    return n


def _mod_row(mods_ref, k):
    return mods_ref[0, k:k + 1, :]


def _modulate_bf16(h, mods_ref, k_shift):
    shift = _mod_row(mods_ref, k_shift)
    scale = _mod_row(mods_ref, k_shift + 1)
    return (h * (1.0 + scale) + shift).astype(BF16)


def _layer_norm(x, g, b, eps=LN_EPS):
    mu = jnp.mean(x, axis=-1, keepdims=True)
    xc = x - mu
    var = jnp.mean(xc * xc, axis=-1, keepdims=True)
    return xc * lax.rsqrt(var + eps) * g + b


def _deepnorm(h, delta_scale, delta, alpha, g, b):
    return _layer_norm(h + (delta_scale * (1.0 / alpha)) * delta, g, b, LN_EPS / (alpha * alpha))


def _silu(x):
    return x / (1.0 + jnp.exp(-x))


def _gelu(x):
    return 0.5 * x * (1.0 + lax.erf(x * math.sqrt(0.5)))


def _dot(a, b):
    return jnp.dot(a, b, preferred_element_type=F32)


def _ada_kernel(cond_ref, w_ref, b_ref, o_ref):
    a = _silu(cond_ref[...]).astype(BF16)
    o_ref[...] = _dot(a, w_ref[...].astype(BF16)) + b_ref[...]


def _ada(cond, w, b, layer):
    m, d = cond.shape
    n = w.shape[2]
    tn = _pick(n, (1024, 512, 256, 128))
    est = 2 * d * tn * 4 + d * tn * 2 + 4 * m * (d + tn) * 4
    return pl.pallas_call(
        _ada_kernel,
        grid=(n // tn,),
        in_specs=[pl.BlockSpec((m, d), lambda j: (0, 0)),
                  pl.BlockSpec((None, d, tn), lambda j: (layer, 0, j)),
                  pl.BlockSpec((None, 1, tn), lambda j: (layer, 0, j))],
        out_specs=pl.BlockSpec((m, tn), lambda j: (0, j)),
        out_shape=jax.ShapeDtypeStruct((m, n), F32),
        compiler_params=_params(("parallel",), est),
        name="ada_mod",
    )(cond, w, b.reshape(b.shape[0], 1, n))


def _fused_kernel(kind, k_shift, k_gate, res_scale, alpha, period, n_steps, n_cast, *refs):
    h_ref, mods_ref = refs[0], refs[1]
    n_in = len(refs) - 2 - n_cast
    w = refs[2:n_in - 2 - n_cast]
    lng_ref, lnb_ref = refs[n_in - 2 - n_cast:n_in - n_cast]
    cast_in = refs[n_in - n_cast:n_in]
    o_ref = refs[n_in]
    cast_out = refs[n_in + 1:n_in + 1 + n_cast]
    a_ref = refs[-1]
    acc_ref = o_ref
    wd_ref = w[-1]
    tm = h_ref.shape[0]


    def row_blocks(size):
        rb = min(tm, max(size, period))
        return [slice(r, r + rb) for r in range(0, tm, rb)]

    def up(a, rows):
        if kind == "ffn":
            return _dot(a, w[0][...]), _dot(a, w[1][...])
        if kind == "conv":
            return _dot(a, w[0][...]), _dot(a, w[1][...]), _dot(a, w[2][...])
        vn_ref, ws_ref, bs_ref = w[1], w[2], w[3]
        gw = w[0].shape[1] // ws_ref.shape[0]
        cols = []
        for g in range(ws_ref.shape[0]):
            parts = [_dot(ws_ref[g], vn_ref[c:c + CHUNK, g * gw:(g + 1) * gw]) + bs_ref[g]
                     for c in range(rows.start, rows.stop, CHUNK)]
            cols.append(jnp.concatenate(parts, axis=0))
        return _dot(a, w[0][...]), jnp.concatenate(cols, axis=1)

    def hidden(pre):
        if kind == "ffn":
            g, u = pre
            return _silu(g) * u
        if kind == "conv":
            b, c, v = pre
            z = c * v
            pos = lax.broadcasted_iota(jnp.int32, z.shape, 0) & (period - 1)
            z_prev = jnp.where(pos == 0, 0.0, pltpu.roll(z, 1, 0))
            z_next = jnp.where(pos == period - 1, 0.0, pltpu.roll(z, z.shape[0] - 1, 0))
            cw = w[3][...]
            return b * (cw[0:1, :] * z_prev + cw[1:2, :] * z + cw[2:3, :] * z_next)
        u, s = pre
        return _gelu(u) * s

    def step(first, last):
        for src, dst in zip(cast_in, cast_out):
            dst[...] = src[...].astype(dst.dtype)

        def start(rows):
            if first:
                a = _modulate_bf16(h_ref[rows, :], mods_ref, k_shift)
                if not last:
                    a_ref[rows, :] = a
            else:
                a = a_ref[rows, :]
            return up(a, rows)

        def finish(rows, pre):
            acc = _dot(hidden(pre).astype(BF16), wd_ref[...])
            if not first:
                acc = acc_ref[rows, :] + acc
            if last:
                o_ref[rows, :] = _deepnorm(h_ref[rows, :], res_scale * _mod_row(mods_ref, k_gate),
                                           acc, alpha, lng_ref[...], lnb_ref[...])
            else:
                acc_ref[rows, :] = acc

        pending = None
        for rows in row_blocks(FUSED_ROW_BLOCK[kind]):
            pre = start(rows)
            if pending is not None:
                finish(*pending)
            pending = (rows, pre)
        finish(*pending)

    if n_steps == 1:
        step(True, True)
        return
    j = pl.program_id(1)
    pl.when(j == 0)(functools.partial(step, True, False))
    if n_steps > 2:
        pl.when(jnp.logical_and(j > 0, j < n_steps - 1))(functools.partial(step, False, False))
    pl.when(j == n_steps - 1)(functools.partial(step, False, True))


def _lead_spec(lead, block, index_fn):
    lead = tuple(lead)
    return pl.BlockSpec((None,) * len(lead) + tuple(block),
                        lambda *g: lead + tuple(index_fn(*g)))


def _fused_call(kind, h, mods, weights, lead, ln_g, ln_b, *, k_shift, k_gate, res_scale, alpha,
                period=GRID_W, vn=None, cast=None):
    n, d = h.shape
    nb = mods.shape[0]
    tm = _pick(n // nb, (FUSED_TILE[kind], 512, 256, 128))
    n_tiles = n // tm
    tiles_per_mod = n_tiles // nb

    row = lambda i, j: (i, 0)
    col_blk = lambda i, j: (0, j)
    row_blk = lambda i, j: (j, 0)
    common_in = [pl.BlockSpec((tm, d), row),
                 pl.BlockSpec((1, N_MOD, d), lambda i, j: (i // tiles_per_mod, 0, 0))]
    if kind == "ffn":
        wg, wu, wd = weights
        f = wg.shape[-1]
        tc = _pick(f, (FUSED_CHUNK, 256, 128))
        n_chunks = f // tc
        w_in = [_lead_spec(lead, (d, tc), col_blk), _lead_spec(lead, (d, tc), col_blk),
                _lead_spec(lead, (tc, d), row_blk)]
        w_args = [wg, wu, wd]
        w_bytes = 3 * d * tc * 2
    elif kind == "conv":
        assert period & (period - 1) == 0 and tm % period == 0, (tm, period)
        w_in3, cw, wd = weights
        tc = _pick(d, (FUSED_CHUNK, 256, 128))
        n_chunks = d // tc
        w_in = [_lead_spec(lead, (d, tc), col_blk),
                _lead_spec(lead, (d, tc), lambda i, j: (0, n_chunks + j)),
                _lead_spec(lead, (d, tc), lambda i, j: (0, 2 * n_chunks + j)),
                _lead_spec(lead, (3, tc), col_blk),
                _lead_spec(lead, (tc, d), row_blk)]
        w_args = [w_in3, w_in3, w_in3, cw, wd]
        w_bytes = 4 * d * tc * 2
    else:
        w_in2, ws, bs, wd = weights
        groups = ws.shape[-3]
        gw = wd.shape[-2] // groups
        per_step = max(g for g in range(1, groups + 1) if groups % g == 0 and g * gw <= 1024)
        tc = per_step * gw
        n_chunks = groups // per_step
        w_in = [_lead_spec(lead, (d, tc), col_blk),
                pl.BlockSpec((tm, tc), lambda i, j: (i, j)),
                _lead_spec(lead, (per_step, CHUNK, CHUNK), lambda i, j: (j, 0, 0)),
                _lead_spec(lead, (per_step, CHUNK, 1), lambda i, j: (j, 0, 0)),
                _lead_spec(lead, (tc, d), row_blk)]
        w_args = [w_in2, vn, ws, bs, wd]
        w_bytes = 2 * d * tc * 2 + tm * tc * 2
    vec = pl.BlockSpec((1, d), lambda i, j: (0, 0))
    rb = min(tm, FUSED_ROW_BLOCK[kind])
    est = (2 * 2 * tm * d * 4
           + tm * d * 2
           + 2 * w_bytes
           + 6 * rb * tc * 4
           + 2 * rb * d * 4)
    cast_arrays, cast_lead = cast if cast is not None else ((), ())
    n_steps = n_tiles * n_chunks
    cast_in, cast_out, cast_shapes = [], [], []
    for arr in cast_arrays:
        rows, cols = arr.shape[-2:]
        if (cols % (n_chunks * V7X_LANES) == 0 and rows % (n_tiles * CAST_ROWS) == 0):
            blk, idx = (rows // n_tiles, cols // n_chunks), (lambda i, j: (i, j))
        else:
            assert rows % (n_steps * CAST_ROWS) == 0, "cast array does not split over the grid"
            blk, idx = (rows // n_steps, cols), (lambda i, j: (i * n_chunks + j, 0))
        cast_in.append(_lead_spec(cast_lead, blk, idx))
        cast_out.append(pl.BlockSpec(blk, idx))
        cast_shapes.append(jax.ShapeDtypeStruct((rows, cols), BF16))
        est += 2 * blk[0] * blk[1] * (4 + 2)
    kern = functools.partial(_fused_kernel, kind, k_shift, k_gate, res_scale, alpha, period,
                             n_chunks, len(cast_arrays))
    outs = pl.pallas_call(
        kern,
        grid=(n_tiles, n_chunks),
        in_specs=common_in + w_in + [vec, vec] + cast_in,
        out_specs=[pl.BlockSpec((tm, d), row)] + cast_out,
        out_shape=[jax.ShapeDtypeStruct((n, d), F32)] + cast_shapes,
        scratch_shapes=[pltpu.VMEM((tm, d), BF16)],
        compiler_params=_params(("parallel", "arbitrary"), est),
        name="fused_" + kind,
    )(h, mods, *w_args, ln_g.reshape(1, d), ln_b.reshape(1, d), *cast_arrays)
    return outs[0] if cast is None else (outs[0], outs[1:])


def _qkv_kernel(k_shift, n_rope, h_ref, mods_ref, w_ref, tab_ref, o_ref, vt_ref, a_ref):
    s = pl.program_id(1)
    n_sec = pl.num_programs(1)
    cos_ref, sin_ref = tab_ref.at[0], tab_ref.at[1]

    @pl.when(s == 0)
    def _():
        a_ref[...] = _modulate_bf16(h_ref[...], mods_ref, k_shift)

    def project(rope, transposed):
        width = w_ref.shape[1]
        slab = _pick(width, (512, 256, 128))
        for c in range(width // slab):
            y = _dot(a_ref[...], w_ref[:, c * slab:(c + 1) * slab])
            if transposed:
                vt_ref[c * slab:(c + 1) * slab, :] = y.T.astype(vt_ref.dtype)
                continue
            for r in range(slab // V7X_LANES):
                yr = y[:, r * V7X_LANES:(r + 1) * V7X_LANES]
                if rope:
                    yr = yr * cos_ref[...] + pltpu.roll(yr, V7X_LANES // 2, 1) * sin_ref[...]
                lo = c * slab + r * V7X_LANES
                o_ref[:, lo:lo + V7X_LANES] = yr.astype(o_ref.dtype)

    if n_rope > 0:
        @pl.when(s < n_rope)
        def _():
            project(True, False)

    @pl.when(jnp.logical_and(s >= n_rope, s < n_sec - 1))
    def _():
        project(False, False)

    @pl.when(s == n_sec - 1)
    def _():
        project(False, True)


def _qkv_call(h, mods, w, lead, tabs, *, k_shift, n_rope, seq, first_sec=0):
    n, d = h.shape
    n_tab = tabs.shape[0]
    nb = mods.shape[0]
    n_sec = w.shape[-1] // d - first_sec
    tm = _pick(min(n // nb, seq), (512, 256, 128))
    n_tiles = n // tm
    tiles_per_mod = n_tiles // nb
    tiles_per_seq = seq // tm
    est = (2 * tm * d * 4 + tm * d * 2 + 2 * d * d * 2 + 2 * 2 * tm * d * 2
           + 8 * tm * V7X_LANES * 4 + 4 * tm * 512 * 4)
    kern = functools.partial(_qkv_kernel, k_shift, n_rope)
    return pl.pallas_call(
        kern,
        grid=(n_tiles, n_sec),
        in_specs=[pl.BlockSpec((tm, d), lambda i, s: (i, 0)),
                  pl.BlockSpec((1, N_MOD, d), lambda i, s: (i // tiles_per_mod, 0, 0)),
                  _lead_spec(lead, (d, d), lambda i, s: (0, first_sec + s)),
                  pl.BlockSpec((None, 2, tm, V7X_LANES),
                               lambda i, s: (jnp.minimum(s, n_tab - 1), 0, i % tiles_per_seq, 0))],
        out_specs=[pl.BlockSpec((tm, d), lambda i, s: (i, jnp.minimum(s, n_sec - 2))),
                   pl.BlockSpec((None, d, tm), lambda i, s: (i, 0, 0))],
        out_shape=[jax.ShapeDtypeStruct((n, (n_sec - 1) * d), BF16),
                   jax.ShapeDtypeStruct((n_tiles, d, tm), BF16)],
        scratch_shapes=[pltpu.VMEM((tm, d), BF16)],
        compiler_params=_params(("parallel", "arbitrary"), est),
        name="qkv_proj",
    )(h, mods, w, tabs)


def _row_blocks(n_rows, block):
    block = min(n_rows, block)
    return [slice(r, r + block) for r in range(0, n_rows, block)]


def _gate_branch_kernel(k_shift, h_ref, mods_ref, w_ref, g_ref, b_ref, o_ref):
    for rows in _row_blocks(h_ref.shape[0], PROJ_ROW_BLOCK):
        a = _modulate_bf16(h_ref[rows, :], mods_ref, k_shift)
        v = _gelu(_dot(a, w_ref[...]))
        o_ref[rows, :] = _layer_norm(v, g_ref[...], b_ref[...]).astype(o_ref.dtype)


def _gate_branch_call(h, mods, w, lead, g, b, *, k_shift):
    n, d = h.shape
    nb = mods.shape[0]
    width = w.shape[-1] // 2
    tm = _pick(n // nb, (512, 256, 128))
    n_tiles = n // tm
    tiles_per_mod = n_tiles // nb
    est = 2 * tm * d * 4 + 2 * d * width * 2 + 2 * tm * width * 2 + 4 * tm * width * 4
    return pl.pallas_call(
        functools.partial(_gate_branch_kernel, k_shift),
        grid=(n_tiles,),
        in_specs=[pl.BlockSpec((tm, d), lambda i: (i, 0)),
                  pl.BlockSpec((1, N_MOD, d), lambda i: (i // tiles_per_mod, 0, 0)),
                  _lead_spec(lead, (d, width), lambda i: (0, 1)),
                  pl.BlockSpec((1, width), lambda i: (0, 0)),
                  pl.BlockSpec((1, width), lambda i: (0, 0))],
        out_specs=pl.BlockSpec((tm, width), lambda i: (i, 0)),
        out_shape=jax.ShapeDtypeStruct((n, width), BF16),
        compiler_params=_params(("parallel",), est),
        name="gmlp_gate_branch",
    )(h, mods, w, g.reshape(1, width), b.reshape(1, width))


def _attn_kernel(lam_init, lam_ref, q_ref, k_ref, vt_ref, kc_ref, vct_ref, subln_ref, o_ref,
                 acc_ref, s_ref):
    dk = q_ref.shape[1] // 2
    tq = q_ref.shape[0]
    tk = vt_ref.shape[2]
    q = q_ref[...]
    qm = (q[:, :dk], q[:, dk:])
    nt = (((1,), (1,)), ((), ()))

    def update(carry, blocks):
        for g, (k_blk, _) in enumerate(blocks):
            for mp in range(2):
                s_ref[g, mp, 0:k_blk.shape[0], :] = lax.dot_general(
                    k_blk[:, mp * dk:(mp + 1) * dk], qm[mp], nt, preferred_element_type=F32)
        carry = list(carry)
        for g, (k_blk, vt_blk) in enumerate(blocks):
            for mp in range(2):
                m_old, l_old = carry[2 * mp], carry[2 * mp + 1]
                st = s_ref[g, mp, 0:k_blk.shape[0], :]
                m_new = jnp.maximum(m_old, jnp.max(st, axis=0, keepdims=True))
                pt = jnp.exp2(st - m_new)
                corr = jnp.exp2(m_old - m_new)
                carry[2 * mp] = m_new
                carry[2 * mp + 1] = corr * l_old + jnp.sum(pt, axis=0, keepdims=True)
                acc_ref[mp] = corr * acc_ref[mp] + _dot(vt_blk, pt.astype(BF16))
        return tuple(carry)

    acc_ref[...] = jnp.zeros_like(acc_ref)
    neg = jnp.full((1, tq), -jnp.inf, F32)
    zero = jnp.zeros((1, tq), F32)
    group = s_ref.shape[0]

    def body(c, carry):
        blocks = []
        for g in range(group):
            start = pl.multiple_of((c * group + g) * tk, tk)
            blocks.append((k_ref[pl.ds(start, tk), :], vt_ref[c * group + g]))
        return update(carry, blocks)

    carry = lax.fori_loop(0, vt_ref.shape[0] // group, body, (neg, zero, neg, zero))
    tc = vct_ref.shape[2]
    carry = update(carry, [(kc_ref[c * tc:(c + 1) * tc, :], vct_ref[c])
                           for c in range(vct_ref.shape[0])])
    _, l0, _, l1 = carry

    lp = lam_ref[...]
    lam = (jnp.exp(jnp.sum(lp[0:1, :] * lp[1:2, :], axis=-1, keepdims=True))
           - jnp.exp(jnp.sum(lp[2:3, :] * lp[3:4, :], axis=-1, keepdims=True)) + lam_init)
    ot = acc_ref[0] * (1.0 / l0) - acc_ref[1] * (lam / l1)
    norm = lax.rsqrt(jnp.mean(ot * ot, axis=0, keepdims=True) + LN_EPS) * (1.0 - lam_init)
    o_ref[...] = (ot * norm * subln_ref[...]).T.astype(o_ref.dtype)


def _attn_call(qk, vt, kc, vct, lam_p, subln, *, n_batch, seq, ctx_len, heads, lam_init):
    d = qk.shape[1] // 2
    dv = d // heads
    tk = vt.shape[2]
    tc = vct.shape[2]
    nk = seq // tk
    nc = ctx_len // tc
    group = _pick(nk, (ATTN_GROUP, 1))
    assert nc <= group and tc <= tk, "context keys must fit one score-scratch group"
    tq = _pick(seq, (ATTN_TQ, 128))
    nq = seq // tq
    est = (2 * 2 * tq * dv * 2 + 2 * 2 * seq * dv * 2 + 2 * 2 * ctx_len * dv * 2
           + 2 * tq * dv * 4 + (2 * group + 6) * tq * tk * 4)
    kern = functools.partial(_attn_kernel, lam_init)
    return pl.pallas_call(
        kern,
        grid=(n_batch, heads, nq),
        in_specs=[pl.BlockSpec(lam_p.shape, lambda b, h, i: (0, 0)),
                  pl.BlockSpec((tq, dv), lambda b, h, i: (b * nq + i, h)),
                  pl.BlockSpec((seq, dv), lambda b, h, i: (b, heads + h)),
                  pl.BlockSpec((nk, dv, tk), lambda b, h, i: (b, h, 0)),
                  pl.BlockSpec((ctx_len, dv), lambda b, h, i: (b, h)),
                  pl.BlockSpec((nc, dv, tc), lambda b, h, i: (b, h, 0)),
                  pl.BlockSpec((dv, 1), lambda b, h, i: (0, 0))],
        out_specs=pl.BlockSpec((tq, dv), lambda b, h, i: (b * nq + i, h)),
        out_shape=jax.ShapeDtypeStruct((n_batch * seq, d), BF16),
        scratch_shapes=[pltpu.VMEM((2, dv, tq), F32), pltpu.VMEM((group, 2, tk, tq), F32)],
        compiler_params=_params(("parallel", "parallel", "arbitrary"), est),
        name="diff_attention",
    )(lam_p, qk, qk, vt, kc, vct, subln.reshape(dv, 1))


def _out_proj_kernel(k_gate, alpha, y_ref, h_ref, mods_ref, w_ref, g_ref, b_ref, o_ref):
    for rows in _row_blocks(h_ref.shape[0], PROJ_ROW_BLOCK):
        y = _dot(y_ref[rows, :], w_ref[...])
        o_ref[rows, :] = _deepnorm(h_ref[rows, :], _mod_row(mods_ref, k_gate), y, alpha,
                                   g_ref[...], b_ref[...])


def _out_proj_call(y, h, mods, w, lead, g, b, *, k_gate, alpha):
    n, d = h.shape
    nb = mods.shape[0]
    tm = _pick(n // nb, (512, 256, 128))
    n_tiles = n // tm
    tiles_per_mod = n_tiles // nb
    est = 2 * tm * d * 2 + 2 * 2 * tm * d * 4 + 2 * d * d * 2 + 4 * tm * d * 4
    return pl.pallas_call(
        functools.partial(_out_proj_kernel, k_gate, alpha),
        grid=(n_tiles,),
        in_specs=[pl.BlockSpec((tm, d), lambda i: (i, 0)),
                  pl.BlockSpec((tm, d), lambda i: (i, 0)),
                  pl.BlockSpec((1, N_MOD, d), lambda i: (i // tiles_per_mod, 0, 0)),
                  _lead_spec(lead, (d, d), lambda i: (0, 0)),
                  pl.BlockSpec((1, d), lambda i: (0, 0)),
                  pl.BlockSpec((1, d), lambda i: (0, 0))],
        out_specs=pl.BlockSpec((tm, d), lambda i: (i, 0)),
        out_shape=jax.ShapeDtypeStruct((n, d), F32),
        compiler_params=_params(("parallel",), est),
        name="out_proj_norm",
    )(y, h, mods, w, g.reshape(1, d), b.reshape(1, d))


def _rope_tables(seq, dk):
    n_freq = dk // 4
    t = jnp.arange(seq)
    inv = ROPE_BASE ** (-jnp.arange(n_freq, dtype=F32) / n_freq)
    ang_r = (t // GRID_W).astype(F32)[:, None] * inv
    ang_c = (t % GRID_W).astype(F32)[:, None] * inv
    cos_t = jnp.concatenate([jnp.cos(ang_r), jnp.cos(ang_c)] * 2, axis=-1)
    sin_t = jnp.concatenate([-jnp.sin(ang_r), -jnp.sin(ang_c), jnp.sin(ang_r), jnp.sin(ang_c)], axis=-1)
    k_tab = jnp.stack([cos_t, sin_t])
    return jnp.stack([k_tab * (dk ** -0.5 * math.log2(math.e)), k_tab])


def _rope_column_layout(w, dk):
    rows, width = w.shape
    w = w.reshape(rows, width // dk, 2, 2, dk // 4)
    return jnp.swapaxes(w, 2, 3).reshape(rows, width)


def kernel(x, c, ctx, c_ctx, ada_w, ada_b, ln_g, ln_b, ffn_wg, ffn_wu, ffn_wd, sc_w_in, sc_conv,
           sc_w_out, da_w_qkv, da_lambda, da_subln, da_w_o, gm_w_in, gm_ln_g, gm_ln_b, gm_w_s,
           gm_b_s, gm_w_out):
    n_batch, seq, d = x.shape
    ctx_len = ctx.shape[1]
    depth = ada_w.shape[0]
    mixer_of_layer = tuple(i % N_MIXERS for i in range(depth))
    last_ctx_layer = max([i for i in range(depth) if mixer_of_layer[i] == 1], default=-1)
    alpha = (2.0 * depth) ** 0.25
    dv = da_subln.shape[-1]
    heads = d // dv
    dk = dv // 2

    h = x.reshape(n_batch * seq, d)
    hc = ctx.reshape(n_batch * ctx_len, d)

    n_cond = n_batch + 1
    cond = jnp.zeros((16 * ((n_cond + 15) // 16), d), F32)
    cond = cond.at[:n_batch].set(c).at[n_batch].set(c_ctx)

    rope_tabs = _rope_tables(seq, dk)
    no_rope_tabs = jnp.zeros((1, 2, ctx_len, dk), F32)

    ffn_f32 = (ffn_wg, ffn_wu, ffn_wd)
    ffn_w = [tuple(w[0, 0].astype(BF16) for w in ffn_f32)]
    conv_w = (sc_w_in.astype(BF16), sc_conv, sc_w_out.astype(BF16))
    qkv_w = da_w_qkv.astype(BF16)
    qkv_w = jnp.concatenate(
        [_rope_column_layout(qkv_w[..., :2 * d].reshape(-1, 2 * d), dk).reshape(qkv_w.shape[0], d, 2 * d),
         qkv_w[..., 2 * d:]], axis=-1)
    attn_wo = da_w_o.astype(BF16)
    gmlp_w = (gm_w_in.astype(BF16), gm_w_s.astype(BF16), gm_b_s[..., None], gm_w_out.astype(BF16))

    for i in range(depth):
        kind = mixer_of_layer[i]
        j = i // N_MIXERS
        ctx_in = i <= last_ctx_layer
        ctx_out = i < last_ctx_layer
        mods_all = _ada(cond, ada_w, ada_b, i).reshape(-1, N_MOD, d)
        md = mods_all[:n_batch]
        mdc = mods_all[n_batch:n_batch + 1]

        def ffn_pair(h_lat, h_ctx, half, k0):
            nxt = (i, 1) if half == 0 else (i + 1, 0)
            args = dict(ln_g=ln_g[i, 2 * half], ln_b=ln_b[i, 2 * half], k_shift=k0,
                        k_gate=k0 + 2, res_scale=0.5, alpha=alpha)
            w_now = ffn_w[0]
            if nxt[0] < depth:
                h_lat, w_next = _fused_call("ffn", h_lat, md, w_now, (), cast=(ffn_f32, nxt), **args)
                ffn_w[0] = tuple(w_next)
            else:
                h_lat = _fused_call("ffn", h_lat, md, w_now, (), **args)
            if h_ctx is not None:
                h_ctx = _fused_call("ffn", h_ctx, mdc, w_now, (), **args)
            return h_lat, h_ctx

        h, hc_new = ffn_pair(h, hc if ctx_in else None, 0, 0)
        if ctx_in:
            hc = hc_new

        if kind == 0:
            conv = functools.partial(_fused_call, "conv", weights=conv_w, lead=(j,),
                                     ln_g=ln_g[i, 1], ln_b=ln_b[i, 1], k_shift=3, k_gate=5,
                                     res_scale=1.0, alpha=alpha)
            h = conv(h, md, period=GRID_W)
            if ctx_out:
                hc = conv(hc, mdc, period=ctx_len)
        elif kind == 1:
            lam_init = 0.8 - 0.6 * math.exp(-0.3 * i)
            qk, vt = _qkv_call(h, md, qkv_w, (j,), rope_tabs, k_shift=3, n_rope=2, seq=seq)
            kc, vct = _qkv_call(hc, mdc, qkv_w, (j,), no_rope_tabs, k_shift=3, n_rope=0,
                                seq=ctx_len, first_sec=1)
            o = _attn_call(qk, vt, kc, vct, da_lambda[j], da_subln[j], n_batch=n_batch, seq=seq,
                           ctx_len=ctx_len, heads=heads, lam_init=lam_init)
            h = _out_proj_call(o, h, md, attn_wo, (j,), ln_g[i, 1], ln_b[i, 1], k_gate=5,
                               alpha=alpha)
            assert not ctx_out, "context-side attention output is not implemented"
        else:
            def gmlp(hh, mm):
                vn = _gate_branch_call(hh, mm, gmlp_w[0], (j,), gm_ln_g[j], gm_ln_b[j], k_shift=3)
                return _fused_call("gmlp", hh, mm, gmlp_w, (j,), ln_g[i, 1], ln_b[i, 1],
                                   k_shift=3, k_gate=5, res_scale=1.0, alpha=alpha, vn=vn)

            h = gmlp(h, md)
            if ctx_out:
                hc = gmlp(hc, mdc)

        h, hc_new = ffn_pair(h, hc if ctx_out else None, 1, 6)
        if ctx_out:
            hc = hc_new
    return h.reshape(n_batch, seq, d)
```

```python
import functools
import math

import jax
import jax.numpy as jnp
from jax import lax
from jax.experimental import pallas as pl
from jax.experimental.pallas import tpu as pltpu

GRID_W = 64
CHUNK = 128
N_MOD = 9
N_MIXERS = 3
ROPE_BASE = 10000.0
LN_EPS = 1e-5

V7X_LANES = 128
V7X_VMEM_BYTES = 64 * 1024 * 1024
V7X_VMEM_CAP = V7X_VMEM_BYTES - 6 * 1024 * 1024

BF16 = jnp.bfloat16
F32 = jnp.float32

ATTN_TQ = 1024
ATTN_KEYS = 1024
ATTN_GROUP = 2
FUSED_CHUNK = 512
CAST_ROWS = 16
FUSED_TILE = {"ffn": 1024, "conv": 1024, "gmlp": 512}
FUSED_ROW_BLOCK = {"ffn": 1024, "conv": 512, "gmlp": 256}
PROJ_ROW_BLOCK = 512


def _params(semantics, vmem_estimate):
    limit = min(V7X_VMEM_CAP, max(32 * 1024 * 1024, int(vmem_estimate * 1.3)))
    return pltpu.CompilerParams(dimension_semantics=semantics, vmem_limit_bytes=limit)


def _pick(n, candidates):
    for c in candidates:
        if n % c == 0:
            return c
    return n


def _mod_row(mods_ref, k):
    return mods_ref[0, k:k + 1, :]


def _modulate_bf16(h, mods_ref, k_shift):
    shift = _mod_row(mods_ref, k_shift)
    scale = _mod_row(mods_ref, k_shift + 1)
    return (h * (1.0 + scale) + shift).astype(BF16)


def _layer_norm(x, g, b, eps=LN_EPS):
    mu = jnp.mean(x, axis=-1, keepdims=True)
    xc = x - mu
    var = jnp.mean(xc * xc, axis=-1, keepdims=True)
    return xc * lax.rsqrt(var + eps) * g + b


def _deepnorm(h, delta_scale, delta, alpha, g, b):
    return _layer_norm(h + (delta_scale * (1.0 / alpha)) * delta, g, b, LN_EPS / (alpha * alpha))


def _silu(x):
    return x / (1.0 + jnp.exp(-x))


def _gelu(x):
    return 0.5 * x * (1.0 + lax.erf(x * math.sqrt(0.5)))


def _dot(a, b):
    return jnp.dot(a, b, preferred_element_type=F32)


def _ada_kernel(cond_ref, w_ref, b_ref, o_ref):
    a = _silu(cond_ref[...]).astype(BF16)
    o_ref[...] = _dot(a, w_ref[...].astype(BF16)) + b_ref[...]


def _ada(cond, w, b, layer):
    m, d = cond.shape
    n = w.shape[2]
    tn = _pick(n, (1024, 512, 256, 128))
    est = 2 * d * tn * 4 + d * tn * 2 + 4 * m * (d + tn) * 4
    return pl.pallas_call(
        _ada_kernel,
        grid=(n // tn,),
        in_specs=[pl.BlockSpec((m, d), lambda j: (0, 0)),
                  pl.BlockSpec((None, d, tn), lambda j: (layer, 0, j)),
                  pl.BlockSpec((None, 1, tn), lambda j: (layer, 0, j))],
        out_specs=pl.BlockSpec((m, tn), lambda j: (0, j)),
        out_shape=jax.ShapeDtypeStruct((m, n), F32),
        compiler_params=_params(("parallel",), est),
        name="ada_mod",
    )(cond, w, b.reshape(b.shape[0], 1, n))


def _fused_kernel(kind, k_shift, k_gate, res_scale, alpha, period, n_steps, n_cast, *refs):
    h_ref, mods_ref = refs[0], refs[1]
    n_in = len(refs) - 2 - n_cast
    w = refs[2:n_in - 2 - n_cast]
    lng_ref, lnb_ref = refs[n_in - 2 - n_cast:n_in - n_cast]
    cast_in = refs[n_in - n_cast:n_in]
    o_ref = refs[n_in]
    cast_out = refs[n_in + 1:n_in + 1 + n_cast]
    a_ref = refs[-1]
    acc_ref = o_ref
    wd_ref = w[-1]
    tm = h_ref.shape[0]


    def row_blocks(size):
        rb = min(tm, max(size, period))
        return [slice(r, r + rb) for r in range(0, tm, rb)]

    def up(a, rows):
        if kind == "ffn":
            return _dot(a, w[0][...]), _dot(a, w[1][...])
        if kind == "conv":
            return _dot(a, w[0][...]), _dot(a, w[1][...]), _dot(a, w[2][...])
        vn_ref, ws_ref, bs_ref = w[1], w[2], w[3]
        gw = w[0].shape[1] // ws_ref.shape[0]
        cols = []
        for g in range(ws_ref.shape[0]):
            parts = [_dot(ws_ref[g], vn_ref[c:c + CHUNK, g * gw:(g + 1) * gw]) + bs_ref[g]
                     for c in range(rows.start, rows.stop, CHUNK)]
            cols.append(jnp.concatenate(parts, axis=0))
        return _dot(a, w[0][...]), jnp.concatenate(cols, axis=1)

    def hidden(pre):
        if kind == "ffn":
            g, u = pre
            return _silu(g) * u
        if kind == "conv":
            b, c, v = pre
            z = c * v
            pos = lax.broadcasted_iota(jnp.int32, z.shape, 0) & (period - 1)
            z_prev = jnp.where(pos == 0, 0.0, pltpu.roll(z, 1, 0))
            z_next = jnp.where(pos == period - 1, 0.0, pltpu.roll(z, z.shape[0] - 1, 0))
            cw = w[3][...]
            return b * (cw[0:1, :] * z_prev + cw[1:2, :] * z + cw[2:3, :] * z_next)
        u, s = pre
        return _gelu(u) * s

    def step(first, last):
        for src, dst in zip(cast_in, cast_out):
            dst[...] = src[...].astype(dst.dtype)

        def start(rows):
            if first:
                a = _modulate_bf16(h_ref[rows, :], mods_ref, k_shift)
                if not last:
                    a_ref[rows, :] = a
            else:
                a = a_ref[rows, :]
            return up(a, rows)

        def finish(rows, pre):
            acc = _dot(hidden(pre).astype(BF16), wd_ref[...])
            if not first:
                acc = acc_ref[rows, :] + acc
            if last:
                o_ref[rows, :] = _deepnorm(h_ref[rows, :], res_scale * _mod_row(mods_ref, k_gate),
                                           acc, alpha, lng_ref[...], lnb_ref[...])
            else:
                acc_ref[rows, :] = acc

        pending = None
        for rows in row_blocks(FUSED_ROW_BLOCK[kind]):
            pre = start(rows)
            if pending is not None:
                finish(*pending)
            pending = (rows, pre)
        finish(*pending)

    if n_steps == 1:
        step(True, True)
        return
    j = pl.program_id(1)
    pl.when(j == 0)(functools.partial(step, True, False))
    if n_steps > 2:
        pl.when(jnp.logical_and(j > 0, j < n_steps - 1))(functools.partial(step, False, False))
    pl.when(j == n_steps - 1)(functools.partial(step, False, True))


def _lead_spec(lead, block, index_fn):
    lead = tuple(lead)
    return pl.BlockSpec((None,) * len(lead) + tuple(block),
                        lambda *g: lead + tuple(index_fn(*g)))


def _fused_call(kind, h, mods, weights, lead, ln_g, ln_b, *, k_shift, k_gate, res_scale, alpha,
                period=GRID_W, vn=None, cast=None):
    n, d = h.shape
    nb = mods.shape[0]
    tm = _pick(n // nb, (FUSED_TILE[kind], 512, 256, 128))
    n_tiles = n // tm
    tiles_per_mod = n_tiles // nb

    row = lambda i, j: (i, 0)
    col_blk = lambda i, j: (0, j)
    row_blk = lambda i, j: (j, 0)
    common_in = [pl.BlockSpec((tm, d), row),
                 pl.BlockSpec((1, N_MOD, d), lambda i, j: (i // tiles_per_mod, 0, 0))]
    if kind == "ffn":
        wg, wu, wd = weights
        f = wg.shape[-1]
        tc = _pick(f, (FUSED_CHUNK, 256, 128))
        n_chunks = f // tc
        w_in = [_lead_spec(lead, (d, tc), col_blk), _lead_spec(lead, (d, tc), col_blk),
                _lead_spec(lead, (tc, d), row_blk)]
        w_args = [wg, wu, wd]
        w_bytes = 3 * d * tc * 2
    elif kind == "conv":
        assert period & (period - 1) == 0 and tm % period == 0, (tm, period)
        w_in3, cw, wd = weights
        tc = _pick(d, (FUSED_CHUNK, 256, 128))
        n_chunks = d // tc
        w_in = [_lead_spec(lead, (d, tc), col_blk),
                _lead_spec(lead, (d, tc), lambda i, j: (0, n_chunks + j)),
                _lead_spec(lead, (d, tc), lambda i, j: (0, 2 * n_chunks + j)),
                _lead_spec(lead, (3, tc), col_blk),
                _lead_spec(lead, (tc, d), row_blk)]
        w_args = [w_in3, w_in3, w_in3, cw, wd]
        w_bytes = 4 * d * tc * 2
    else:
        w_in2, ws, bs, wd = weights
        groups = ws.shape[-3]
        gw = wd.shape[-2] // groups
        per_step = max(g for g in range(1, groups + 1) if groups % g == 0 and g * gw <= 1024)
        tc = per_step * gw
        n_chunks = groups // per_step
        w_in = [_lead_spec(lead, (d, tc), col_blk),
                pl.BlockSpec((tm, tc), lambda i, j: (i, j)),
                _lead_spec(lead, (per_step, CHUNK, CHUNK), lambda i, j: (j, 0, 0)),
                _lead_spec(lead, (per_step, CHUNK, 1), lambda i, j: (j, 0, 0)),
                _lead_spec(lead, (tc, d), row_blk)]
        w_args = [w_in2, vn, ws, bs, wd]
        w_bytes = 2 * d * tc * 2 + tm * tc * 2
    vec = pl.BlockSpec((1, d), lambda i, j: (0, 0))
    rb = min(tm, FUSED_ROW_BLOCK[kind])
    est = (2 * 2 * tm * d * 4
           + tm * d * 2
           + 2 * w_bytes
           + 6 * rb * tc * 4
           + 2 * rb * d * 4)
    cast_arrays, cast_lead = cast if cast is not None else ((), ())
    n_steps = n_tiles * n_chunks
    cast_in, cast_out, cast_shapes = [], [], []
    for arr in cast_arrays:
        rows, cols = arr.shape[-2:]
        if (cols % (n_chunks * V7X_LANES) == 0 and rows % (n_tiles * CAST_ROWS) == 0):
            blk, idx = (rows // n_tiles, cols // n_chunks), (lambda i, j: (i, j))
        else:
            assert rows % (n_steps * CAST_ROWS) == 0, "cast array does not split over the grid"
            blk, idx = (rows // n_steps, cols), (lambda i, j: (i * n_chunks + j, 0))
        cast_in.append(_lead_spec(cast_lead, blk, idx))
        cast_out.append(pl.BlockSpec(blk, idx))
        cast_shapes.append(jax.ShapeDtypeStruct((rows, cols), BF16))
        est += 2 * blk[0] * blk[1] * (4 + 2)
    kern = functools.partial(_fused_kernel, kind, k_shift, k_gate, res_scale, alpha, period,
                             n_chunks, len(cast_arrays))
    outs = pl.pallas_call(
        kern,
        grid=(n_tiles, n_chunks),
        in_specs=common_in + w_in + [vec, vec] + cast_in,
        out_specs=[pl.BlockSpec((tm, d), row)] + cast_out,
        out_shape=[jax.ShapeDtypeStruct((n, d), F32)] + cast_shapes,
        scratch_shapes=[pltpu.VMEM((tm, d), BF16)],
        compiler_params=_params(("parallel", "arbitrary"), est),
        name="fused_" + kind,
    )(h, mods, *w_args, ln_g.reshape(1, d), ln_b.reshape(1, d), *cast_arrays)
    return outs[0] if cast is None else (outs[0], outs[1:])


def _qkv_kernel(k_shift, n_rope, h_ref, mods_ref, w_ref, tab_ref, o_ref, vt_ref, a_ref):
    s = pl.program_id(1)
    n_sec = pl.num_programs(1)
    cos_ref, sin_ref = tab_ref.at[0], tab_ref.at[1]

    @pl.when(s == 0)
    def _():
        a_ref[...] = _modulate_bf16(h_ref[...], mods_ref, k_shift)

    def project(rope, transposed):
        width = w_ref.shape[1]
        slab = _pick(width, (512, 256, 128))
        for c in range(width // slab):
            y = _dot(a_ref[...], w_ref[:, c * slab:(c + 1) * slab])
            if transposed:
                vt_ref[c * slab:(c + 1) * slab, :] = y.T.astype(vt_ref.dtype)
                continue
            for r in range(slab // V7X_LANES):
                yr = y[:, r * V7X_LANES:(r + 1) * V7X_LANES]
                if rope:
                    yr = yr * cos_ref[...] + pltpu.roll(yr, V7X_LANES // 2, 1) * sin_ref[...]
                lo = c * slab + r * V7X_LANES
                o_ref[:, lo:lo + V7X_LANES] = yr.astype(o_ref.dtype)

    if n_rope > 0:
        @pl.when(s < n_rope)
        def _():
            project(True, False)

    @pl.when(jnp.logical_and(s >= n_rope, s < n_sec - 1))
    def _():
        project(False, False)

    @pl.when(s == n_sec - 1)
    def _():
        project(False, True)


def _qkv_call(h, mods, w, lead, tabs, *, k_shift, n_rope, seq, first_sec=0):
    n, d = h.shape
    n_tab = tabs.shape[0]
    nb = mods.shape[0]
    n_sec = w.shape[-1] // d - first_sec
    tm = _pick(min(n // nb, seq), (512, 256, 128))
    n_tiles = n // tm
    tiles_per_mod = n_tiles // nb
    tiles_per_seq = seq // tm
    est = (2 * tm * d * 4 + tm * d * 2 + 2 * d * d * 2 + 2 * 2 * tm * d * 2
           + 8 * tm * V7X_LANES * 4 + 4 * tm * 512 * 4)
    kern = functools.partial(_qkv_kernel, k_shift, n_rope)
    return pl.pallas_call(
        kern,
        grid=(n_tiles, n_sec),
        in_specs=[pl.BlockSpec((tm, d), lambda i, s: (i, 0)),
                  pl.BlockSpec((1, N_MOD, d), lambda i, s: (i // tiles_per_mod, 0, 0)),
                  _lead_spec(lead, (d, d), lambda i, s: (0, first_sec + s)),
                  pl.BlockSpec((None, 2, tm, V7X_LANES),
                               lambda i, s: (jnp.minimum(s, n_tab - 1), 0, i % tiles_per_seq, 0))],
        out_specs=[pl.BlockSpec((tm, d), lambda i, s: (i, jnp.minimum(s, n_sec - 2))),
                   pl.BlockSpec((None, d, tm), lambda i, s: (i, 0, 0))],
        out_shape=[jax.ShapeDtypeStruct((n, (n_sec - 1) * d), BF16),
                   jax.ShapeDtypeStruct((n_tiles, d, tm), BF16)],
        scratch_shapes=[pltpu.VMEM((tm, d), BF16)],
        compiler_params=_params(("parallel", "arbitrary"), est),
        name="qkv_proj",
    )(h, mods, w, tabs)


def _row_blocks(n_rows, block):
    block = min(n_rows, block)
    return [slice(r, r + block) for r in range(0, n_rows, block)]


def _gate_branch_kernel(k_shift, h_ref, mods_ref, w_ref, g_ref, b_ref, o_ref):
    for rows in _row_blocks(h_ref.shape[0], PROJ_ROW_BLOCK):
        a = _modulate_bf16(h_ref[rows, :], mods_ref, k_shift)
        v = _gelu(_dot(a, w_ref[...]))
        o_ref[rows, :] = _layer_norm(v, g_ref[...], b_ref[...]).astype(o_ref.dtype)


def _gate_branch_call(h, mods, w, lead, g, b, *, k_shift):
    n, d = h.shape
    nb = mods.shape[0]
    width = w.shape[-1] // 2
    tm = _pick(n // nb, (512, 256, 128))
    n_tiles = n // tm
    tiles_per_mod = n_tiles // nb
    est = 2 * tm * d * 4 + 2 * d * width * 2 + 2 * tm * width * 2 + 4 * tm * width * 4
    return pl.pallas_call(
        functools.partial(_gate_branch_kernel, k_shift),
        grid=(n_tiles,),
        in_specs=[pl.BlockSpec((tm, d), lambda i: (i, 0)),
                  pl.BlockSpec((1, N_MOD, d), lambda i: (i // tiles_per_mod, 0, 0)),
                  _lead_spec(lead, (d, width), lambda i: (0, 1)),
                  pl.BlockSpec((1, width), lambda i: (0, 0)),
                  pl.BlockSpec((1, width), lambda i: (0, 0))],
        out_specs=pl.BlockSpec((tm, width), lambda i: (i, 0)),
        out_shape=jax.ShapeDtypeStruct((n, width), BF16),
        compiler_params=_params(("parallel",), est),
        name="gmlp_gate_branch",
    )(h, mods, w, g.reshape(1, width), b.reshape(1, width))


def _attn_kernel(lam_init, lam_ref, q_ref, k_ref, vt_ref, kc_ref, vct_ref, subln_ref, o_ref,
                 acc_ref, s_ref):
    dk = q_ref.shape[1] // 2
    tq = q_ref.shape[0]
    tk = vt_ref.shape[2]
    q = q_ref[...]
    qm = (q[:, :dk], q[:, dk:])
    nt = (((1,), (1,)), ((), ()))

    def update(carry, blocks):
        for g, (k_blk, _) in enumerate(blocks):
            for mp in range(2):
                s_ref[g, mp, 0:k_blk.shape[0], :] = lax.dot_general(
                    k_blk[:, mp * dk:(mp + 1) * dk], qm[mp], nt, preferred_element_type=F32)
        carry = list(carry)
        for g, (k_blk, vt_chunks) in enumerate(blocks):
            for mp in range(2):
                m_old, l_old = carry[2 * mp], carry[2 * mp + 1]
                st = s_ref[g, mp, 0:k_blk.shape[0], :]
                m_new = jnp.maximum(m_old, jnp.max(st, axis=0, keepdims=True))
                p = jnp.exp2(st - m_new)
                corr = jnp.exp2(m_old - m_new)
                carry[2 * mp] = m_new
                carry[2 * mp + 1] = corr * l_old + jnp.sum(p, axis=0, keepdims=True)
                pt = p.astype(BF16)
                acc = corr * acc_ref[mp]
                lo = 0
                for vt_c in vt_chunks:
                    acc = acc + _dot(vt_c, pt[lo:lo + vt_c.shape[1], :])
                    lo += vt_c.shape[1]
                acc_ref[mp] = acc
        return tuple(carry)

    acc_ref[...] = jnp.zeros_like(acc_ref)
    neg = jnp.full((1, tq), -jnp.inf, F32)
    zero = jnp.zeros((1, tq), F32)
    group, kb = s_ref.shape[0], s_ref.shape[2]
    chunks = kb // tk

    def body(c, carry):
        blocks = []
        for g in range(group):
            blk = c * group + g
            start = pl.multiple_of(blk * kb, kb)
            blocks.append((k_ref[pl.ds(start, kb), :],
                           [vt_ref[blk * chunks + i] for i in range(chunks)]))
        return update(carry, blocks)

    carry = lax.fori_loop(0, vt_ref.shape[0] // (group * chunks), body, (neg, zero, neg, zero))
    tc = vct_ref.shape[2]
    carry = update(carry, [(kc_ref[c * tc:(c + 1) * tc, :], [vct_ref[c]])
                           for c in range(vct_ref.shape[0])])
    _, l0, _, l1 = carry

    lp = lam_ref[...]
    lam = (jnp.exp(jnp.sum(lp[0:1, :] * lp[1:2, :], axis=-1, keepdims=True))
           - jnp.exp(jnp.sum(lp[2:3, :] * lp[3:4, :], axis=-1, keepdims=True)) + lam_init)
    ot = acc_ref[0] * (1.0 / l0) - acc_ref[1] * (lam / l1)
    norm = lax.rsqrt(jnp.mean(ot * ot, axis=0, keepdims=True) + LN_EPS) * (1.0 - lam_init)
    o_ref[...] = (ot * norm * subln_ref[...]).T.astype(o_ref.dtype)


def _attn_call(qk, vt, kc, vct, lam_p, subln, *, n_batch, seq, ctx_len, heads, lam_init):
    d = qk.shape[1] // 2
    dv = d // heads
    tk = vt.shape[2]
    tc = vct.shape[2]
    nk = seq // tk
    nc = ctx_len // tc
    kb = tk * _pick(nk, (ATTN_KEYS // tk, 1))
    group = _pick(seq // kb, (ATTN_GROUP, 1))
    assert nc <= group and tc <= kb, "context keys must fit one score-scratch group"
    tq = _pick(seq, (ATTN_TQ, 128))
    nq = seq // tq
    est = (2 * 2 * tq * dv * 2 + 2 * 2 * seq * dv * 2 + 2 * 2 * ctx_len * dv * 2
           + 2 * tq * dv * 4 + (2 * group + 3) * tq * kb * 4)
    kern = functools.partial(_attn_kernel, lam_init)
    return pl.pallas_call(
        kern,
        grid=(n_batch, heads, nq),
        in_specs=[pl.BlockSpec(lam_p.shape, lambda b, h, i: (0, 0)),
                  pl.BlockSpec((tq, dv), lambda b, h, i: (b * nq + i, h)),
                  pl.BlockSpec((seq, dv), lambda b, h, i: (b, heads + h)),
                  pl.BlockSpec((nk, dv, tk), lambda b, h, i: (b, h, 0)),
                  pl.BlockSpec((ctx_len, dv), lambda b, h, i: (b, h)),
                  pl.BlockSpec((nc, dv, tc), lambda b, h, i: (b, h, 0)),
                  pl.BlockSpec((dv, 1), lambda b, h, i: (0, 0))],
        out_specs=pl.BlockSpec((tq, dv), lambda b, h, i: (b * nq + i, h)),
        out_shape=jax.ShapeDtypeStruct((n_batch * seq, d), BF16),
        scratch_shapes=[pltpu.VMEM((2, dv, tq), F32), pltpu.VMEM((group, 2, kb, tq), F32)],
        compiler_params=_params(("parallel", "parallel", "arbitrary"), est),
        name="diff_attention",
    )(lam_p, qk, qk, vt, kc, vct, subln.reshape(dv, 1))


def _out_proj_kernel(k_gate, alpha, y_ref, h_ref, mods_ref, w_ref, g_ref, b_ref, o_ref):
    for rows in _row_blocks(h_ref.shape[0], PROJ_ROW_BLOCK):
        y = _dot(y_ref[rows, :], w_ref[...])
        o_ref[rows, :] = _deepnorm(h_ref[rows, :], _mod_row(mods_ref, k_gate), y, alpha,
                                   g_ref[...], b_ref[...])


def _out_proj_call(y, h, mods, w, lead, g, b, *, k_gate, alpha):
    n, d = h.shape
    nb = mods.shape[0]
    tm = _pick(n // nb, (512, 256, 128))
    n_tiles = n // tm
    tiles_per_mod = n_tiles // nb
    est = 2 * tm * d * 2 + 2 * 2 * tm * d * 4 + 2 * d * d * 2 + 4 * tm * d * 4
    return pl.pallas_call(
        functools.partial(_out_proj_kernel, k_gate, alpha),
        grid=(n_tiles,),
        in_specs=[pl.BlockSpec((tm, d), lambda i: (i, 0)),
                  pl.BlockSpec((tm, d), lambda i: (i, 0)),
                  pl.BlockSpec((1, N_MOD, d), lambda i: (i // tiles_per_mod, 0, 0)),
                  _lead_spec(lead, (d, d), lambda i: (0, 0)),
                  pl.BlockSpec((1, d), lambda i: (0, 0)),
                  pl.BlockSpec((1, d), lambda i: (0, 0))],
        out_specs=pl.BlockSpec((tm, d), lambda i: (i, 0)),
        out_shape=jax.ShapeDtypeStruct((n, d), F32),
        compiler_params=_params(("parallel",), est),
        name="out_proj_norm",
    )(y, h, mods, w, g.reshape(1, d), b.reshape(1, d))


def _rope_tables(seq, dk):
    n_freq = dk // 4
    t = jnp.arange(seq)
    inv = ROPE_BASE ** (-jnp.arange(n_freq, dtype=F32) / n_freq)
    ang_r = (t // GRID_W).astype(F32)[:, None] * inv
    ang_c = (t % GRID_W).astype(F32)[:, None] * inv
    cos_t = jnp.concatenate([jnp.cos(ang_r), jnp.cos(ang_c)] * 2, axis=-1)
    sin_t = jnp.concatenate([-jnp.sin(ang_r), -jnp.sin(ang_c), jnp.sin(ang_r), jnp.sin(ang_c)], axis=-1)
    k_tab = jnp.stack([cos_t, sin_t])
    return jnp.stack([k_tab * (dk ** -0.5 * math.log2(math.e)), k_tab])


def _rope_column_layout(w, dk):
    rows, width = w.shape
    w = w.reshape(rows, width // dk, 2, 2, dk // 4)
    return jnp.swapaxes(w, 2, 3).reshape(rows, width)


def kernel(x, c, ctx, c_ctx, ada_w, ada_b, ln_g, ln_b, ffn_wg, ffn_wu, ffn_wd, sc_w_in, sc_conv,
           sc_w_out, da_w_qkv, da_lambda, da_subln, da_w_o, gm_w_in, gm_ln_g, gm_ln_b, gm_w_s,
           gm_b_s, gm_w_out):
    n_batch, seq, d = x.shape
    ctx_len = ctx.shape[1]
    depth = ada_w.shape[0]
    mixer_of_layer = tuple(i % N_MIXERS for i in range(depth))
    last_ctx_layer = max([i for i in range(depth) if mixer_of_layer[i] == 1], default=-1)
    alpha = (2.0 * depth) ** 0.25
    dv = da_subln.shape[-1]
    heads = d // dv
    dk = dv // 2

    h = x.reshape(n_batch * seq, d)
    hc = ctx.reshape(n_batch * ctx_len, d)

    n_cond = n_batch + 1
    cond = jnp.zeros((16 * ((n_cond + 15) // 16), d), F32)
    cond = cond.at[:n_batch].set(c).at[n_batch].set(c_ctx)

    rope_tabs = _rope_tables(seq, dk)
    no_rope_tabs = jnp.zeros((1, 2, ctx_len, dk), F32)

    ffn_f32 = (ffn_wg, ffn_wu, ffn_wd)
    ffn_w = [tuple(w[0, 0].astype(BF16) for w in ffn_f32)]
    conv_w = (sc_w_in.astype(BF16), sc_conv, sc_w_out.astype(BF16))
    qkv_w = da_w_qkv.astype(BF16)
    qkv_w = jnp.concatenate(
        [_rope_column_layout(qkv_w[..., :2 * d].reshape(-1, 2 * d), dk).reshape(qkv_w.shape[0], d, 2 * d),
         qkv_w[..., 2 * d:]], axis=-1)
    attn_wo = da_w_o.astype(BF16)
    gmlp_w = (gm_w_in.astype(BF16), gm_w_s.astype(BF16), gm_b_s[..., None], gm_w_out.astype(BF16))

    for i in range(depth):
        kind = mixer_of_layer[i]
        j = i // N_MIXERS
        ctx_in = i <= last_ctx_layer
        ctx_out = i < last_ctx_layer
        mods_all = _ada(cond, ada_w, ada_b, i).reshape(-1, N_MOD, d)
        md = mods_all[:n_batch]
        mdc = mods_all[n_batch:n_batch + 1]

        def ffn_pair(h_lat, h_ctx, half, k0):
            nxt = (i, 1) if half == 0 else (i + 1, 0)
            args = dict(ln_g=ln_g[i, 2 * half], ln_b=ln_b[i, 2 * half], k_shift=k0,
                        k_gate=k0 + 2, res_scale=0.5, alpha=alpha)
            w_now = ffn_w[0]
            if nxt[0] < depth:
                h_lat, w_next = _fused_call("ffn", h_lat, md, w_now, (), cast=(ffn_f32, nxt), **args)
                ffn_w[0] = tuple(w_next)
            else:
                h_lat = _fused_call("ffn", h_lat, md, w_now, (), **args)
            if h_ctx is not None:
                h_ctx = _fused_call("ffn", h_ctx, mdc, w_now, (), **args)
            return h_lat, h_ctx

        h, hc_new = ffn_pair(h, hc if ctx_in else None, 0, 0)
        if ctx_in:
            hc = hc_new

        if kind == 0:
            conv = functools.partial(_fused_call, "conv", weights=conv_w, lead=(j,),
                                     ln_g=ln_g[i, 1], ln_b=ln_b[i, 1], k_shift=3, k_gate=5,
                                     res_scale=1.0, alpha=alpha)
            h = conv(h, md, period=GRID_W)
            if ctx_out:
                hc = conv(hc, mdc, period=ctx_len)
        elif kind == 1:
            lam_init = 0.8 - 0.6 * math.exp(-0.3 * i)
            qk, vt = _qkv_call(h, md, qkv_w, (j,), rope_tabs, k_shift=3, n_rope=2, seq=seq)
            kc, vct = _qkv_call(hc, mdc, qkv_w, (j,), no_rope_tabs, k_shift=3, n_rope=0,
                                seq=ctx_len, first_sec=1)
            o = _attn_call(qk, vt, kc, vct, da_lambda[j], da_subln[j], n_batch=n_batch, seq=seq,
                           ctx_len=ctx_len, heads=heads, lam_init=lam_init)
            h = _out_proj_call(o, h, md, attn_wo, (j,), ln_g[i, 1], ln_b[i, 1], k_gate=5,
                               alpha=alpha)
            assert not ctx_out, "context-side attention output is not implemented"
        else:
            def gmlp(hh, mm):
                vn = _gate_branch_call(hh, mm, gmlp_w[0], (j,), gm_ln_g[j], gm_ln_b[j], k_shift=3)
                return _fused_call("gmlp", hh, mm, gmlp_w, (j,), ln_g[i, 1], ln_b[i, 1],
                                   k_shift=3, k_gate=5, res_scale=1.0, alpha=alpha, vn=vn)

            h = gmlp(h, md)
            if ctx_out:
                hc = gmlp(hc, mdc)

        h, hc_new = ffn_pair(h, hc if ctx_out else None, 1, 6)
        if ctx_out:
            hc = hc_new
    return h.reshape(n_batch, seq, d)
```

```python
import functools
import math

import jax
import jax.numpy as jnp
from jax import lax
from jax.experimental import pallas as pl
from jax.experimental.pallas import tpu as pltpu

GRID_W = 64
CHUNK = 128
N_MOD = 9
N_MIXERS = 3
ROPE_BASE = 10000.0
LN_EPS = 1e-5

V7X_LANES = 128
V7X_VMEM_BYTES = 64 * 1024 * 1024
V7X_VMEM_CAP = V7X_VMEM_BYTES - 6 * 1024 * 1024

BF16 = jnp.bfloat16
F32 = jnp.float32

ATTN_TQ = 1024
ATTN_KEYS = 2048
ATTN_GROUP = 1
FUSED_CHUNK = 512
CAST_ROWS = 16
FUSED_TILE = {"ffn": 1024, "conv": 1024, "gmlp": 512}
FUSED_ROW_BLOCK = {"ffn": 1024, "conv": 512, "gmlp": 256}
PROJ_ROW_BLOCK = {"gate": 512, "out": 256}


def _params(semantics, vmem_estimate):
    limit = min(V7X_VMEM_CAP, max(32 * 1024 * 1024, int(vmem_estimate * 1.3)))
    return pltpu.CompilerParams(dimension_semantics=semantics, vmem_limit_bytes=limit)


def _pick(n, candidates):
    for c in candidates:
        if n % c == 0:
            return c
    return n


def _mod_row(mods_ref, k):
    return mods_ref[0, k:k + 1, :]


def _modulate_bf16(h, mods_ref, k_shift):
    shift = _mod_row(mods_ref, k_shift)
    scale = _mod_row(mods_ref, k_shift + 1)
    return (h * (1.0 + scale) + shift).astype(BF16)


def _layer_norm(x, g, b, eps=LN_EPS):
    mu = jnp.mean(x, axis=-1, keepdims=True)
    xc = x - mu
    var = jnp.mean(xc * xc, axis=-1, keepdims=True)
    return xc * lax.rsqrt(var + eps) * g + b


def _deepnorm(h, delta_scale, delta, alpha, g, b):
    return _layer_norm(h + (delta_scale * (1.0 / alpha)) * delta, g, b, LN_EPS / (alpha * alpha))


def _silu(x):
    return x / (1.0 + jnp.exp(-x))


def _gelu(x):
    return 0.5 * x * (1.0 + lax.erf(x * math.sqrt(0.5)))


def _dot(a, b):
    return jnp.dot(a, b, preferred_element_type=F32)


def _ada_kernel(cond_ref, w_ref, b_ref, o_ref):
    a = _silu(cond_ref[...]).astype(BF16)
    o_ref[...] = _dot(a, w_ref[...].astype(BF16)) + b_ref[...]


def _ada(cond, w, b, layer):
    m, d = cond.shape
    n = w.shape[2]
    tn = _pick(n, (1024, 512, 256, 128))
    est = 2 * d * tn * 4 + d * tn * 2 + 4 * m * (d + tn) * 4
    return pl.pallas_call(
        _ada_kernel,
        grid=(n // tn,),
        in_specs=[pl.BlockSpec((m, d), lambda j: (0, 0)),
                  pl.BlockSpec((None, d, tn), lambda j: (layer, 0, j)),
                  pl.BlockSpec((None, 1, tn), lambda j: (layer, 0, j))],
        out_specs=pl.BlockSpec((m, tn), lambda j: (0, j)),
        out_shape=jax.ShapeDtypeStruct((m, n), F32),
        compiler_params=_params(("parallel",), est),
        name="ada_mod",
    )(cond, w, b.reshape(b.shape[0], 1, n))


def _fused_kernel(kind, k_shift, k_gate, res_scale, alpha, period, n_steps, n_cast, *refs):
    h_ref, mods_ref = refs[0], refs[1]
    n_in = len(refs) - 2 - n_cast
    w = refs[2:n_in - 2 - n_cast]
    lng_ref, lnb_ref = refs[n_in - 2 - n_cast:n_in - n_cast]
    cast_in = refs[n_in - n_cast:n_in]
    o_ref = refs[n_in]
    cast_out = refs[n_in + 1:n_in + 1 + n_cast]
    a_ref = refs[-1]
    acc_ref = o_ref
    wd_ref = w[-1]
    tm = h_ref.shape[0]


    def row_blocks(size):
        rb = min(tm, max(size, period))
        return [slice(r, r + rb) for r in range(0, tm, rb)]

    def up(a, rows):
        if kind == "ffn":
            return _dot(a, w[0][...]), _dot(a, w[1][...])
        if kind == "conv":
            return _dot(a, w[0][...]), _dot(a, w[1][...]), _dot(a, w[2][...])
        vn_ref, ws_ref, bs_ref = w[1], w[2], w[3]
        gw = w[0].shape[1] // ws_ref.shape[0]
        cols = []
        for g in range(ws_ref.shape[0]):
            parts = [_dot(ws_ref[g], vn_ref[c:c + CHUNK, g * gw:(g + 1) * gw]) + bs_ref[g]
                     for c in range(rows.start, rows.stop, CHUNK)]
            cols.append(jnp.concatenate(parts, axis=0))
        return _dot(a, w[0][...]), jnp.concatenate(cols, axis=1)

    def hidden(pre):
        if kind == "ffn":
            g, u = pre
            return _silu(g) * u
        if kind == "conv":
            b, c, v = pre
            z = c * v
            pos = lax.broadcasted_iota(jnp.int32, z.shape, 0) & (period - 1)
            z_prev = jnp.where(pos == 0, 0.0, pltpu.roll(z, 1, 0))
            z_next = jnp.where(pos == period - 1, 0.0, pltpu.roll(z, z.shape[0] - 1, 0))
            cw = w[3][...]
            return b * (cw[0:1, :] * z_prev + cw[1:2, :] * z + cw[2:3, :] * z_next)
        u, s = pre
        return _gelu(u) * s

    def step(first, last):
        for src, dst in zip(cast_in, cast_out):
            dst[...] = src[...].astype(dst.dtype)

        def start(rows):
            if first:
                a = _modulate_bf16(h_ref[rows, :], mods_ref, k_shift)
                if not last:
                    a_ref[rows, :] = a
            else:
                a = a_ref[rows, :]
            return up(a, rows)

        def finish(rows, pre):
            acc = _dot(hidden(pre).astype(BF16), wd_ref[...])
            if not first:
                acc = acc_ref[rows, :] + acc
            if last:
                o_ref[rows, :] = _deepnorm(h_ref[rows, :], res_scale * _mod_row(mods_ref, k_gate),
                                           acc, alpha, lng_ref[...], lnb_ref[...])
            else:
                acc_ref[rows, :] = acc

        pending = None
        for rows in row_blocks(FUSED_ROW_BLOCK[kind]):
            pre = start(rows)
            if pending is not None:
                finish(*pending)
            pending = (rows, pre)
        finish(*pending)

    if n_steps == 1:
        step(True, True)
        return
    j = pl.program_id(1)
    pl.when(j == 0)(functools.partial(step, True, False))
    if n_steps > 2:
        pl.when(jnp.logical_and(j > 0, j < n_steps - 1))(functools.partial(step, False, False))
    pl.when(j == n_steps - 1)(functools.partial(step, False, True))


def _lead_spec(lead, block, index_fn):
    lead = tuple(lead)
    return pl.BlockSpec((None,) * len(lead) + tuple(block),
                        lambda *g: lead + tuple(index_fn(*g)))


def _fused_call(kind, h, mods, weights, lead, ln_g, ln_b, *, k_shift, k_gate, res_scale, alpha,
                period=GRID_W, vn=None, cast=None):
    n, d = h.shape
    nb = mods.shape[0]
    tm = _pick(n // nb, (FUSED_TILE[kind], 512, 256, 128))
    n_tiles = n // tm
    tiles_per_mod = n_tiles // nb

    row = lambda i, j: (i, 0)
    col_blk = lambda i, j: (0, j)
    row_blk = lambda i, j: (j, 0)
    common_in = [pl.BlockSpec((tm, d), row),
                 pl.BlockSpec((1, N_MOD, d), lambda i, j: (i // tiles_per_mod, 0, 0))]
    if kind == "ffn":
        wg, wu, wd = weights
        f = wg.shape[-1]
        tc = _pick(f, (FUSED_CHUNK, 256, 128))
        n_chunks = f // tc
        w_in = [_lead_spec(lead, (d, tc), col_blk), _lead_spec(lead, (d, tc), col_blk),
                _lead_spec(lead, (tc, d), row_blk)]
        w_args = [wg, wu, wd]
        w_bytes = 3 * d * tc * 2
    elif kind == "conv":
        assert period & (period - 1) == 0 and tm % period == 0, (tm, period)
        w_in3, cw, wd = weights
        tc = _pick(d, (FUSED_CHUNK, 256, 128))
        n_chunks = d // tc
        w_in = [_lead_spec(lead, (d, tc), col_blk),
                _lead_spec(lead, (d, tc), lambda i, j: (0, n_chunks + j)),
                _lead_spec(lead, (d, tc), lambda i, j: (0, 2 * n_chunks + j)),
                _lead_spec(lead, (3, tc), col_blk),
                _lead_spec(lead, (tc, d), row_blk)]
        w_args = [w_in3, w_in3, w_in3, cw, wd]
        w_bytes = 4 * d * tc * 2
    else:
        w_in2, ws, bs, wd = weights
        groups = ws.shape[-3]
        gw = wd.shape[-2] // groups
        per_step = max(g for g in range(1, groups + 1) if groups % g == 0 and g * gw <= 1024)
        tc = per_step * gw
        n_chunks = groups // per_step
        w_in = [_lead_spec(lead, (d, tc), col_blk),
                pl.BlockSpec((tm, tc), lambda i, j: (i, j)),
                _lead_spec(lead, (per_step, CHUNK, CHUNK), lambda i, j: (j, 0, 0)),
                _lead_spec(lead, (per_step, CHUNK, 1), lambda i, j: (j, 0, 0)),
                _lead_spec(lead, (tc, d), row_blk)]
        w_args = [w_in2, vn, ws, bs, wd]
        w_bytes = 2 * d * tc * 2 + tm * tc * 2
    vec = pl.BlockSpec((1, d), lambda i, j: (0, 0))
    rb = min(tm, FUSED_ROW_BLOCK[kind])
    est = (2 * 2 * tm * d * 4
           + tm * d * 2
           + 2 * w_bytes
           + 6 * rb * tc * 4
           + 2 * rb * d * 4)
    cast_arrays, cast_lead = cast if cast is not None else ((), ())
    n_steps = n_tiles * n_chunks
    cast_in, cast_out, cast_shapes = [], [], []
    for arr in cast_arrays:
        rows, cols = arr.shape[-2:]
        if (cols % (n_chunks * V7X_LANES) == 0 and rows % (n_tiles * CAST_ROWS) == 0):
            blk, idx = (rows // n_tiles, cols // n_chunks), (lambda i, j: (i, j))
        else:
            assert rows % (n_steps * CAST_ROWS) == 0, "cast array does not split over the grid"
            blk, idx = (rows // n_steps, cols), (lambda i, j: (i * n_chunks + j, 0))
        cast_in.append(_lead_spec(cast_lead, blk, idx))
        cast_out.append(pl.BlockSpec(blk, idx))
        cast_shapes.append(jax.ShapeDtypeStruct((rows, cols), BF16))
        est += 2 * blk[0] * blk[1] * (4 + 2)
    kern = functools.partial(_fused_kernel, kind, k_shift, k_gate, res_scale, alpha, period,
                             n_chunks, len(cast_arrays))
    outs = pl.pallas_call(
        kern,
        grid=(n_tiles, n_chunks),
        in_specs=common_in + w_in + [vec, vec] + cast_in,
        out_specs=[pl.BlockSpec((tm, d), row)] + cast_out,
        out_shape=[jax.ShapeDtypeStruct((n, d), F32)] + cast_shapes,
        scratch_shapes=[pltpu.VMEM((tm, d), BF16)],
        compiler_params=_params(("parallel", "arbitrary"), est),
        name="fused_" + kind,
    )(h, mods, *w_args, ln_g.reshape(1, d), ln_b.reshape(1, d), *cast_arrays)
    return outs[0] if cast is None else (outs[0], outs[1:])


def _qkv_kernel(k_shift, n_rope, h_ref, mods_ref, w_ref, tab_ref, o_ref, vt_ref, a_ref):
    s = pl.program_id(1)
    n_sec = pl.num_programs(1)
    cos_ref, sin_ref = tab_ref.at[0], tab_ref.at[1]

    @pl.when(s == 0)
    def _():
        a_ref[...] = _modulate_bf16(h_ref[...], mods_ref, k_shift)

    def project(rope, transposed):
        width = w_ref.shape[1]
        slab = _pick(width, (512, 256, 128))
        for c in range(width // slab):
            y = _dot(a_ref[...], w_ref[:, c * slab:(c + 1) * slab])
            if transposed:
                vt_ref[c * slab:(c + 1) * slab, :] = y.T.astype(vt_ref.dtype)
                continue
            for r in range(slab // V7X_LANES):
                yr = y[:, r * V7X_LANES:(r + 1) * V7X_LANES]
                if rope:
                    yr = yr * cos_ref[...] + pltpu.roll(yr, V7X_LANES // 2, 1) * sin_ref[...]
                lo = c * slab + r * V7X_LANES
                o_ref[:, lo:lo + V7X_LANES] = yr.astype(o_ref.dtype)

    if n_rope > 0:
        @pl.when(s < n_rope)
        def _():
            project(True, False)

    @pl.when(jnp.logical_and(s >= n_rope, s < n_sec - 1))
    def _():
        project(False, False)

    @pl.when(s == n_sec - 1)
    def _():
        project(False, True)


def _qkv_call(h, mods, w, lead, tabs, *, k_shift, n_rope, seq, first_sec=0):
    n, d = h.shape
    n_tab = tabs.shape[0]
    nb = mods.shape[0]
    n_sec = w.shape[-1] // d - first_sec
    tm = _pick(min(n // nb, seq), (512, 256, 128))
    n_tiles = n // tm
    tiles_per_mod = n_tiles // nb
    tiles_per_seq = seq // tm
    est = (2 * tm * d * 4 + tm * d * 2 + 2 * d * d * 2 + 2 * 2 * tm * d * 2
           + 8 * tm * V7X_LANES * 4 + 4 * tm * 512 * 4)
    kern = functools.partial(_qkv_kernel, k_shift, n_rope)
    return pl.pallas_call(
        kern,
        grid=(n_tiles, n_sec),
        in_specs=[pl.BlockSpec((tm, d), lambda i, s: (i, 0)),
                  pl.BlockSpec((1, N_MOD, d), lambda i, s: (i // tiles_per_mod, 0, 0)),
                  _lead_spec(lead, (d, d), lambda i, s: (0, first_sec + s)),
                  pl.BlockSpec((None, 2, tm, V7X_LANES),
                               lambda i, s: (jnp.minimum(s, n_tab - 1), 0, i % tiles_per_seq, 0))],
        out_specs=[pl.BlockSpec((tm, d), lambda i, s: (i, jnp.minimum(s, n_sec - 2))),
                   pl.BlockSpec((None, d, tm), lambda i, s: (i, 0, 0))],
        out_shape=[jax.ShapeDtypeStruct((n, (n_sec - 1) * d), BF16),
                   jax.ShapeDtypeStruct((n_tiles, d, tm), BF16)],
        scratch_shapes=[pltpu.VMEM((tm, d), BF16)],
        compiler_params=_params(("parallel", "arbitrary"), est),
        name="qkv_proj",
    )(h, mods, w, tabs)


def _row_blocks(n_rows, block):
    block = min(n_rows, block)
    return [slice(r, r + block) for r in range(0, n_rows, block)]


def _gate_branch_kernel(k_shift, h_ref, mods_ref, w_ref, g_ref, b_ref, o_ref):
    for rows in _row_blocks(h_ref.shape[0], PROJ_ROW_BLOCK["gate"]):
        a = _modulate_bf16(h_ref[rows, :], mods_ref, k_shift)
        v = _gelu(_dot(a, w_ref[...]))
        o_ref[rows, :] = _layer_norm(v, g_ref[...], b_ref[...]).astype(o_ref.dtype)


def _gate_branch_call(h, mods, w, lead, g, b, *, k_shift):
    n, d = h.shape
    nb = mods.shape[0]
    width = w.shape[-1] // 2
    tm = _pick(n // nb, (512, 256, 128))
    n_tiles = n // tm
    tiles_per_mod = n_tiles // nb
    est = 2 * tm * d * 4 + 2 * d * width * 2 + 2 * tm * width * 2 + 4 * tm * width * 4
    return pl.pallas_call(
        functools.partial(_gate_branch_kernel, k_shift),
        grid=(n_tiles,),
        in_specs=[pl.BlockSpec((tm, d), lambda i: (i, 0)),
                  pl.BlockSpec((1, N_MOD, d), lambda i: (i // tiles_per_mod, 0, 0)),
                  _lead_spec(lead, (d, width), lambda i: (0, 1)),
                  pl.BlockSpec((1, width), lambda i: (0, 0)),
                  pl.BlockSpec((1, width), lambda i: (0, 0))],
        out_specs=pl.BlockSpec((tm, width), lambda i: (i, 0)),
        out_shape=jax.ShapeDtypeStruct((n, width), BF16),
        compiler_params=_params(("parallel",), est),
        name="gmlp_gate_branch",
    )(h, mods, w, g.reshape(1, width), b.reshape(1, width))


def _attn_kernel(lam_init, lam_ref, q_ref, k_ref, vt_ref, kc_ref, vct_ref, subln_ref, o_ref,
                 acc_ref, s_ref):
    dk = q_ref.shape[1] // 2
    tq = q_ref.shape[0]
    tk = vt_ref.shape[2]
    q = q_ref[...]
    qm = (q[:, :dk], q[:, dk:])
    nt = (((1,), (1,)), ((), ()))

    def update(carry, blocks):
        for g, (k_blk, _) in enumerate(blocks):
            for mp in range(2):
                s_ref[g, mp, 0:k_blk.shape[0], :] = lax.dot_general(
                    k_blk[:, mp * dk:(mp + 1) * dk], qm[mp], nt, preferred_element_type=F32)
        carry = list(carry)
        for g, (k_blk, vt_chunks) in enumerate(blocks):
            for mp in range(2):
                m_old, l_old = carry[2 * mp], carry[2 * mp + 1]
                st = s_ref[g, mp, 0:k_blk.shape[0], :]
                m_new = jnp.maximum(m_old, jnp.max(st, axis=0, keepdims=True))
                p = jnp.exp2(st - m_new)
                corr = jnp.exp2(m_old - m_new)
                carry[2 * mp] = m_new
                carry[2 * mp + 1] = corr * l_old + jnp.sum(p, axis=0, keepdims=True)
                pt = p.astype(BF16)
                acc = corr * acc_ref[mp]
                lo = 0
                for vt_c in vt_chunks:
                    acc = acc + _dot(vt_c, pt[lo:lo + vt_c.shape[1], :])
                    lo += vt_c.shape[1]
                acc_ref[mp] = acc
        return tuple(carry)

    acc_ref[...] = jnp.zeros_like(acc_ref)
    neg = jnp.full((1, tq), -jnp.inf, F32)
    zero = jnp.zeros((1, tq), F32)
    group, kb = s_ref.shape[0], s_ref.shape[2]
    chunks = kb // tk

    def body(c, carry):
        blocks = []
        for g in range(group):
            blk = c * group + g
            start = pl.multiple_of(blk * kb, kb)
            blocks.append((k_ref[pl.ds(start, kb), :],
                           [vt_ref[blk * chunks + i] for i in range(chunks)]))
        return update(carry, blocks)

    carry = lax.fori_loop(0, vt_ref.shape[0] // (group * chunks), body, (neg, zero, neg, zero))
    tc = vct_ref.shape[2]
    carry = update(carry, [(kc_ref[c * tc:(c + 1) * tc, :], [vct_ref[c]])
                           for c in range(vct_ref.shape[0])])
    _, l0, _, l1 = carry

    lp = lam_ref[...]
    lam = (jnp.exp(jnp.sum(lp[0:1, :] * lp[1:2, :], axis=-1, keepdims=True))
           - jnp.exp(jnp.sum(lp[2:3, :] * lp[3:4, :], axis=-1, keepdims=True)) + lam_init)
    ot = acc_ref[0] * (1.0 / l0) - acc_ref[1] * (lam / l1)
    norm = lax.rsqrt(jnp.mean(ot * ot, axis=0, keepdims=True) + LN_EPS) * (1.0 - lam_init)
    o_ref[...] = (ot * norm * subln_ref[...]).T.astype(o_ref.dtype)


def _attn_call(qk, vt, kc, vct, lam_p, subln, *, n_batch, seq, ctx_len, heads, lam_init):
    d = qk.shape[1] // 2
    dv = d // heads
    tk = vt.shape[2]
    tc = vct.shape[2]
    nk = seq // tk
    nc = ctx_len // tc
    kb = tk * _pick(nk, (ATTN_KEYS // tk, 1))
    group = _pick(seq // kb, (ATTN_GROUP, 1))
    assert nc <= group and tc <= kb, "context keys must fit one score-scratch group"
    tq = _pick(seq, (ATTN_TQ, 128))
    nq = seq // tq
    est = (2 * 2 * tq * dv * 2 + 2 * 2 * seq * dv * 2 + 2 * 2 * ctx_len * dv * 2
           + 2 * tq * dv * 4 + (2 * group + 3) * tq * kb * 4)
    kern = functools.partial(_attn_kernel, lam_init)
    return pl.pallas_call(
        kern,
        grid=(n_batch, heads, nq),
        in_specs=[pl.BlockSpec(lam_p.shape, lambda b, h, i: (0, 0)),
                  pl.BlockSpec((tq, dv), lambda b, h, i: (b * nq + i, h)),
                  pl.BlockSpec((seq, dv), lambda b, h, i: (b, heads + h)),
                  pl.BlockSpec((nk, dv, tk), lambda b, h, i: (b, h, 0)),
                  pl.BlockSpec((ctx_len, dv), lambda b, h, i: (b, h)),
                  pl.BlockSpec((nc, dv, tc), lambda b, h, i: (b, h, 0)),
                  pl.BlockSpec((dv, 1), lambda b, h, i: (0, 0))],
        out_specs=pl.BlockSpec((tq, dv), lambda b, h, i: (b * nq + i, h)),
        out_shape=jax.ShapeDtypeStruct((n_batch * seq, d), BF16),
        scratch_shapes=[pltpu.VMEM((2, dv, tq), F32), pltpu.VMEM((group, 2, kb, tq), F32)],
        compiler_params=_params(("parallel", "parallel", "arbitrary"), est),
        name="diff_attention",
    )(lam_p, qk, qk, vt, kc, vct, subln.reshape(dv, 1))


def _out_proj_kernel(k_gate, alpha, y_ref, h_ref, mods_ref, w_ref, g_ref, b_ref, o_ref):
    for rows in _row_blocks(h_ref.shape[0], PROJ_ROW_BLOCK["out"]):
        y = _dot(y_ref[rows, :], w_ref[...])
        o_ref[rows, :] = _deepnorm(h_ref[rows, :], _mod_row(mods_ref, k_gate), y, alpha,
                                   g_ref[...], b_ref[...])


def _out_proj_call(y, h, mods, w, lead, g, b, *, k_gate, alpha):
    n, d = h.shape
    nb = mods.shape[0]
    tm = _pick(n // nb, (512, 256, 128))
    n_tiles = n // tm
    tiles_per_mod = n_tiles // nb
    est = 2 * tm * d * 2 + 2 * 2 * tm * d * 4 + 2 * d * d * 2 + 4 * tm * d * 4
    return pl.pallas_call(
        functools.partial(_out_proj_kernel, k_gate, alpha),
        grid=(n_tiles,),
        in_specs=[pl.BlockSpec((tm, d), lambda i: (i, 0)),
                  pl.BlockSpec((tm, d), lambda i: (i, 0)),
                  pl.BlockSpec((1, N_MOD, d), lambda i: (i // tiles_per_mod, 0, 0)),
                  _lead_spec(lead, (d, d), lambda i: (0, 0)),
                  pl.BlockSpec((1, d), lambda i: (0, 0)),
                  pl.BlockSpec((1, d), lambda i: (0, 0))],
        out_specs=pl.BlockSpec((tm, d), lambda i: (i, 0)),
        out_shape=jax.ShapeDtypeStruct((n, d), F32),
        compiler_params=_params(("parallel",), est),
        name="out_proj_norm",
    )(y, h, mods, w, g.reshape(1, d), b.reshape(1, d))


def _rope_tables(seq, dk):
    n_freq = dk // 4
    t = jnp.arange(seq)
    inv = ROPE_BASE ** (-jnp.arange(n_freq, dtype=F32) / n_freq)
    ang_r = (t // GRID_W).astype(F32)[:, None] * inv
    ang_c = (t % GRID_W).astype(F32)[:, None] * inv
    cos_t = jnp.concatenate([jnp.cos(ang_r), jnp.cos(ang_c)] * 2, axis=-1)
    sin_t = jnp.concatenate([-jnp.sin(ang_r), -jnp.sin(ang_c), jnp.sin(ang_r), jnp.sin(ang_c)], axis=-1)
    k_tab = jnp.stack([cos_t, sin_t])
    return jnp.stack([k_tab * (dk ** -0.5 * math.log2(math.e)), k_tab])


def _rope_column_layout(w, dk):
    rows, width = w.shape
    w = w.reshape(rows, width // dk, 2, 2, dk // 4)
    return jnp.swapaxes(w, 2, 3).reshape(rows, width)


def kernel(x, c, ctx, c_ctx, ada_w, ada_b, ln_g, ln_b, ffn_wg, ffn_wu, ffn_wd, sc_w_in, sc_conv,
           sc_w_out, da_w_qkv, da_lambda, da_subln, da_w_o, gm_w_in, gm_ln_g, gm_ln_b, gm_w_s,
           gm_b_s, gm_w_out):
    n_batch, seq, d = x.shape
    ctx_len = ctx.shape[1]
    depth = ada_w.shape[0]
    mixer_of_layer = tuple(i % N_MIXERS for i in range(depth))
    last_ctx_layer = max([i for i in range(depth) if mixer_of_layer[i] == 1], default=-1)
    alpha = (2.0 * depth) ** 0.25
    dv = da_subln.shape[-1]
    heads = d // dv
    dk = dv // 2

    h = x.reshape(n_batch * seq, d)
    hc = ctx.reshape(n_batch * ctx_len, d)

    n_cond = n_batch + 1
    cond = jnp.zeros((16 * ((n_cond + 15) // 16), d), F32)
    cond = cond.at[:n_batch].set(c).at[n_batch].set(c_ctx)

    rope_tabs = _rope_tables(seq, dk)
    no_rope_tabs = jnp.zeros((1, 2, ctx_len, dk), F32)

    ffn_f32 = (ffn_wg, ffn_wu, ffn_wd)
    ffn_w = [tuple(w[0, 0].astype(BF16) for w in ffn_f32)]
    conv_w = (sc_w_in.astype(BF16), sc_conv, sc_w_out.astype(BF16))
    qkv_w = da_w_qkv.astype(BF16)
    qkv_w = jnp.concatenate(
        [_rope_column_layout(qkv_w[..., :2 * d].reshape(-1, 2 * d), dk).reshape(qkv_w.shape[0], d, 2 * d),
         qkv_w[..., 2 * d:]], axis=-1)
    attn_wo = da_w_o.astype(BF16)
    gmlp_w = (gm_w_in.astype(BF16), gm_w_s.astype(BF16), gm_b_s[..., None], gm_w_out.astype(BF16))

    for i in range(depth):
        kind = mixer_of_layer[i]
        j = i // N_MIXERS
        ctx_in = i <= last_ctx_layer
        ctx_out = i < last_ctx_layer
        mods_all = _ada(cond, ada_w, ada_b, i).reshape(-1, N_MOD, d)
        md = mods_all[:n_batch]
        mdc = mods_all[n_batch:n_batch + 1]

        def ffn_pair(h_lat, h_ctx, half, k0):
            nxt = (i, 1) if half == 0 else (i + 1, 0)
            args = dict(ln_g=ln_g[i, 2 * half], ln_b=ln_b[i, 2 * half], k_shift=k0,
                        k_gate=k0 + 2, res_scale=0.5, alpha=alpha)
            w_now = ffn_w[0]
            if nxt[0] < depth:
                h_lat, w_next = _fused_call("ffn", h_lat, md, w_now, (), cast=(ffn_f32, nxt), **args)
                ffn_w[0] = tuple(w_next)
            else:
                h_lat = _fused_call("ffn", h_lat, md, w_now, (), **args)
            if h_ctx is not None:
                h_ctx = _fused_call("ffn", h_ctx, mdc, w_now, (), **args)
            return h_lat, h_ctx

        h, hc_new = ffn_pair(h, hc if ctx_in else None, 0, 0)
        if ctx_in:
            hc = hc_new

        if kind == 0:
            conv = functools.partial(_fused_call, "conv", weights=conv_w, lead=(j,),
                                     ln_g=ln_g[i, 1], ln_b=ln_b[i, 1], k_shift=3, k_gate=5,
                                     res_scale=1.0, alpha=alpha)
            h = conv(h, md, period=GRID_W)
            if ctx_out:
                hc = conv(hc, mdc, period=ctx_len)
        elif kind == 1:
            lam_init = 0.8 - 0.6 * math.exp(-0.3 * i)
            qk, vt = _qkv_call(h, md, qkv_w, (j,), rope_tabs, k_shift=3, n_rope=2, seq=seq)
            kc, vct = _qkv_call(hc, mdc, qkv_w, (j,), no_rope_tabs, k_shift=3, n_rope=0,
                                seq=ctx_len, first_sec=1)
            o = _attn_call(qk, vt, kc, vct, da_lambda[j], da_subln[j], n_batch=n_batch, seq=seq,
                           ctx_len=ctx_len, heads=heads, lam_init=lam_init)
            h = _out_proj_call(o, h, md, attn_wo, (j,), ln_g[i, 1], ln_b[i, 1], k_gate=5,
                               alpha=alpha)
            assert not ctx_out, "context-side attention output is not implemented"
        else:
            def gmlp(hh, mm):
                vn = _gate_branch_call(hh, mm, gmlp_w[0], (j,), gm_ln_g[j], gm_ln_b[j], k_shift=3)
                return _fused_call("gmlp", hh, mm, gmlp_w, (j,), ln_g[i, 1], ln_b[i, 1],
                                   k_shift=3, k_gate=5, res_scale=1.0, alpha=alpha, vn=vn)

            h = gmlp(h, md)
            if ctx_out:
                hc = gmlp(hc, mdc)

        h, hc_new = ffn_pair(h, hc if ctx_out else None, 1, 6)
        if ctx_out:
            hc = hc_new
    return h.reshape(n_batch, seq, d)
```

```python
import functools
import math

import jax
import jax.numpy as jnp
from jax import lax
from jax.experimental import pallas as pl
from jax.experimental.pallas import tpu as pltpu

GRID_W = 64
CHUNK = 128
N_MOD = 9
N_MIXERS = 3
ROPE_BASE = 10000.0
LN_EPS = 1e-5

V7X_LANES = 128
V7X_VMEM_BYTES = 64 * 1024 * 1024
V7X_VMEM_CAP = V7X_VMEM_BYTES - 6 * 1024 * 1024
V7X_VMEM_DEFAULT_LIMIT = 32 * 1024 * 1024
VMEM_ESTIMATE_MARGIN = 1.3

BF16 = jnp.bfloat16
F32 = jnp.float32

ATTN_TQ = 1024
ATTN_KEYS = 2048
ATTN_GROUP = 1
FUSED_CHUNK = 512
CAST_ROWS = 16
FUSED_TILE = {"ffn": 1024, "conv": 1024, "gmlp": 512}
FUSED_ROW_BLOCK = {"ffn": 1024, "conv": 512, "gmlp": 256}
PROJ_ROW_BLOCK = {"gate": 512, "out": 256}


def _params(semantics, vmem_estimate):
    limit = min(V7X_VMEM_CAP, max(V7X_VMEM_DEFAULT_LIMIT, int(vmem_estimate * VMEM_ESTIMATE_MARGIN)))
    return pltpu.CompilerParams(dimension_semantics=semantics, vmem_limit_bytes=limit)


def _pick(n, candidates):
    for c in candidates:
        if n % c == 0:
            return c
    return n


def _mod_row(mods_ref, k):
    return mods_ref[0, k:k + 1, :]


def _modulate_bf16(h, mods_ref, k_shift):
    shift = _mod_row(mods_ref, k_shift)
    scale = _mod_row(mods_ref, k_shift + 1)
    return (h * (1.0 + scale) + shift).astype(BF16)


def _layer_norm(x, g, b, eps=LN_EPS):
    mu = jnp.mean(x, axis=-1, keepdims=True)
    xc = x - mu
    var = jnp.mean(xc * xc, axis=-1, keepdims=True)
    return xc * lax.rsqrt(var + eps) * g + b


def _deepnorm(h, delta_scale, delta, alpha, g, b):
    return _layer_norm(h + (delta_scale * (1.0 / alpha)) * delta, g, b, LN_EPS / (alpha * alpha))


def _silu(x):
    return x / (1.0 + jnp.exp(-x))


def _gelu(x):
    return 0.5 * x * (1.0 + lax.erf(x * math.sqrt(0.5)))


def _dot(a, b):
    return jnp.dot(a, b, preferred_element_type=F32)


def _ada_kernel(cond_ref, w_ref, b_ref, o_ref):
    a = _silu(cond_ref[...]).astype(BF16)
    o_ref[...] = _dot(a, w_ref[...].astype(BF16)) + b_ref[...]


def _ada(cond, w, b, layer):
    m, d = cond.shape
    n = w.shape[2]
    tn = _pick(n, (1024, 512, 256, 128))
    est = 2 * d * tn * 4 + d * tn * 2 + 4 * m * (d + tn) * 4
    return pl.pallas_call(
        _ada_kernel,
        grid=(n // tn,),
        in_specs=[pl.BlockSpec((m, d), lambda j: (0, 0)),
                  pl.BlockSpec((None, d, tn), lambda j: (layer, 0, j)),
                  pl.BlockSpec((None, 1, tn), lambda j: (layer, 0, j))],
        out_specs=pl.BlockSpec((m, tn), lambda j: (0, j)),
        out_shape=jax.ShapeDtypeStruct((m, n), F32),
        compiler_params=_params(("parallel",), est),
        name="ada_mod",
    )(cond, w, b.reshape(b.shape[0], 1, n))


def _fused_kernel(kind, k_shift, k_gate, res_scale, alpha, period, n_steps, n_cast, *refs):
    h_ref, mods_ref = refs[0], refs[1]
    n_in = len(refs) - 2 - n_cast
    w = refs[2:n_in - 2 - n_cast]
    lng_ref, lnb_ref = refs[n_in - 2 - n_cast:n_in - n_cast]
    cast_in = refs[n_in - n_cast:n_in]
    o_ref = refs[n_in]
    cast_out = refs[n_in + 1:n_in + 1 + n_cast]
    a_ref = refs[-1]
    acc_ref = o_ref
    wd_ref = w[-1]
    tm = h_ref.shape[0]


    def row_blocks(size):
        rb = min(tm, max(size, period))
        return [slice(r, r + rb) for r in range(0, tm, rb)]

    def up(a, rows):
        if kind == "ffn":
            return _dot(a, w[0][...]), _dot(a, w[1][...])
        if kind == "conv":
            return _dot(a, w[0][...]), _dot(a, w[1][...]), _dot(a, w[2][...])
        vn_ref, ws_ref, bs_ref = w[1], w[2], w[3]
        gw = w[0].shape[1] // ws_ref.shape[0]
        cols = []
        for g in range(ws_ref.shape[0]):
            parts = [_dot(ws_ref[g], vn_ref[c:c + CHUNK, g * gw:(g + 1) * gw]) + bs_ref[g]
                     for c in range(rows.start, rows.stop, CHUNK)]
            cols.append(jnp.concatenate(parts, axis=0))
        return _dot(a, w[0][...]), jnp.concatenate(cols, axis=1)

    def hidden(pre):
        if kind == "ffn":
            g, u = pre
            return _silu(g) * u
        if kind == "conv":
            b, c, v = pre
            z = c * v
            pos = lax.broadcasted_iota(jnp.int32, z.shape, 0) & (period - 1)
            z_prev = jnp.where(pos == 0, 0.0, pltpu.roll(z, 1, 0))
            z_next = jnp.where(pos == period - 1, 0.0, pltpu.roll(z, z.shape[0] - 1, 0))
            cw = w[3][...]
            return b * (cw[0:1, :] * z_prev + cw[1:2, :] * z + cw[2:3, :] * z_next)
        u, s = pre
        return _gelu(u) * s

    def step(first, last):
        for src, dst in zip(cast_in, cast_out):
            dst[...] = src[...].astype(dst.dtype)

        def start(rows):
            if first:
                a = _modulate_bf16(h_ref[rows, :], mods_ref, k_shift)
                if not last:
                    a_ref[rows, :] = a
            else:
                a = a_ref[rows, :]
            return up(a, rows)

        def finish(rows, pre):
            acc = _dot(hidden(pre).astype(BF16), wd_ref[...])
            if not first:
                acc = acc_ref[rows, :] + acc
            if last:
                o_ref[rows, :] = _deepnorm(h_ref[rows, :], res_scale * _mod_row(mods_ref, k_gate),
                                           acc, alpha, lng_ref[...], lnb_ref[...])
            else:
                acc_ref[rows, :] = acc

        pending = None
        for rows in row_blocks(FUSED_ROW_BLOCK[kind]):
            pre = start(rows)
            if pending is not None:
                finish(*pending)
            pending = (rows, pre)
        finish(*pending)

    if n_steps == 1:
        step(True, True)
        return
    j = pl.program_id(1)
    pl.when(j == 0)(functools.partial(step, True, False))
    if n_steps > 2:
        pl.when(jnp.logical_and(j > 0, j < n_steps - 1))(functools.partial(step, False, False))
    pl.when(j == n_steps - 1)(functools.partial(step, False, True))


def _lead_spec(lead, block, index_fn):
    lead = tuple(lead)
    return pl.BlockSpec((None,) * len(lead) + tuple(block),
                        lambda *g: lead + tuple(index_fn(*g)))


def _fused_call(kind, h, mods, weights, lead, ln_g, ln_b, *, k_shift, k_gate, res_scale, alpha,
                period=GRID_W, vn=None, cast=None):
    n, d = h.shape
    nb = mods.shape[0]
    tm = _pick(n // nb, (FUSED_TILE[kind], 512, 256, 128))
    n_tiles = n // tm
    tiles_per_mod = n_tiles // nb

    row = lambda i, j: (i, 0)
    col_blk = lambda i, j: (0, j)
    row_blk = lambda i, j: (j, 0)
    common_in = [pl.BlockSpec((tm, d), row),
                 pl.BlockSpec((1, N_MOD, d), lambda i, j: (i // tiles_per_mod, 0, 0))]
    if kind == "ffn":
        wg, wu, wd = weights
        f = wg.shape[-1]
        tc = _pick(f, (FUSED_CHUNK, 256, 128))
        n_chunks = f // tc
        w_in = [_lead_spec(lead, (d, tc), col_blk), _lead_spec(lead, (d, tc), col_blk),
                _lead_spec(lead, (tc, d), row_blk)]
        w_args = [wg, wu, wd]
        w_bytes = 3 * d * tc * 2
    elif kind == "conv":
        assert period & (period - 1) == 0 and tm % period == 0, (tm, period)
        w_in3, cw, wd = weights
        tc = _pick(d, (FUSED_CHUNK, 256, 128))
        n_chunks = d // tc
        w_in = [_lead_spec(lead, (d, tc), col_blk),
                _lead_spec(lead, (d, tc), lambda i, j: (0, n_chunks + j)),
                _lead_spec(lead, (d, tc), lambda i, j: (0, 2 * n_chunks + j)),
                _lead_spec(lead, (3, tc), col_blk),
                _lead_spec(lead, (tc, d), row_blk)]
        w_args = [w_in3, w_in3, w_in3, cw, wd]
        w_bytes = 4 * d * tc * 2
    else:
        w_in2, ws, bs, wd = weights
        groups = ws.shape[-3]
        gw = wd.shape[-2] // groups
        per_step = max(g for g in range(1, groups + 1) if groups % g == 0 and g * gw <= 1024)
        tc = per_step * gw
        n_chunks = groups // per_step
        w_in = [_lead_spec(lead, (d, tc), col_blk),
                pl.BlockSpec((tm, tc), lambda i, j: (i, j)),
                _lead_spec(lead, (per_step, CHUNK, CHUNK), lambda i, j: (j, 0, 0)),
                _lead_spec(lead, (per_step, CHUNK, 1), lambda i, j: (j, 0, 0)),
                _lead_spec(lead, (tc, d), row_blk)]
        w_args = [w_in2, vn, ws, bs, wd]
        w_bytes = 2 * d * tc * 2 + tm * tc * 2
    vec = pl.BlockSpec((1, d), lambda i, j: (0, 0))
    rb = min(tm, FUSED_ROW_BLOCK[kind])
    est = (2 * 2 * tm * d * 4
           + tm * d * 2
           + 2 * w_bytes
           + 6 * rb * tc * 4
           + 2 * rb * d * 4)
    cast_arrays, cast_lead = cast if cast is not None else ((), ())
    n_steps = n_tiles * n_chunks
    cast_in, cast_out, cast_shapes = [], [], []
    for arr in cast_arrays:
        rows, cols = arr.shape[-2:]
        if (cols % (n_chunks * V7X_LANES) == 0 and rows % (n_tiles * CAST_ROWS) == 0):
            blk, idx = (rows // n_tiles, cols // n_chunks), (lambda i, j: (i, j))
        else:
            assert rows % (n_steps * CAST_ROWS) == 0, "cast array does not split over the grid"
            blk, idx = (rows // n_steps, cols), (lambda i, j: (i * n_chunks + j, 0))
        cast_in.append(_lead_spec(cast_lead, blk, idx))
        cast_out.append(pl.BlockSpec(blk, idx))
        cast_shapes.append(jax.ShapeDtypeStruct((rows, cols), BF16))
        est += 2 * blk[0] * blk[1] * (4 + 2)
    kern = functools.partial(_fused_kernel, kind, k_shift, k_gate, res_scale, alpha, period,
                             n_chunks, len(cast_arrays))
    outs = pl.pallas_call(
        kern,
        grid=(n_tiles, n_chunks),
        in_specs=common_in + w_in + [vec, vec] + cast_in,
        out_specs=[pl.BlockSpec((tm, d), row)] + cast_out,
        out_shape=[jax.ShapeDtypeStruct((n, d), F32)] + cast_shapes,
        scratch_shapes=[pltpu.VMEM((tm, d), BF16)],
        compiler_params=_params(("parallel", "arbitrary"), est),
        name="fused_" + kind,
    )(h, mods, *w_args, ln_g.reshape(1, d), ln_b.reshape(1, d), *cast_arrays)
    return outs[0] if cast is None else (outs[0], outs[1:])


def _qkv_kernel(k_shift, n_rope, h_ref, mods_ref, w_ref, tab_ref, o_ref, vt_ref, a_ref):
    s = pl.program_id(1)
    n_sec = pl.num_programs(1)
    cos_ref, sin_ref = tab_ref.at[0], tab_ref.at[1]

    @pl.when(s == 0)
    def _():
        a_ref[...] = _modulate_bf16(h_ref[...], mods_ref, k_shift)

    def project(rope, transposed):
        width = w_ref.shape[1]
        slab = _pick(width, (512, 256, 128))
        for c in range(width // slab):
            y = _dot(a_ref[...], w_ref[:, c * slab:(c + 1) * slab])
            if transposed:
                vt_ref[c * slab:(c + 1) * slab, :] = y.T.astype(vt_ref.dtype)
                continue
            for r in range(slab // V7X_LANES):
                yr = y[:, r * V7X_LANES:(r + 1) * V7X_LANES]
                if rope:
                    yr = yr * cos_ref[...] + pltpu.roll(yr, V7X_LANES // 2, 1) * sin_ref[...]
                lo = c * slab + r * V7X_LANES
                o_ref[:, lo:lo + V7X_LANES] = yr.astype(o_ref.dtype)

    if n_rope > 0:
        @pl.when(s < n_rope)
        def _():
            project(True, False)

    @pl.when(jnp.logical_and(s >= n_rope, s < n_sec - 1))
    def _():
        project(False, False)

    @pl.when(s == n_sec - 1)
    def _():
        project(False, True)


def _qkv_call(h, mods, w, lead, tabs, *, k_shift, n_rope, seq, first_sec=0):
    n, d = h.shape
    n_tab = tabs.shape[0]
    nb = mods.shape[0]
    n_sec = w.shape[-1] // d - first_sec
    tm = _pick(min(n // nb, seq), (512, 256, 128))
    n_tiles = n // tm
    tiles_per_mod = n_tiles // nb
    tiles_per_seq = seq // tm
    est = (2 * tm * d * 4 + tm * d * 2 + 2 * d * d * 2 + 2 * 2 * tm * d * 2
           + 8 * tm * V7X_LANES * 4 + 4 * tm * 512 * 4)
    kern = functools.partial(_qkv_kernel, k_shift, n_rope)
    return pl.pallas_call(
        kern,
        grid=(n_tiles, n_sec),
        in_specs=[pl.BlockSpec((tm, d), lambda i, s: (i, 0)),
                  pl.BlockSpec((1, N_MOD, d), lambda i, s: (i // tiles_per_mod, 0, 0)),
                  _lead_spec(lead, (d, d), lambda i, s: (0, first_sec + s)),
                  pl.BlockSpec((None, 2, tm, V7X_LANES),
                               lambda i, s: (jnp.minimum(s, n_tab - 1), 0, i % tiles_per_seq, 0))],
        out_specs=[pl.BlockSpec((tm, d), lambda i, s: (i, jnp.minimum(s, n_sec - 2))),
                   pl.BlockSpec((None, d, tm), lambda i, s: (i, 0, 0))],
        out_shape=[jax.ShapeDtypeStruct((n, (n_sec - 1) * d), BF16),
                   jax.ShapeDtypeStruct((n_tiles, d, tm), BF16)],
        scratch_shapes=[pltpu.VMEM((tm, d), BF16)],
        compiler_params=_params(("parallel", "arbitrary"), est),
        name="qkv_proj",
    )(h, mods, w, tabs)


def _row_blocks(n_rows, block):
    block = min(n_rows, block)
    return [slice(r, r + block) for r in range(0, n_rows, block)]


def _gate_branch_kernel(k_shift, h_ref, mods_ref, w_ref, g_ref, b_ref, o_ref):
    for rows in _row_blocks(h_ref.shape[0], PROJ_ROW_BLOCK["gate"]):
        a = _modulate_bf16(h_ref[rows, :], mods_ref, k_shift)
        v = _gelu(_dot(a, w_ref[...]))
        o_ref[rows, :] = _layer_norm(v, g_ref[...], b_ref[...]).astype(o_ref.dtype)


def _gate_branch_call(h, mods, w, lead, g, b, *, k_shift):
    n, d = h.shape
    nb = mods.shape[0]
    width = w.shape[-1] // 2
    tm = _pick(n // nb, (1024, 512, 256, 128))
    n_tiles = n // tm
    tiles_per_mod = n_tiles // nb
    est = 2 * tm * d * 4 + 2 * d * width * 2 + 2 * tm * width * 2 + 4 * tm * width * 4
    return pl.pallas_call(
        functools.partial(_gate_branch_kernel, k_shift),
        grid=(n_tiles,),
        in_specs=[pl.BlockSpec((tm, d), lambda i: (i, 0)),
                  pl.BlockSpec((1, N_MOD, d), lambda i: (i // tiles_per_mod, 0, 0)),
                  _lead_spec(lead, (d, width), lambda i: (0, 1)),
                  pl.BlockSpec((1, width), lambda i: (0, 0)),
                  pl.BlockSpec((1, width), lambda i: (0, 0))],
        out_specs=pl.BlockSpec((tm, width), lambda i: (i, 0)),
        out_shape=jax.ShapeDtypeStruct((n, width), BF16),
        compiler_params=_params(("parallel",), est),
        name="gmlp_gate_branch",
    )(h, mods, w, g.reshape(1, width), b.reshape(1, width))


def _attn_kernel(lam_init, lam_ref, q_ref, k_ref, vt_ref, kc_ref, vct_ref, subln_ref, o_ref,
                 acc_ref, s_ref):
    dk = q_ref.shape[1] // 2
    tq = q_ref.shape[0]
    tk = vt_ref.shape[2]
    q = q_ref[...]
    qm = (q[:, :dk], q[:, dk:])
    nt = (((1,), (1,)), ((), ()))

    def update(carry, blocks):
        for g, (k_blk, _) in enumerate(blocks):
            for mp in range(2):
                s_ref[g, mp, 0:k_blk.shape[0], :] = lax.dot_general(
                    k_blk[:, mp * dk:(mp + 1) * dk], qm[mp], nt, preferred_element_type=F32)
        carry = list(carry)
        for g, (k_blk, vt_chunks) in enumerate(blocks):
            for mp in range(2):
                m_old, l_old = carry[2 * mp], carry[2 * mp + 1]
                st = s_ref[g, mp, 0:k_blk.shape[0], :]
                m_new = jnp.maximum(m_old, jnp.max(st, axis=0, keepdims=True))
                p = jnp.exp2(st - m_new)
                corr = jnp.exp2(m_old - m_new)
                carry[2 * mp] = m_new
                carry[2 * mp + 1] = corr * l_old + jnp.sum(p, axis=0, keepdims=True)
                pt = p.astype(BF16)
                acc = corr * acc_ref[mp]
                lo = 0
                for vt_c in vt_chunks:
                    acc = acc + _dot(vt_c, pt[lo:lo + vt_c.shape[1], :])
                    lo += vt_c.shape[1]
                acc_ref[mp] = acc
        return tuple(carry)

    acc_ref[...] = jnp.zeros_like(acc_ref)
    neg = jnp.full((1, tq), -jnp.inf, F32)
    zero = jnp.zeros((1, tq), F32)
    group, kb = s_ref.shape[0], s_ref.shape[2]
    chunks = kb // tk

    def body(c, carry):
        blocks = []
        for g in range(group):
            blk = c * group + g
            start = pl.multiple_of(blk * kb, kb)
            blocks.append((k_ref[pl.ds(start, kb), :],
                           [vt_ref[blk * chunks + i] for i in range(chunks)]))
        return update(carry, blocks)

    carry = lax.fori_loop(0, vt_ref.shape[0] // (group * chunks), body, (neg, zero, neg, zero))
    tc = vct_ref.shape[2]
    carry = update(carry, [(kc_ref[c * tc:(c + 1) * tc, :], [vct_ref[c]])
                           for c in range(vct_ref.shape[0])])
    _, l0, _, l1 = carry

    lp = lam_ref[...]
    lam = (jnp.exp(jnp.sum(lp[0:1, :] * lp[1:2, :], axis=-1, keepdims=True))
           - jnp.exp(jnp.sum(lp[2:3, :] * lp[3:4, :], axis=-1, keepdims=True)) + lam_init)
    ot = acc_ref[0] * (1.0 / l0) - acc_ref[1] * (lam / l1)
    norm = lax.rsqrt(jnp.mean(ot * ot, axis=0, keepdims=True) + LN_EPS) * (1.0 - lam_init)
    o_ref[...] = (ot * norm * subln_ref[...]).T.astype(o_ref.dtype)


def _attn_call(qk, vt, kc, vct, lam_p, subln, *, n_batch, seq, ctx_len, heads, lam_init):
    d = qk.shape[1] // 2
    dv = d // heads
    tk = vt.shape[2]
    tc = vct.shape[2]
    nk = seq // tk
    nc = ctx_len // tc
    kb = tk * _pick(nk, (ATTN_KEYS // tk, 1))
    group = _pick(seq // kb, (ATTN_GROUP, 1))
    assert nc <= group and tc <= kb, "context keys must fit one score-scratch group"
    tq = _pick(seq, (ATTN_TQ, 128))
    nq = seq // tq
    est = (2 * 2 * tq * dv * 2 + 2 * 2 * seq * dv * 2 + 2 * 2 * ctx_len * dv * 2
           + 2 * tq * dv * 4 + (2 * group + 3) * tq * kb * 4)
    kern = functools.partial(_attn_kernel, lam_init)
    return pl.pallas_call(
        kern,
        grid=(n_batch, heads, nq),
        in_specs=[pl.BlockSpec(lam_p.shape, lambda b, h, i: (0, 0)),
                  pl.BlockSpec((tq, dv), lambda b, h, i: (b * nq + i, h)),
                  pl.BlockSpec((seq, dv), lambda b, h, i: (b, heads + h)),
                  pl.BlockSpec((nk, dv, tk), lambda b, h, i: (b, h, 0)),
                  pl.BlockSpec((ctx_len, dv), lambda b, h, i: (b, h)),
                  pl.BlockSpec((nc, dv, tc), lambda b, h, i: (b, h, 0)),
                  pl.BlockSpec((dv, 1), lambda b, h, i: (0, 0))],
        out_specs=pl.BlockSpec((tq, dv), lambda b, h, i: (b * nq + i, h)),
        out_shape=jax.ShapeDtypeStruct((n_batch * seq, d), BF16),
        scratch_shapes=[pltpu.VMEM((2, dv, tq), F32), pltpu.VMEM((group, 2, kb, tq), F32)],
        compiler_params=_params(("parallel", "parallel", "arbitrary"), est),
        name="diff_attention",
    )(lam_p, qk, qk, vt, kc, vct, subln.reshape(dv, 1))


def _out_proj_kernel(k_gate, alpha, y_ref, h_ref, mods_ref, w_ref, g_ref, b_ref, o_ref):
    for rows in _row_blocks(h_ref.shape[0], PROJ_ROW_BLOCK["out"]):
        y = _dot(y_ref[rows, :], w_ref[...])
        o_ref[rows, :] = _deepnorm(h_ref[rows, :], _mod_row(mods_ref, k_gate), y, alpha,
                                   g_ref[...], b_ref[...])


def _out_proj_call(y, h, mods, w, lead, g, b, *, k_gate, alpha):
    n, d = h.shape
    nb = mods.shape[0]
    tm = _pick(n // nb, (512, 256, 128))
    n_tiles = n // tm
    tiles_per_mod = n_tiles // nb
    est = 2 * tm * d * 2 + 2 * 2 * tm * d * 4 + 2 * d * d * 2 + 4 * tm * d * 4
    return pl.pallas_call(
        functools.partial(_out_proj_kernel, k_gate, alpha),
        grid=(n_tiles,),
        in_specs=[pl.BlockSpec((tm, d), lambda i: (i, 0)),
                  pl.BlockSpec((tm, d), lambda i: (i, 0)),
                  pl.BlockSpec((1, N_MOD, d), lambda i: (i // tiles_per_mod, 0, 0)),
                  _lead_spec(lead, (d, d), lambda i: (0, 0)),
                  pl.BlockSpec((1, d), lambda i: (0, 0)),
                  pl.BlockSpec((1, d), lambda i: (0, 0))],
        out_specs=pl.BlockSpec((tm, d), lambda i: (i, 0)),
        out_shape=jax.ShapeDtypeStruct((n, d), F32),
        compiler_params=_params(("parallel",), est),
        name="out_proj_norm",
    )(y, h, mods, w, g.reshape(1, d), b.reshape(1, d))


def _rope_tables(seq, dk):
    n_freq = dk // 4
    t = jnp.arange(seq)
    inv = ROPE_BASE ** (-jnp.arange(n_freq, dtype=F32) / n_freq)
    ang_r = (t // GRID_W).astype(F32)[:, None] * inv
    ang_c = (t % GRID_W).astype(F32)[:, None] * inv
    cos_t = jnp.concatenate([jnp.cos(ang_r), jnp.cos(ang_c)] * 2, axis=-1)
    sin_t = jnp.concatenate([-jnp.sin(ang_r), -jnp.sin(ang_c), jnp.sin(ang_r), jnp.sin(ang_c)], axis=-1)
    k_tab = jnp.stack([cos_t, sin_t])
    return jnp.stack([k_tab * (dk ** -0.5 * math.log2(math.e)), k_tab])


def _rope_column_layout(w, dk):
    rows, width = w.shape
    w = w.reshape(rows, width // dk, 2, 2, dk // 4)
    return jnp.swapaxes(w, 2, 3).reshape(rows, width)


def kernel(x, c, ctx, c_ctx, ada_w, ada_b, ln_g, ln_b, ffn_wg, ffn_wu, ffn_wd, sc_w_in, sc_conv,
           sc_w_out, da_w_qkv, da_lambda, da_subln, da_w_o, gm_w_in, gm_ln_g, gm_ln_b, gm_w_s,
           gm_b_s, gm_w_out):
    n_batch, seq, d = x.shape
    ctx_len = ctx.shape[1]
    depth = ada_w.shape[0]
    mixer_of_layer = tuple(i % N_MIXERS for i in range(depth))
    last_ctx_layer = max([i for i in range(depth) if mixer_of_layer[i] == 1], default=-1)
    alpha = (2.0 * depth) ** 0.25
    dv = da_subln.shape[-1]
    heads = d // dv
    dk = dv // 2

    h = x.reshape(n_batch * seq, d)
    hc = ctx.reshape(n_batch * ctx_len, d)

    n_cond = n_batch + 1
    cond = jnp.zeros((16 * ((n_cond + 15) // 16), d), F32)
    cond = cond.at[:n_batch].set(c).at[n_batch].set(c_ctx)

    rope_tabs = _rope_tables(seq, dk)
    no_rope_tabs = jnp.zeros((1, 2, ctx_len, dk), F32)

    ffn_f32 = (ffn_wg, ffn_wu, ffn_wd)
    ffn_w = [tuple(w[0, 0].astype(BF16) for w in ffn_f32)]
    conv_w = (sc_w_in.astype(BF16), sc_conv, sc_w_out.astype(BF16))
    qkv_w = da_w_qkv.astype(BF16)
    qkv_w = jnp.concatenate(
        [_rope_column_layout(qkv_w[..., :2 * d].reshape(-1, 2 * d), dk).reshape(qkv_w.shape[0], d, 2 * d),
         qkv_w[..., 2 * d:]], axis=-1)
    attn_wo = da_w_o.astype(BF16)
    gmlp_w = (gm_w_in.astype(BF16), gm_w_s.astype(BF16), gm_b_s[..., None], gm_w_out.astype(BF16))

    for i in range(depth):
        kind = mixer_of_layer[i]
        j = i // N_MIXERS
        ctx_in = i <= last_ctx_layer
        ctx_out = i < last_ctx_layer
        mods_all = _ada(cond, ada_w, ada_b, i).reshape(-1, N_MOD, d)
        md = mods_all[:n_batch]
        mdc = mods_all[n_batch:n_batch + 1]

        def ffn_pair(h_lat, h_ctx, half, k0):
            nxt = (i, 1) if half == 0 else (i + 1, 0)
            args = dict(ln_g=ln_g[i, 2 * half], ln_b=ln_b[i, 2 * half], k_shift=k0,
                        k_gate=k0 + 2, res_scale=0.5, alpha=alpha)
            w_now = ffn_w[0]
            if nxt[0] < depth:
                h_lat, w_next = _fused_call("ffn", h_lat, md, w_now, (), cast=(ffn_f32, nxt), **args)
                ffn_w[0] = tuple(w_next)
            else:
                h_lat = _fused_call("ffn", h_lat, md, w_now, (), **args)
            if h_ctx is not None:
                h_ctx = _fused_call("ffn", h_ctx, mdc, w_now, (), **args)
            return h_lat, h_ctx

        h, hc_new = ffn_pair(h, hc if ctx_in else None, 0, 0)
        if ctx_in:
            hc = hc_new

        if kind == 0:
            conv = functools.partial(_fused_call, "conv", weights=conv_w, lead=(j,),
                                     ln_g=ln_g[i, 1], ln_b=ln_b[i, 1], k_shift=3, k_gate=5,
                                     res_scale=1.0, alpha=alpha)
            h = conv(h, md, period=GRID_W)
            if ctx_out:
                hc = conv(hc, mdc, period=ctx_len)
        elif kind == 1:
            lam_init = 0.8 - 0.6 * math.exp(-0.3 * i)
            qk, vt = _qkv_call(h, md, qkv_w, (j,), rope_tabs, k_shift=3, n_rope=2, seq=seq)
            kc, vct = _qkv_call(hc, mdc, qkv_w, (j,), no_rope_tabs, k_shift=3, n_rope=0,
                                seq=ctx_len, first_sec=1)
            o = _attn_call(qk, vt, kc, vct, da_lambda[j], da_subln[j], n_batch=n_batch, seq=seq,
                           ctx_len=ctx_len, heads=heads, lam_init=lam_init)
            h = _out_proj_call(o, h, md, attn_wo, (j,), ln_g[i, 1], ln_b[i, 1], k_gate=5,
                               alpha=alpha)
            assert not ctx_out, "context-side attention output is not implemented"
        else:
            def gmlp(hh, mm):
                vn = _gate_branch_call(hh, mm, gmlp_w[0], (j,), gm_ln_g[j], gm_ln_b[j], k_shift=3)
                return _fused_call("gmlp", hh, mm, gmlp_w, (j,), ln_g[i, 1], ln_b[i, 1],
                                   k_shift=3, k_gate=5, res_scale=1.0, alpha=alpha, vn=vn)

            h = gmlp(h, md)
            if ctx_out:
                hc = gmlp(hc, mdc)

        h, hc_new = ffn_pair(h, hc if ctx_out else None, 1, 6)
        if ctx_out:
            hc = hc_new
    return h.reshape(n_batch, seq, d)
```

```python
import functools
import math

import jax
import jax.numpy as jnp
from jax import lax
from jax.experimental import pallas as pl
from jax.experimental.pallas import tpu as pltpu

GRID_W = 64
CHUNK = 128
N_MOD = 9
N_MIXERS = 3
ROPE_BASE = 10000.0
LN_EPS = 1e-5

V7X_LANES = 128
V7X_VMEM_BYTES = 64 * 1024 * 1024
V7X_VMEM_CAP = V7X_VMEM_BYTES - 6 * 1024 * 1024

BF16 = jnp.bfloat16
F32 = jnp.float32

ATTN_TQ = 1024
ATTN_KEYS = 2048
ATTN_GROUP = 1
FUSED_CHUNK = 512
CAST_ROWS = 16
FUSED_TILE = {"ffn": 1024, "conv": 1024, "gmlp": 512}
FUSED_ROW_BLOCK = {"ffn": 1024, "conv": 512, "gmlp": 256}
PROJ_ROW_BLOCK = {"gate": 512, "out": 256}
FFN_LAST_ROW_BLOCK = 256


def _params(semantics, vmem_estimate):
    limit = min(V7X_VMEM_CAP, max(32 * 1024 * 1024, int(vmem_estimate * 1.3)))
    return pltpu.CompilerParams(dimension_semantics=semantics, vmem_limit_bytes=limit)


def _pick(n, candidates):
    for c in candidates:
        if n % c == 0:
            return c
    return n


def _mod_row(mods_ref, k):
    return mods_ref[0, k:k + 1, :]


def _modulate_bf16(h, mods_ref, k_shift):
    shift = _mod_row(mods_ref, k_shift)
    scale = _mod_row(mods_ref, k_shift + 1)
    return (h * (1.0 + scale) + shift).astype(BF16)


def _layer_norm(x, g, b, eps=LN_EPS):
    mu = jnp.mean(x, axis=-1, keepdims=True)
    xc = x - mu
    var = jnp.mean(xc * xc, axis=-1, keepdims=True)
    return xc * lax.rsqrt(var + eps) * g + b


def _deepnorm(h, delta_scale, delta, alpha, g, b):
    return _layer_norm(h + (delta_scale * (1.0 / alpha)) * delta, g, b, LN_EPS / (alpha * alpha))


def _silu(x):
    return x / (1.0 + jnp.exp(-x))


def _gelu(x):
    return 0.5 * x * (1.0 + lax.erf(x * math.sqrt(0.5)))


def _dot(a, b):
    return jnp.dot(a, b, preferred_element_type=F32)


def _ada_kernel(cond_ref, w_ref, b_ref, o_ref):
    a = _silu(cond_ref[...]).astype(BF16)
    o_ref[...] = _dot(a, w_ref[...].astype(BF16)) + b_ref[...]


def _ada(cond, w, b, layer):
    m, d = cond.shape
    n = w.shape[2]
    tn = _pick(n, (1024, 512, 256, 128))
    est = 2 * d * tn * 4 + d * tn * 2 + 4 * m * (d + tn) * 4
    return pl.pallas_call(
        _ada_kernel,
        grid=(n // tn,),
        in_specs=[pl.BlockSpec((m, d), lambda j: (0, 0)),
                  pl.BlockSpec((None, d, tn), lambda j: (layer, 0, j)),
                  pl.BlockSpec((None, 1, tn), lambda j: (layer, 0, j))],
        out_specs=pl.BlockSpec((m, tn), lambda j: (0, j)),
        out_shape=jax.ShapeDtypeStruct((m, n), F32),
        compiler_params=_params(("parallel",), est),
        name="ada_mod",
    )(cond, w, b.reshape(b.shape[0], 1, n))


def _fused_kernel(kind, k_shift, k_gate, res_scale, alpha, period, n_steps, n_cast, *refs):
    h_ref, mods_ref = refs[0], refs[1]
    n_in = len(refs) - 2 - n_cast
    w = refs[2:n_in - 2 - n_cast]
    lng_ref, lnb_ref = refs[n_in - 2 - n_cast:n_in - n_cast]
    cast_in = refs[n_in - n_cast:n_in]
    o_ref = refs[n_in]
    cast_out = refs[n_in + 1:n_in + 1 + n_cast]
    a_ref = refs[-1]
    acc_ref = o_ref
    wd_ref = w[-1]
    tm = h_ref.shape[0]


    def row_blocks(size):
        rb = min(tm, max(size, period))
        return [slice(r, r + rb) for r in range(0, tm, rb)]

    def up(a, rows):
        if kind == "ffn":
            return _dot(a, w[0][...]), _dot(a, w[1][...])
        if kind == "conv":
            return _dot(a, w[0][...]), _dot(a, w[1][...]), _dot(a, w[2][...])
        vn_ref, ws_ref, bs_ref = w[1], w[2], w[3]
        gw = w[0].shape[1] // ws_ref.shape[0]
        cols = []
        for g in range(ws_ref.shape[0]):
            parts = [_dot(ws_ref[g], vn_ref[c:c + CHUNK, g * gw:(g + 1) * gw]) + bs_ref[g]
                     for c in range(rows.start, rows.stop, CHUNK)]
            cols.append(jnp.concatenate(parts, axis=0))
        return _dot(a, w[0][...]), jnp.concatenate(cols, axis=1)

    def hidden(pre):
        if kind == "ffn":
            g, u = pre
            return _silu(g) * u
        if kind == "conv":
            b, c, v = pre
            z = c * v
            pos = lax.broadcasted_iota(jnp.int32, z.shape, 0) & (period - 1)
            z_prev = jnp.where(pos == 0, 0.0, pltpu.roll(z, 1, 0))
            z_next = jnp.where(pos == period - 1, 0.0, pltpu.roll(z, z.shape[0] - 1, 0))
            cw = w[3][...]
            return b * (cw[0:1, :] * z_prev + cw[1:2, :] * z + cw[2:3, :] * z_next)
        u, s = pre
        return _gelu(u) * s

    def step(first, last):
        for src, dst in zip(cast_in, cast_out):
            dst[...] = src[...].astype(dst.dtype)

        def start(rows):
            if first:
                a = _modulate_bf16(h_ref[rows, :], mods_ref, k_shift)
                if not last:
                    a_ref[rows, :] = a
            else:
                a = a_ref[rows, :]
            return up(a, rows)

        def finish(rows, pre):
            acc = _dot(hidden(pre).astype(BF16), wd_ref[...])
            if not first:
                acc = acc_ref[rows, :] + acc
            if last:
                o_ref[rows, :] = _deepnorm(h_ref[rows, :], res_scale * _mod_row(mods_ref, k_gate),
                                           acc, alpha, lng_ref[...], lnb_ref[...])
            else:
                acc_ref[rows, :] = acc

        if last and not first and kind == "ffn":
            for rows in row_blocks(FFN_LAST_ROW_BLOCK):
                finish(rows, start(rows))
            return
        pending = None
        for rows in row_blocks(FUSED_ROW_BLOCK[kind]):
            pre = start(rows)
            if pending is not None:
                finish(*pending)
            pending = (rows, pre)
        finish(*pending)

    if n_steps == 1:
        step(True, True)
        return
    j = pl.program_id(1)
    pl.when(j == 0)(functools.partial(step, True, False))
    if n_steps > 2:
        pl.when(jnp.logical_and(j > 0, j < n_steps - 1))(functools.partial(step, False, False))
    pl.when(j == n_steps - 1)(functools.partial(step, False, True))


def _lead_spec(lead, block, index_fn):
    lead = tuple(lead)
    return pl.BlockSpec((None,) * len(lead) + tuple(block),
                        lambda *g: lead + tuple(index_fn(*g)))


def _fused_call(kind, h, mods, weights, lead, ln_g, ln_b, *, k_shift, k_gate, res_scale, alpha,
                period=GRID_W, vn=None, cast=None):
    n, d = h.shape
    nb = mods.shape[0]
    tm = _pick(n // nb, (FUSED_TILE[kind], 512, 256, 128))
    n_tiles = n // tm
    tiles_per_mod = n_tiles // nb

    row = lambda i, j: (i, 0)
    col_blk = lambda i, j: (0, j)
    row_blk = lambda i, j: (j, 0)
    common_in = [pl.BlockSpec((tm, d), row),
                 pl.BlockSpec((1, N_MOD, d), lambda i, j: (i // tiles_per_mod, 0, 0))]
    if kind == "ffn":
        wg, wu, wd = weights
        f = wg.shape[-1]
        tc = _pick(f, (FUSED_CHUNK, 256, 128))
        n_chunks = f // tc
        w_in = [_lead_spec(lead, (d, tc), col_blk), _lead_spec(lead, (d, tc), col_blk),
                _lead_spec(lead, (tc, d), row_blk)]
        w_args = [wg, wu, wd]
        w_bytes = 3 * d * tc * 2
    elif kind == "conv":
        assert period & (period - 1) == 0 and tm % period == 0, (tm, period)
        w_in3, cw, wd = weights
        tc = _pick(d, (FUSED_CHUNK, 256, 128))
        n_chunks = d // tc
        w_in = [_lead_spec(lead, (d, tc), col_blk),
                _lead_spec(lead, (d, tc), lambda i, j: (0, n_chunks + j)),
                _lead_spec(lead, (d, tc), lambda i, j: (0, 2 * n_chunks + j)),
                _lead_spec(lead, (3, tc), col_blk),
                _lead_spec(lead, (tc, d), row_blk)]
        w_args = [w_in3, w_in3, w_in3, cw, wd]
        w_bytes = 4 * d * tc * 2
    else:
        w_in2, ws, bs, wd = weights
        groups = ws.shape[-3]
        gw = wd.shape[-2] // groups
        per_step = max(g for g in range(1, groups + 1) if groups % g == 0 and g * gw <= 1024)
        tc = per_step * gw
        n_chunks = groups // per_step
        w_in = [_lead_spec(lead, (d, tc), col_blk),
                pl.BlockSpec((tm, tc), lambda i, j: (i, j)),
                _lead_spec(lead, (per_step, CHUNK, CHUNK), lambda i, j: (j, 0, 0)),
                _lead_spec(lead, (per_step, CHUNK, 1), lambda i, j: (j, 0, 0)),
                _lead_spec(lead, (tc, d), row_blk)]
        w_args = [w_in2, vn, ws, bs, wd]
        w_bytes = 2 * d * tc * 2 + tm * tc * 2
    vec = pl.BlockSpec((1, d), lambda i, j: (0, 0))
    rb = min(tm, FUSED_ROW_BLOCK[kind])
    est = (2 * 2 * tm * d * 4
           + tm * d * 2
           + 2 * w_bytes
           + 6 * rb * tc * 4
           + 2 * rb * d * 4)
    cast_arrays, cast_lead = cast if cast is not None else ((), ())
    n_steps = n_tiles * n_chunks
    cast_in, cast_out, cast_shapes = [], [], []
    for arr in cast_arrays:
        rows, cols = arr.shape[-2:]
        if (cols % (n_chunks * V7X_LANES) == 0 and rows % (n_tiles * CAST_ROWS) == 0):
            blk, idx = (rows // n_tiles, cols // n_chunks), (lambda i, j: (i, j))
        else:
            assert rows % (n_steps * CAST_ROWS) == 0, "cast array does not split over the grid"
            blk, idx = (rows // n_steps, cols), (lambda i, j: (i * n_chunks + j, 0))
        cast_in.append(_lead_spec(cast_lead, blk, idx))
        cast_out.append(pl.BlockSpec(blk, idx))
        cast_shapes.append(jax.ShapeDtypeStruct((rows, cols), BF16))
        est += 2 * blk[0] * blk[1] * (4 + 2)
    kern = functools.partial(_fused_kernel, kind, k_shift, k_gate, res_scale, alpha, period,
                             n_chunks, len(cast_arrays))
    outs = pl.pallas_call(
        kern,
        grid=(n_tiles, n_chunks),
        in_specs=common_in + w_in + [vec, vec] + cast_in,
        out_specs=[pl.BlockSpec((tm, d), row)] + cast_out,
        out_shape=[jax.ShapeDtypeStruct((n, d), F32)] + cast_shapes,
        scratch_shapes=[pltpu.VMEM((tm, d), BF16)],
        compiler_params=_params(("parallel", "arbitrary"), est),
        name="fused_" + kind,
    )(h, mods, *w_args, ln_g.reshape(1, d), ln_b.reshape(1, d), *cast_arrays)
    return outs[0] if cast is None else (outs[0], outs[1:])


def _qkv_kernel(k_shift, n_rope, h_ref, mods_ref, w_ref, tab_ref, o_ref, vt_ref, a_ref):
    s = pl.program_id(1)
    n_sec = pl.num_programs(1)
    cos_ref, sin_ref = tab_ref.at[0], tab_ref.at[1]

    @pl.when(s == 0)
    def _():
        a_ref[...] = _modulate_bf16(h_ref[...], mods_ref, k_shift)

    def project(rope, transposed):
        width = w_ref.shape[1]
        slab = _pick(width, (512, 256, 128))
        for c in range(width // slab):
            y = _dot(a_ref[...], w_ref[:, c * slab:(c + 1) * slab])
            if transposed:
                vt_ref[c * slab:(c + 1) * slab, :] = y.T.astype(vt_ref.dtype)
                continue
            for r in range(slab // V7X_LANES):
                yr = y[:, r * V7X_LANES:(r + 1) * V7X_LANES]
                if rope:
                    yr = yr * cos_ref[...] + pltpu.roll(yr, V7X_LANES // 2, 1) * sin_ref[...]
                lo = c * slab + r * V7X_LANES
                o_ref[:, lo:lo + V7X_LANES] = yr.astype(o_ref.dtype)

    if n_rope > 0:
        @pl.when(s < n_rope)
        def _():
            project(True, False)

    @pl.when(jnp.logical_and(s >= n_rope, s < n_sec - 1))
    def _():
        project(False, False)

    @pl.when(s == n_sec - 1)
    def _():
        project(False, True)


def _qkv_call(h, mods, w, lead, tabs, *, k_shift, n_rope, seq, first_sec=0):
    n, d = h.shape
    n_tab = tabs.shape[0]
    nb = mods.shape[0]
    n_sec = w.shape[-1] // d - first_sec
    tm = _pick(min(n // nb, seq), (512, 256, 128))
    n_tiles = n // tm
    tiles_per_mod = n_tiles // nb
    tiles_per_seq = seq // tm
    est = (2 * tm * d * 4 + tm * d * 2 + 2 * d * d * 2 + 2 * 2 * tm * d * 2
           + 8 * tm * V7X_LANES * 4 + 4 * tm * 512 * 4)
    kern = functools.partial(_qkv_kernel, k_shift, n_rope)
    return pl.pallas_call(
        kern,
        grid=(n_tiles, n_sec),
        in_specs=[pl.BlockSpec((tm, d), lambda i, s: (i, 0)),
                  pl.BlockSpec((1, N_MOD, d), lambda i, s: (i // tiles_per_mod, 0, 0)),
                  _lead_spec(lead, (d, d), lambda i, s: (0, first_sec + s)),
                  pl.BlockSpec((None, 2, tm, V7X_LANES),
                               lambda i, s: (jnp.minimum(s, n_tab - 1), 0, i % tiles_per_seq, 0))],
        out_specs=[pl.BlockSpec((tm, d), lambda i, s: (i, jnp.minimum(s, n_sec - 2))),
                   pl.BlockSpec((None, d, tm), lambda i, s: (i, 0, 0))],
        out_shape=[jax.ShapeDtypeStruct((n, (n_sec - 1) * d), BF16),
                   jax.ShapeDtypeStruct((n_tiles, d, tm), BF16)],
        scratch_shapes=[pltpu.VMEM((tm, d), BF16)],
        compiler_params=_params(("parallel", "arbitrary"), est),
        name="qkv_proj",
    )(h, mods, w, tabs)


def _row_blocks(n_rows, block):
    block = min(n_rows, block)
    return [slice(r, r + block) for r in range(0, n_rows, block)]


def _gate_branch_kernel(k_shift, h_ref, mods_ref, w_ref, g_ref, b_ref, o_ref):
    for rows in _row_blocks(h_ref.shape[0], PROJ_ROW_BLOCK["gate"]):
        a = _modulate_bf16(h_ref[rows, :], mods_ref, k_shift)
        v = _gelu(_dot(a, w_ref[...]))
        o_ref[rows, :] = _layer_norm(v, g_ref[...], b_ref[...]).astype(o_ref.dtype)


def _gate_branch_call(h, mods, w, lead, g, b, *, k_shift):
    n, d = h.shape
    nb = mods.shape[0]
    width = w.shape[-1] // 2
    tm = _pick(n // nb, (512, 256, 128))
    n_tiles = n // tm
    tiles_per_mod = n_tiles // nb
    est = 2 * tm * d * 4 + 2 * d * width * 2 + 2 * tm * width * 2 + 4 * tm * width * 4
    return pl.pallas_call(
        functools.partial(_gate_branch_kernel, k_shift),
        grid=(n_tiles,),
        in_specs=[pl.BlockSpec((tm, d), lambda i: (i, 0)),
                  pl.BlockSpec((1, N_MOD, d), lambda i: (i // tiles_per_mod, 0, 0)),
                  _lead_spec(lead, (d, width), lambda i: (0, 1)),
                  pl.BlockSpec((1, width), lambda i: (0, 0)),
                  pl.BlockSpec((1, width), lambda i: (0, 0))],
        out_specs=pl.BlockSpec((tm, width), lambda i: (i, 0)),
        out_shape=jax.ShapeDtypeStruct((n, width), BF16),
        compiler_params=_params(("parallel",), est),
        name="gmlp_gate_branch",
    )(h, mods, w, g.reshape(1, width), b.reshape(1, width))


def _attn_kernel(lam_init, lam_ref, q_ref, k_ref, vt_ref, kc_ref, vct_ref, subln_ref, o_ref,
                 acc_ref, s_ref):
    dk = q_ref.shape[1] // 2
    tq = q_ref.shape[0]
    tk = vt_ref.shape[2]
    q = q_ref[...]
    qm = (q[:, :dk], q[:, dk:])
    nt = (((1,), (1,)), ((), ()))

    def update(carry, blocks):
        for g, (k_blk, _) in enumerate(blocks):
            for mp in range(2):
                s_ref[g, mp, 0:k_blk.shape[0], :] = lax.dot_general(
                    k_blk[:, mp * dk:(mp + 1) * dk], qm[mp], nt, preferred_element_type=F32)
        carry = list(carry)
        for g, (k_blk, vt_chunks) in enumerate(blocks):
            for mp in range(2):
                m_old, l_old = carry[2 * mp], carry[2 * mp + 1]
                st = s_ref[g, mp, 0:k_blk.shape[0], :]
                m_new = jnp.maximum(m_old, jnp.max(st, axis=0, keepdims=True))
                p = jnp.exp2(st - m_new)
                corr = jnp.exp2(m_old - m_new)
                carry[2 * mp] = m_new
                carry[2 * mp + 1] = corr * l_old + jnp.sum(p, axis=0, keepdims=True)
                pt = p.astype(BF16)
                acc = corr * acc_ref[mp]
                lo = 0
                for vt_c in vt_chunks:
                    acc = acc + _dot(vt_c, pt[lo:lo + vt_c.shape[1], :])
                    lo += vt_c.shape[1]
                acc_ref[mp] = acc
        return tuple(carry)

    acc_ref[...] = jnp.zeros_like(acc_ref)
    neg = jnp.full((1, tq), -jnp.inf, F32)
    zero = jnp.zeros((1, tq), F32)
    group, kb = s_ref.shape[0], s_ref.shape[2]
    chunks = kb // tk

    def body(c, carry):
        blocks = []
        for g in range(group):
            blk = c * group + g
            start = pl.multiple_of(blk * kb, kb)
            blocks.append((k_ref[pl.ds(start, kb), :],
                           [vt_ref[blk * chunks + i] for i in range(chunks)]))
        return update(carry, blocks)

    carry = lax.fori_loop(0, vt_ref.shape[0] // (group * chunks), body, (neg, zero, neg, zero))
    tc = vct_ref.shape[2]
    carry = update(carry, [(kc_ref[c * tc:(c + 1) * tc, :], [vct_ref[c]])
                           for c in range(vct_ref.shape[0])])
    _, l0, _, l1 = carry

    lp = lam_ref[...]
    lam = (jnp.exp(jnp.sum(lp[0:1, :] * lp[1:2, :], axis=-1, keepdims=True))
           - jnp.exp(jnp.sum(lp[2:3, :] * lp[3:4, :], axis=-1, keepdims=True)) + lam_init)
    ot = acc_ref[0] * (1.0 / l0) - acc_ref[1] * (lam / l1)
    norm = lax.rsqrt(jnp.mean(ot * ot, axis=0, keepdims=True) + LN_EPS) * (1.0 - lam_init)
    o_ref[...] = (ot * norm * subln_ref[...]).T.astype(o_ref.dtype)


def _attn_call(qk, vt, kc, vct, lam_p, subln, *, n_batch, seq, ctx_len, heads, lam_init):
    d = qk.shape[1] // 2
    dv = d // heads
    tk = vt.shape[2]
    tc = vct.shape[2]
    nk = seq // tk
    nc = ctx_len // tc
    kb = tk * _pick(nk, (ATTN_KEYS // tk, 1))
    group = _pick(seq // kb, (ATTN_GROUP, 1))
    assert nc <= group and tc <= kb, "context keys must fit one score-scratch group"
    tq = _pick(seq, (ATTN_TQ, 128))
    nq = seq // tq
    est = (2 * 2 * tq * dv * 2 + 2 * 2 * seq * dv * 2 + 2 * 2 * ctx_len * dv * 2
           + 2 * tq * dv * 4 + (2 * group + 3) * tq * kb * 4)
    kern = functools.partial(_attn_kernel, lam_init)
    return pl.pallas_call(
        kern,
        grid=(n_batch, heads, nq),
        in_specs=[pl.BlockSpec(lam_p.shape, lambda b, h, i: (0, 0)),
                  pl.BlockSpec((tq, dv), lambda b, h, i: (b * nq + i, h)),
                  pl.BlockSpec((seq, dv), lambda b, h, i: (b, heads + h)),
                  pl.BlockSpec((nk, dv, tk), lambda b, h, i: (b, h, 0)),
                  pl.BlockSpec((ctx_len, dv), lambda b, h, i: (b, h)),
                  pl.BlockSpec((nc, dv, tc), lambda b, h, i: (b, h, 0)),
                  pl.BlockSpec((dv, 1), lambda b, h, i: (0, 0))],
        out_specs=pl.BlockSpec((tq, dv), lambda b, h, i: (b * nq + i, h)),
        out_shape=jax.ShapeDtypeStruct((n_batch * seq, d), BF16),
        scratch_shapes=[pltpu.VMEM((2, dv, tq), F32), pltpu.VMEM((group, 2, kb, tq), F32)],
        compiler_params=_params(("parallel", "parallel", "arbitrary"), est),
        name="diff_attention",
    )(lam_p, qk, qk, vt, kc, vct, subln.reshape(dv, 1))


def _out_proj_kernel(k_gate, alpha, y_ref, h_ref, mods_ref, w_ref, g_ref, b_ref, o_ref):
    for rows in _row_blocks(h_ref.shape[0], PROJ_ROW_BLOCK["out"]):
        y = _dot(y_ref[rows, :], w_ref[...])
        o_ref[rows, :] = _deepnorm(h_ref[rows, :], _mod_row(mods_ref, k_gate), y, alpha,
                                   g_ref[...], b_ref[...])


def _out_proj_call(y, h, mods, w, lead, g, b, *, k_gate, alpha):
    n, d = h.shape
    nb = mods.shape[0]
    tm = _pick(n // nb, (512, 256, 128))
    n_tiles = n // tm
    tiles_per_mod = n_tiles // nb
    est = 2 * tm * d * 2 + 2 * 2 * tm * d * 4 + 2 * d * d * 2 + 4 * tm * d * 4
    return pl.pallas_call(
        functools.partial(_out_proj_kernel, k_gate, alpha),
        grid=(n_tiles,),
        in_specs=[pl.BlockSpec((tm, d), lambda i: (i, 0)),
                  pl.BlockSpec((tm, d), lambda i: (i, 0)),
                  pl.BlockSpec((1, N_MOD, d), lambda i: (i // tiles_per_mod, 0, 0)),
                  _lead_spec(lead, (d, d), lambda i: (0, 0)),
                  pl.BlockSpec((1, d), lambda i: (0, 0)),
                  pl.BlockSpec((1, d), lambda i: (0, 0))],
        out_specs=pl.BlockSpec((tm, d), lambda i: (i, 0)),
        out_shape=jax.ShapeDtypeStruct((n, d), F32),
        compiler_params=_params(("parallel",), est),
        name="out_proj_norm",
    )(y, h, mods, w, g.reshape(1, d), b.reshape(1, d))


def _rope_tables(seq, dk):
    n_freq = dk // 4
    t = jnp.arange(seq)
    inv = ROPE_BASE ** (-jnp.arange(n_freq, dtype=F32) / n_freq)
    ang_r = (t // GRID_W).astype(F32)[:, None] * inv
    ang_c = (t % GRID_W).astype(F32)[:, None] * inv
    cos_t = jnp.concatenate([jnp.cos(ang_r), jnp.cos(ang_c)] * 2, axis=-1)
    sin_t = jnp.concatenate([-jnp.sin(ang_r), -jnp.sin(ang_c), jnp.sin(ang_r), jnp.sin(ang_c)], axis=-1)
    k_tab = jnp.stack([cos_t, sin_t])
    return jnp.stack([k_tab * (dk ** -0.5 * math.log2(math.e)), k_tab])


def _rope_column_layout(w, dk):
    rows, width = w.shape
    w = w.reshape(rows, width // dk, 2, 2, dk // 4)
    return jnp.swapaxes(w, 2, 3).reshape(rows, width)


def kernel(x, c, ctx, c_ctx, ada_w, ada_b, ln_g, ln_b, ffn_wg, ffn_wu, ffn_wd, sc_w_in, sc_conv,
           sc_w_out, da_w_qkv, da_lambda, da_subln, da_w_o, gm_w_in, gm_ln_g, gm_ln_b, gm_w_s,
           gm_b_s, gm_w_out):
    n_batch, seq, d = x.shape
    ctx_len = ctx.shape[1]
    depth = ada_w.shape[0]
    mixer_of_layer = tuple(i % N_MIXERS for i in range(depth))
    last_ctx_layer = max([i for i in range(depth) if mixer_of_layer[i] == 1], default=-1)
    alpha = (2.0 * depth) ** 0.25
    dv = da_subln.shape[-1]
    heads = d // dv
    dk = dv // 2

    h = x.reshape(n_batch * seq, d)
    hc = ctx.reshape(n_batch * ctx_len, d)

    n_cond = n_batch + 1
    cond = jnp.zeros((16 * ((n_cond + 15) // 16), d), F32)
    cond = cond.at[:n_batch].set(c).at[n_batch].set(c_ctx)

    rope_tabs = _rope_tables(seq, dk)
    no_rope_tabs = jnp.zeros((1, 2, ctx_len, dk), F32)

    ffn_f32 = (ffn_wg, ffn_wu, ffn_wd)
    ffn_w = [tuple(w[0, 0].astype(BF16) for w in ffn_f32)]
    conv_w = (sc_w_in.astype(BF16), sc_conv, sc_w_out.astype(BF16))
    qkv_w = da_w_qkv.astype(BF16)
    qkv_w = jnp.concatenate(
        [_rope_column_layout(qkv_w[..., :2 * d].reshape(-1, 2 * d), dk).reshape(qkv_w.shape[0], d, 2 * d),
         qkv_w[..., 2 * d:]], axis=-1)
    attn_wo = da_w_o.astype(BF16)
    gmlp_w = (gm_w_in.astype(BF16), gm_w_s.astype(BF16), gm_b_s[..., None], gm_w_out.astype(BF16))

    for i in range(depth):
        kind = mixer_of_layer[i]
        j = i // N_MIXERS
        ctx_in = i <= last_ctx_layer
        ctx_out = i < last_ctx_layer
        mods_all = _ada(cond, ada_w, ada_b, i).reshape(-1, N_MOD, d)
        md = mods_all[:n_batch]
        mdc = mods_all[n_batch:n_batch + 1]

        def ffn_pair(h_lat, h_ctx, half, k0):
            nxt = (i, 1) if half == 0 else (i + 1, 0)
            args = dict(ln_g=ln_g[i, 2 * half], ln_b=ln_b[i, 2 * half], k_shift=k0,
                        k_gate=k0 + 2, res_scale=0.5, alpha=alpha)
            w_now = ffn_w[0]
            if nxt[0] < depth:
                h_lat, w_next = _fused_call("ffn", h_lat, md, w_now, (), cast=(ffn_f32, nxt), **args)
                ffn_w[0] = tuple(w_next)
            else:
                h_lat = _fused_call("ffn", h_lat, md, w_now, (), **args)
            if h_ctx is not None:
                h_ctx = _fused_call("ffn", h_ctx, mdc, w_now, (), **args)
            return h_lat, h_ctx

        h, hc_new = ffn_pair(h, hc if ctx_in else None, 0, 0)
        if ctx_in:
            hc = hc_new

        if kind == 0:
            conv = functools.partial(_fused_call, "conv", weights=conv_w, lead=(j,),
                                     ln_g=ln_g[i, 1], ln_b=ln_b[i, 1], k_shift=3, k_gate=5,
                                     res_scale=1.0, alpha=alpha)
            h = conv(h, md, period=GRID_W)
            if ctx_out:
                hc = conv(hc, mdc, period=ctx_len)
        elif kind == 1:
            lam_init = 0.8 - 0.6 * math.exp(-0.3 * i)
            qk, vt = _qkv_call(h, md, qkv_w, (j,), rope_tabs, k_shift=3, n_rope=2, seq=seq)
            kc, vct = _qkv_call(hc, mdc, qkv_w, (j,), no_rope_tabs, k_shift=3, n_rope=0,
                                seq=ctx_len, first_sec=1)
            o = _attn_call(qk, vt, kc, vct, da_lambda[j], da_subln[j], n_batch=n_batch, seq=seq,
                           ctx_len=ctx_len, heads=heads, lam_init=lam_init)
            h = _out_proj_call(o, h, md, attn_wo, (j,), ln_g[i, 1], ln_b[i, 1], k_gate=5,
                               alpha=alpha)
            assert not ctx_out, "context-side attention output is not implemented"
        else:
            def gmlp(hh, mm):
                vn = _gate_branch_call(hh, mm, gmlp_w[0], (j,), gm_ln_g[j], gm_ln_b[j], k_shift=3)
                return _fused_call("gmlp", hh, mm, gmlp_w, (j,), ln_g[i, 1], ln_b[i, 1],
                                   k_shift=3, k_gate=5, res_scale=1.0, alpha=alpha, vn=vn)

            h = gmlp(h, md)
            if ctx_out:
                hc = gmlp(hc, mdc)

        h, hc_new = ffn_pair(h, hc if ctx_out else None, 1, 6)
        if ctx_out:
            hc = hc_new
    return h.reshape(n_batch, seq, d)
```

```python
import functools
import math

import jax
import jax.numpy as jnp
from jax import lax
from jax.experimental import pallas as pl
from jax.experimental.pallas import tpu as pltpu

GRID_W = 64
CHUNK = 128
N_MOD = 9
N_MIXERS = 3
ROPE_BASE = 10000.0
LN_EPS = 1e-5

V7X_LANES = 128
V7X_VMEM_BYTES = 64 * 1024 * 1024
V7X_VMEM_CAP = V7X_VMEM_BYTES - 6 * 1024 * 1024

BF16 = jnp.bfloat16
F32 = jnp.float32

ATTN_TQ = 1024
ATTN_KEYS = 2048
ATTN_GROUP = 1
FUSED_CHUNK = 512
CAST_ROWS = 16
FUSED_TILE = {"ffn": 1024, "conv": 1024, "gmlp": 512}
FUSED_ROW_BLOCK = {"ffn": 1024, "conv": 512, "gmlp": 256}
PROJ_ROW_BLOCK = {"gate": 512, "out": 256}


def _params(semantics, vmem_estimate):
    limit = min(V7X_VMEM_CAP, max(32 * 1024 * 1024, int(vmem_estimate * 1.3)))
    return pltpu.CompilerParams(dimension_semantics=semantics, vmem_limit_bytes=limit)


def _pick(n, candidates):
    for c in candidates:
        if n % c == 0:
            return c
    return n


def _mod_row(mods_ref, k):
    return mods_ref[0, k:k + 1, :]


def _modulate_bf16(h, mods_ref, k_shift):
    shift = _mod_row(mods_ref, k_shift)
    scale = _mod_row(mods_ref, k_shift + 1)
    return (h * (1.0 + scale) + shift).astype(BF16)


def _layer_norm(x, g, b, eps=LN_EPS):
    mu = jnp.mean(x, axis=-1, keepdims=True)
    xc = x - mu
    var = jnp.mean(xc * xc, axis=-1, keepdims=True)
    return xc * lax.rsqrt(var + eps) * g + b


def _deepnorm(h, delta_scale, delta, alpha, g, b):
    return _layer_norm(h + (delta_scale * (1.0 / alpha)) * delta, g, b, LN_EPS / (alpha * alpha))


def _silu(x):
    return x / (1.0 + jnp.exp(-x))


def _gelu(x):
    return 0.5 * x * (1.0 + lax.erf(x * math.sqrt(0.5)))


def _dot(a, b):
    return jnp.dot(a, b, preferred_element_type=F32)


def _ada_kernel(cond_ref, w_ref, b_ref, o_ref):
    a = _silu(cond_ref[...]).astype(BF16)
    o_ref[...] = _dot(a, w_ref[...].astype(BF16)) + b_ref[...]


def _ada(cond, w, b):
    m, d = cond.shape
    n_layers, _, n = w.shape
    tn = _pick(n, (1024, 512, 256, 128))
    est = 2 * d * tn * 4 + d * tn * 2 + 4 * m * (d + tn) * 4
    return pl.pallas_call(
        _ada_kernel,
        grid=(n_layers, n // tn),
        in_specs=[pl.BlockSpec((m, d), lambda l, j: (0, 0)),
                  pl.BlockSpec((None, d, tn), lambda l, j: (l, 0, j)),
                  pl.BlockSpec((None, 1, tn), lambda l, j: (l, 0, j))],
        out_specs=pl.BlockSpec((None, m, tn), lambda l, j: (l, 0, j)),
        out_shape=jax.ShapeDtypeStruct((n_layers, m, n), F32),
        compiler_params=_params(("parallel", "parallel"), est),
        name="ada_mod",
    )(cond, w, b.reshape(n_layers, 1, n))


def _fused_kernel(kind, k_shift, k_gate, res_scale, alpha, period, n_steps, n_cast, *refs):
    h_ref, mods_ref = refs[0], refs[1]
    n_in = len(refs) - 2 - n_cast
    w = refs[2:n_in - 2 - n_cast]
    lng_ref, lnb_ref = refs[n_in - 2 - n_cast:n_in - n_cast]
    cast_in = refs[n_in - n_cast:n_in]
    o_ref = refs[n_in]
    cast_out = refs[n_in + 1:n_in + 1 + n_cast]
    a_ref = refs[-1]
    acc_ref = o_ref
    wd_ref = w[-1]
    tm = h_ref.shape[0]


    def row_blocks(size):
        rb = min(tm, max(size, period))
        return [slice(r, r + rb) for r in range(0, tm, rb)]

    def up(a, rows):
        if kind == "ffn":
            return _dot(a, w[0][...]), _dot(a, w[1][...])
        if kind == "conv":
            return _dot(a, w[0][...]), _dot(a, w[1][...]), _dot(a, w[2][...])
        vn_ref, ws_ref, bs_ref = w[1], w[2], w[3]
        gw = w[0].shape[1] // ws_ref.shape[0]
        cols = []
        for g in range(ws_ref.shape[0]):
            parts = [_dot(ws_ref[g], vn_ref[c:c + CHUNK, g * gw:(g + 1) * gw]) + bs_ref[g]
                     for c in range(rows.start, rows.stop, CHUNK)]
            cols.append(jnp.concatenate(parts, axis=0))
        return _dot(a, w[0][...]), jnp.concatenate(cols, axis=1)

    def hidden(pre):
        if kind == "ffn":
            g, u = pre
            return _silu(g) * u
        if kind == "conv":
            b, c, v = pre
            z = c * v
            pos = lax.broadcasted_iota(jnp.int32, z.shape, 0) & (period - 1)
            z_prev = jnp.where(pos == 0, 0.0, pltpu.roll(z, 1, 0))
            z_next = jnp.where(pos == period - 1, 0.0, pltpu.roll(z, z.shape[0] - 1, 0))
            cw = w[3][...]
            return b * (cw[0:1, :] * z_prev + cw[1:2, :] * z + cw[2:3, :] * z_next)
        u, s = pre
        return _gelu(u) * s

    def step(first, last):
        for src, dst in zip(cast_in, cast_out):
            dst[...] = src[...].astype(dst.dtype)

        def start(rows):
            if first:
                a = _modulate_bf16(h_ref[rows, :], mods_ref, k_shift)
                if not last:
                    a_ref[rows, :] = a
            else:
                a = a_ref[rows, :]
            return up(a, rows)

        def finish(rows, pre):
            acc = _dot(hidden(pre).astype(BF16), wd_ref[...])
            if not first:
                acc = acc_ref[rows, :] + acc
            if last:
                o_ref[rows, :] = _deepnorm(h_ref[rows, :], res_scale * _mod_row(mods_ref, k_gate),
                                           acc, alpha, lng_ref[...], lnb_ref[...])
            else:
                acc_ref[rows, :] = acc

        pending = None
        for rows in row_blocks(FUSED_ROW_BLOCK[kind]):
            pre = start(rows)
            if pending is not None:
                finish(*pending)
            pending = (rows, pre)
        finish(*pending)

    if n_steps == 1:
        step(True, True)
        return
    j = pl.program_id(1)
    pl.when(j == 0)(functools.partial(step, True, False))
    if n_steps > 2:
        pl.when(jnp.logical_and(j > 0, j < n_steps - 1))(functools.partial(step, False, False))
    pl.when(j == n_steps - 1)(functools.partial(step, False, True))


def _lead_spec(lead, block, index_fn):
    lead = tuple(lead)
    return pl.BlockSpec((None,) * len(lead) + tuple(block),
                        lambda *g: lead + tuple(index_fn(*g)))


def _fused_call(kind, h, mods, weights, lead, ln_g, ln_b, *, k_shift, k_gate, res_scale, alpha,
                period=GRID_W, vn=None, cast=None):
    n, d = h.shape
    nb = mods.shape[0]
    tm = _pick(n // nb, (FUSED_TILE[kind], 512, 256, 128))
    n_tiles = n // tm
    tiles_per_mod = n_tiles // nb

    row = lambda i, j: (i, 0)
    col_blk = lambda i, j: (0, j)
    row_blk = lambda i, j: (j, 0)
    common_in = [pl.BlockSpec((tm, d), row),
                 pl.BlockSpec((1, N_MOD, d), lambda i, j: (i // tiles_per_mod, 0, 0))]
    if kind == "ffn":
        wg, wu, wd = weights
        f = wg.shape[-1]
        tc = _pick(f, (FUSED_CHUNK, 256, 128))
        n_chunks = f // tc
        w_in = [_lead_spec(lead, (d, tc), col_blk), _lead_spec(lead, (d, tc), col_blk),
                _lead_spec(lead, (tc, d), row_blk)]
        w_args = [wg, wu, wd]
        w_bytes = 3 * d * tc * 2
    elif kind == "conv":
        assert period & (period - 1) == 0 and tm % period == 0, (tm, period)
        w_in3, cw, wd = weights
        tc = _pick(d, (FUSED_CHUNK, 256, 128))
        n_chunks = d // tc
        w_in = [_lead_spec(lead, (d, tc), col_blk),
                _lead_spec(lead, (d, tc), lambda i, j: (0, n_chunks + j)),
                _lead_spec(lead, (d, tc), lambda i, j: (0, 2 * n_chunks + j)),
                _lead_spec(lead, (3, tc), col_blk),
                _lead_spec(lead, (tc, d), row_blk)]
        w_args = [w_in3, w_in3, w_in3, cw, wd]
        w_bytes = 4 * d * tc * 2
    else:
        w_in2, ws, bs, wd = weights
        groups = ws.shape[-3]
        gw = wd.shape[-2] // groups
        per_step = max(g for g in range(1, groups + 1) if groups % g == 0 and g * gw <= 1024)
        tc = per_step * gw
        n_chunks = groups // per_step
        w_in = [_lead_spec(lead, (d, tc), col_blk),
                pl.BlockSpec((tm, tc), lambda i, j: (i, j)),
                _lead_spec(lead, (per_step, CHUNK, CHUNK), lambda i, j: (j, 0, 0)),
                _lead_spec(lead, (per_step, CHUNK, 1), lambda i, j: (j, 0, 0)),
                _lead_spec(lead, (tc, d), row_blk)]
        w_args = [w_in2, vn, ws, bs, wd]
        w_bytes = 2 * d * tc * 2 + tm * tc * 2
    vec = pl.BlockSpec((1, d), lambda i, j: (0, 0))
    rb = min(tm, FUSED_ROW_BLOCK[kind])
    est = (2 * 2 * tm * d * 4
           + tm * d * 2
           + 2 * w_bytes
           + 6 * rb * tc * 4
           + 2 * rb * d * 4)
    cast_arrays, cast_lead = cast if cast is not None else ((), ())
    n_steps = n_tiles * n_chunks
    cast_in, cast_out, cast_shapes = [], [], []
    for arr in cast_arrays:
        rows, cols = arr.shape[-2:]
        if (cols % (n_chunks * V7X_LANES) == 0 and rows % (n_tiles * CAST_ROWS) == 0):
            blk, idx = (rows // n_tiles, cols // n_chunks), (lambda i, j: (i, j))
        else:
            assert rows % (n_steps * CAST_ROWS) == 0, "cast array does not split over the grid"
            blk, idx = (rows // n_steps, cols), (lambda i, j: (i * n_chunks + j, 0))
        cast_in.append(_lead_spec(cast_lead, blk, idx))
        cast_out.append(pl.BlockSpec(blk, idx))
        cast_shapes.append(jax.ShapeDtypeStruct((rows, cols), BF16))
        est += 2 * blk[0] * blk[1] * (4 + 2)
    kern = functools.partial(_fused_kernel, kind, k_shift, k_gate, res_scale, alpha, period,
                             n_chunks, len(cast_arrays))
    outs = pl.pallas_call(
        kern,
        grid=(n_tiles, n_chunks),
        in_specs=common_in + w_in + [vec, vec] + cast_in,
        out_specs=[pl.BlockSpec((tm, d), row)] + cast_out,
        out_shape=[jax.ShapeDtypeStruct((n, d), F32)] + cast_shapes,
        scratch_shapes=[pltpu.VMEM((tm, d), BF16)],
        compiler_params=_params(("parallel", "arbitrary"), est),
        name="fused_" + kind,
    )(h, mods, *w_args, ln_g.reshape(1, d), ln_b.reshape(1, d), *cast_arrays)
    return outs[0] if cast is None else (outs[0], outs[1:])


def _qkv_kernel(k_shift, n_rope, h_ref, mods_ref, w_ref, tab_ref, o_ref, vt_ref, a_ref):
    s = pl.program_id(1)
    n_sec = pl.num_programs(1)
    cos_ref, sin_ref = tab_ref.at[0], tab_ref.at[1]

    @pl.when(s == 0)
    def _():
        a_ref[...] = _modulate_bf16(h_ref[...], mods_ref, k_shift)

    def project(rope, transposed):
        width = w_ref.shape[1]
        slab = _pick(width, (512, 256, 128))
        for c in range(width // slab):
            y = _dot(a_ref[...], w_ref[:, c * slab:(c + 1) * slab])
            if transposed:
                vt_ref[c * slab:(c + 1) * slab, :] = y.T.astype(vt_ref.dtype)
                continue
            for r in range(slab // V7X_LANES):
                yr = y[:, r * V7X_LANES:(r + 1) * V7X_LANES]
                if rope:
                    yr = yr * cos_ref[...] + pltpu.roll(yr, V7X_LANES // 2, 1) * sin_ref[...]
                lo = c * slab + r * V7X_LANES
                o_ref[:, lo:lo + V7X_LANES] = yr.astype(o_ref.dtype)

    if n_rope > 0:
        @pl.when(s < n_rope)
        def _():
            project(True, False)

    @pl.when(jnp.logical_and(s >= n_rope, s < n_sec - 1))
    def _():
        project(False, False)

    @pl.when(s == n_sec - 1)
    def _():
        project(False, True)


def _qkv_call(h, mods, w, lead, tabs, *, k_shift, n_rope, seq, first_sec=0):
    n, d = h.shape
    n_tab = tabs.shape[0]
    nb = mods.shape[0]
    n_sec = w.shape[-1] // d - first_sec
    tm = _pick(min(n // nb, seq), (512, 256, 128))
    n_tiles = n // tm
    tiles_per_mod = n_tiles // nb
    tiles_per_seq = seq // tm
    est = (2 * tm * d * 4 + tm * d * 2 + 2 * d * d * 2 + 2 * 2 * tm * d * 2
           + 8 * tm * V7X_LANES * 4 + 4 * tm * 512 * 4)
    kern = functools.partial(_qkv_kernel, k_shift, n_rope)
    return pl.pallas_call(
        kern,
        grid=(n_tiles, n_sec),
        in_specs=[pl.BlockSpec((tm, d), lambda i, s: (i, 0)),
                  pl.BlockSpec((1, N_MOD, d), lambda i, s: (i // tiles_per_mod, 0, 0)),
                  _lead_spec(lead, (d, d), lambda i, s: (0, first_sec + s)),
                  pl.BlockSpec((None, 2, tm, V7X_LANES),
                               lambda i, s: (jnp.minimum(s, n_tab - 1), 0, i % tiles_per_seq, 0))],
        out_specs=[pl.BlockSpec((tm, d), lambda i, s: (i, jnp.minimum(s, n_sec - 2))),
                   pl.BlockSpec((None, d, tm), lambda i, s: (i, 0, 0))],
        out_shape=[jax.ShapeDtypeStruct((n, (n_sec - 1) * d), BF16),
                   jax.ShapeDtypeStruct((n_tiles, d, tm), BF16)],
        scratch_shapes=[pltpu.VMEM((tm, d), BF16)],
        compiler_params=_params(("parallel", "arbitrary"), est),
        name="qkv_proj",
    )(h, mods, w, tabs)


def _row_blocks(n_rows, block):
    block = min(n_rows, block)
    return [slice(r, r + block) for r in range(0, n_rows, block)]


def _gate_branch_kernel(k_shift, h_ref, mods_ref, w_ref, g_ref, b_ref, o_ref):
    for rows in _row_blocks(h_ref.shape[0], PROJ_ROW_BLOCK["gate"]):
        a = _modulate_bf16(h_ref[rows, :], mods_ref, k_shift)
        v = _gelu(_dot(a, w_ref[...]))
        o_ref[rows, :] = _layer_norm(v, g_ref[...], b_ref[...]).astype(o_ref.dtype)


def _gate_branch_call(h, mods, w, lead, g, b, *, k_shift):
    n, d = h.shape
    nb = mods.shape[0]
    width = w.shape[-1] // 2
    tm = _pick(n // nb, (512, 256, 128))
    n_tiles = n // tm
    tiles_per_mod = n_tiles // nb
    est = 2 * tm * d * 4 + 2 * d * width * 2 + 2 * tm * width * 2 + 4 * tm * width * 4
    return pl.pallas_call(
        functools.partial(_gate_branch_kernel, k_shift),
        grid=(n_tiles,),
        in_specs=[pl.BlockSpec((tm, d), lambda i: (i, 0)),
                  pl.BlockSpec((1, N_MOD, d), lambda i: (i // tiles_per_mod, 0, 0)),
                  _lead_spec(lead, (d, width), lambda i: (0, 1)),
                  pl.BlockSpec((1, width), lambda i: (0, 0)),
                  pl.BlockSpec((1, width), lambda i: (0, 0))],
        out_specs=pl.BlockSpec((tm, width), lambda i: (i, 0)),
        out_shape=jax.ShapeDtypeStruct((n, width), BF16),
        compiler_params=_params(("parallel",), est),
        name="gmlp_gate_branch",
    )(h, mods, w, g.reshape(1, width), b.reshape(1, width))


def _attn_kernel(lam_init, lam_ref, q_ref, k_ref, vt_ref, kc_ref, vct_ref, subln_ref, o_ref,
                 acc_ref, s_ref):
    dk = q_ref.shape[1] // 2
    tq = q_ref.shape[0]
    tk = vt_ref.shape[2]
    q = q_ref[...]
    qm = (q[:, :dk], q[:, dk:])
    nt = (((1,), (1,)), ((), ()))

    def update(carry, blocks):
        for g, (k_blk, _) in enumerate(blocks):
            for mp in range(2):
                s_ref[g, mp, 0:k_blk.shape[0], :] = lax.dot_general(
                    k_blk[:, mp * dk:(mp + 1) * dk], qm[mp], nt, preferred_element_type=F32)
        carry = list(carry)
        for g, (k_blk, vt_chunks) in enumerate(blocks):
            for mp in range(2):
                m_old, l_old = carry[2 * mp], carry[2 * mp + 1]
                st = s_ref[g, mp, 0:k_blk.shape[0], :]
                m_new = jnp.maximum(m_old, jnp.max(st, axis=0, keepdims=True))
                p = jnp.exp2(st - m_new)
                corr = jnp.exp2(m_old - m_new)
                carry[2 * mp] = m_new
                carry[2 * mp + 1] = corr * l_old + jnp.sum(p, axis=0, keepdims=True)
                pt = p.astype(BF16)
                acc = corr * acc_ref[mp]
                lo = 0
                for vt_c in vt_chunks:
                    acc = acc + _dot(vt_c, pt[lo:lo + vt_c.shape[1], :])
                    lo += vt_c.shape[1]
                acc_ref[mp] = acc
        return tuple(carry)

    acc_ref[...] = jnp.zeros_like(acc_ref)
    neg = jnp.full((1, tq), -jnp.inf, F32)
    zero = jnp.zeros((1, tq), F32)
    group, kb = s_ref.shape[0], s_ref.shape[2]
    chunks = kb // tk

    def body(c, carry):
        blocks = []
        for g in range(group):
            blk = c * group + g
            start = pl.multiple_of(blk * kb, kb)
            blocks.append((k_ref[pl.ds(start, kb), :],
                           [vt_ref[blk * chunks + i] for i in range(chunks)]))
        return update(carry, blocks)

    carry = lax.fori_loop(0, vt_ref.shape[0] // (group * chunks), body, (neg, zero, neg, zero))
    tc = vct_ref.shape[2]
    carry = update(carry, [(kc_ref[c * tc:(c + 1) * tc, :], [vct_ref[c]])
                           for c in range(vct_ref.shape[0])])
    _, l0, _, l1 = carry

    lp = lam_ref[...]
    lam = (jnp.exp(jnp.sum(lp[0:1, :] * lp[1:2, :], axis=-1, keepdims=True))
           - jnp.exp(jnp.sum(lp[2:3, :] * lp[3:4, :], axis=-1, keepdims=True)) + lam_init)
    ot = acc_ref[0] * (1.0 / l0) - acc_ref[1] * (lam / l1)
    norm = lax.rsqrt(jnp.mean(ot * ot, axis=0, keepdims=True) + LN_EPS) * (1.0 - lam_init)
    o_ref[...] = (ot * norm * subln_ref[...]).T.astype(o_ref.dtype)


def _attn_call(qk, vt, kc, vct, lam_p, subln, *, n_batch, seq, ctx_len, heads, lam_init):
    d = qk.shape[1] // 2
    dv = d // heads
    tk = vt.shape[2]
    tc = vct.shape[2]
    nk = seq // tk
    nc = ctx_len // tc
    kb = tk * _pick(nk, (ATTN_KEYS // tk, 1))
    group = _pick(seq // kb, (ATTN_GROUP, 1))
    assert nc <= group and tc <= kb, "context keys must fit one score-scratch group"
    tq = _pick(seq, (ATTN_TQ, 128))
    nq = seq // tq
    est = (2 * 2 * tq * dv * 2 + 2 * 2 * seq * dv * 2 + 2 * 2 * ctx_len * dv * 2
           + 2 * tq * dv * 4 + (2 * group + 3) * tq * kb * 4)
    kern = functools.partial(_attn_kernel, lam_init)
    return pl.pallas_call(
        kern,
        grid=(n_batch, heads, nq),
        in_specs=[pl.BlockSpec(lam_p.shape, lambda b, h, i: (0, 0)),
                  pl.BlockSpec((tq, dv), lambda b, h, i: (b * nq + i, h)),
                  pl.BlockSpec((seq, dv), lambda b, h, i: (b, heads + h)),
                  pl.BlockSpec((nk, dv, tk), lambda b, h, i: (b, h, 0)),
                  pl.BlockSpec((ctx_len, dv), lambda b, h, i: (b, h)),
                  pl.BlockSpec((nc, dv, tc), lambda b, h, i: (b, h, 0)),
                  pl.BlockSpec((dv, 1), lambda b, h, i: (0, 0))],
        out_specs=pl.BlockSpec((tq, dv), lambda b, h, i: (b * nq + i, h)),
        out_shape=jax.ShapeDtypeStruct((n_batch * seq, d), BF16),
        scratch_shapes=[pltpu.VMEM((2, dv, tq), F32), pltpu.VMEM((group, 2, kb, tq), F32)],
        compiler_params=_params(("parallel", "parallel", "arbitrary"), est),
        name="diff_attention",
    )(lam_p, qk, qk, vt, kc, vct, subln.reshape(dv, 1))


def _out_proj_kernel(k_gate, alpha, y_ref, h_ref, mods_ref, w_ref, g_ref, b_ref, o_ref):
    for rows in _row_blocks(h_ref.shape[0], PROJ_ROW_BLOCK["out"]):
        y = _dot(y_ref[rows, :], w_ref[...])
        o_ref[rows, :] = _deepnorm(h_ref[rows, :], _mod_row(mods_ref, k_gate), y, alpha,
                                   g_ref[...], b_ref[...])


def _out_proj_call(y, h, mods, w, lead, g, b, *, k_gate, alpha):
    n, d = h.shape
    nb = mods.shape[0]
    tm = _pick(n // nb, (512, 256, 128))
    n_tiles = n // tm
    tiles_per_mod = n_tiles // nb
    est = 2 * tm * d * 2 + 2 * 2 * tm * d * 4 + 2 * d * d * 2 + 4 * tm * d * 4
    return pl.pallas_call(
        functools.partial(_out_proj_kernel, k_gate, alpha),
        grid=(n_tiles,),
        in_specs=[pl.BlockSpec((tm, d), lambda i: (i, 0)),
                  pl.BlockSpec((tm, d), lambda i: (i, 0)),
                  pl.BlockSpec((1, N_MOD, d), lambda i: (i // tiles_per_mod, 0, 0)),
                  _lead_spec(lead, (d, d), lambda i: (0, 0)),
                  pl.BlockSpec((1, d), lambda i: (0, 0)),
                  pl.BlockSpec((1, d), lambda i: (0, 0))],
        out_specs=pl.BlockSpec((tm, d), lambda i: (i, 0)),
        out_shape=jax.ShapeDtypeStruct((n, d), F32),
        compiler_params=_params(("parallel",), est),
        name="out_proj_norm",
    )(y, h, mods, w, g.reshape(1, d), b.reshape(1, d))


def _rope_tables(seq, dk):
    n_freq = dk // 4
    t = jnp.arange(seq)
    inv = ROPE_BASE ** (-jnp.arange(n_freq, dtype=F32) / n_freq)
    ang_r = (t // GRID_W).astype(F32)[:, None] * inv
    ang_c = (t % GRID_W).astype(F32)[:, None] * inv
    cos_t = jnp.concatenate([jnp.cos(ang_r), jnp.cos(ang_c)] * 2, axis=-1)
    sin_t = jnp.concatenate([-jnp.sin(ang_r), -jnp.sin(ang_c), jnp.sin(ang_r), jnp.sin(ang_c)], axis=-1)
    k_tab = jnp.stack([cos_t, sin_t])
    return jnp.stack([k_tab * (dk ** -0.5 * math.log2(math.e)), k_tab])


def _rope_column_layout(w, dk):
    rows, width = w.shape
    w = w.reshape(rows, width // dk, 2, 2, dk // 4)
    return jnp.swapaxes(w, 2, 3).reshape(rows, width)


def kernel(x, c, ctx, c_ctx, ada_w, ada_b, ln_g, ln_b, ffn_wg, ffn_wu, ffn_wd, sc_w_in, sc_conv,
           sc_w_out, da_w_qkv, da_lambda, da_subln, da_w_o, gm_w_in, gm_ln_g, gm_ln_b, gm_w_s,
           gm_b_s, gm_w_out):
    n_batch, seq, d = x.shape
    ctx_len = ctx.shape[1]
    depth = ada_w.shape[0]
    mixer_of_layer = tuple(i % N_MIXERS for i in range(depth))
    last_ctx_layer = max([i for i in range(depth) if mixer_of_layer[i] == 1], default=-1)
    alpha = (2.0 * depth) ** 0.25
    dv = da_subln.shape[-1]
    heads = d // dv
    dk = dv // 2

    h = x.reshape(n_batch * seq, d)
    hc = ctx.reshape(n_batch * ctx_len, d)

    n_cond = n_batch + 1
    cond = jnp.zeros((16 * ((n_cond + 15) // 16), d), F32)
    cond = cond.at[:n_batch].set(c).at[n_batch].set(c_ctx)

    rope_tabs = _rope_tables(seq, dk)
    no_rope_tabs = jnp.zeros((1, 2, ctx_len, dk), F32)

    ffn_f32 = (ffn_wg, ffn_wu, ffn_wd)
    ffn_w = [tuple(w[0, 0].astype(BF16) for w in ffn_f32)]
    conv_w = (sc_w_in.astype(BF16), sc_conv, sc_w_out.astype(BF16))
    qkv_w = da_w_qkv.astype(BF16)
    qkv_w = jnp.concatenate(
        [_rope_column_layout(qkv_w[..., :2 * d].reshape(-1, 2 * d), dk).reshape(qkv_w.shape[0], d, 2 * d),
         qkv_w[..., 2 * d:]], axis=-1)
    attn_wo = da_w_o.astype(BF16)
    gmlp_w = (gm_w_in.astype(BF16), gm_w_s.astype(BF16), gm_b_s[..., None], gm_w_out.astype(BF16))

    mods_layers = _ada(cond, ada_w, ada_b).reshape(depth, -1, N_MOD, d)

    for i in range(depth):
        kind = mixer_of_layer[i]
        j = i // N_MIXERS
        ctx_in = i <= last_ctx_layer
        ctx_out = i < last_ctx_layer
        mods_all = mods_layers[i]
        md = mods_all[:n_batch]
        mdc = mods_all[n_batch:n_batch + 1]

        def ffn_pair(h_lat, h_ctx, half, k0):
            nxt = (i, 1) if half == 0 else (i + 1, 0)
            args = dict(ln_g=ln_g[i, 2 * half], ln_b=ln_b[i, 2 * half], k_shift=k0,
                        k_gate=k0 + 2, res_scale=0.5, alpha=alpha)
            w_now = ffn_w[0]
            if nxt[0] < depth:
                h_lat, w_next = _fused_call("ffn", h_lat, md, w_now, (), cast=(ffn_f32, nxt), **args)
                ffn_w[0] = tuple(w_next)
            else:
                h_lat = _fused_call("ffn", h_lat, md, w_now, (), **args)
            if h_ctx is not None:
                h_ctx = _fused_call("ffn", h_ctx, mdc, w_now, (), **args)
            return h_lat, h_ctx

        h, hc_new = ffn_pair(h, hc if ctx_in else None, 0, 0)
        if ctx_in:
            hc = hc_new

        if kind == 0:
            conv = functools.partial(_fused_call, "conv", weights=conv_w, lead=(j,),
                                     ln_g=ln_g[i, 1], ln_b=ln_b[i, 1], k_shift=3, k_gate=5,
                                     res_scale=1.0, alpha=alpha)
            h = conv(h, md, period=GRID_W)
            if ctx_out:
                hc = conv(hc, mdc, period=ctx_len)
        elif kind == 1:
            lam_init = 0.8 - 0.6 * math.exp(-0.3 * i)
            qk, vt = _qkv_call(h, md, qkv_w, (j,), rope_tabs, k_shift=3, n_rope=2, seq=seq)
            kc, vct = _qkv_call(hc, mdc, qkv_w, (j,), no_rope_tabs, k_shift=3, n_rope=0,
                                seq=ctx_len, first_sec=1)
            o = _attn_call(qk, vt, kc, vct, da_lambda[j], da_subln[j], n_batch=n_batch, seq=seq,
                           ctx_len=ctx_len, heads=heads, lam_init=lam_init)
            h = _out_proj_call(o, h, md, attn_wo, (j,), ln_g[i, 1], ln_b[i, 1], k_gate=5,
                               alpha=alpha)
            assert not ctx_out, "context-side attention output is not implemented"
        else:
            def gmlp(hh, mm):
                vn = _gate_branch_call(hh, mm, gmlp_w[0], (j,), gm_ln_g[j], gm_ln_b[j], k_shift=3)
                return _fused_call("gmlp", hh, mm, gmlp_w, (j,), ln_g[i, 1], ln_b[i, 1],
                                   k_shift=3, k_gate=5, res_scale=1.0, alpha=alpha, vn=vn)

            h = gmlp(h, md)
            if ctx_out:
                hc = gmlp(hc, mdc)

        h, hc_new = ffn_pair(h, hc if ctx_out else None, 1, 6)
        if ctx_out:
            hc = hc_new
    return h.reshape(n_batch, seq, d)
```
